```python
import math
import jax, jax.numpy as jnp
from jax import lax
import numpy as np

D_MODEL = 2048
BATCH = 4
SEQ = 8192
DEPTH = 4

GRID_W = 64
CTX_LEN = 256
CHUNK = 128
EPS = 1e-6

RET_HEADS = 8
RET_DK = D_MODEL // 16
RET_DV = D_MODEL // 8
RET_QK_W = RET_HEADS * RET_DK
RET_V_W = RET_HEADS * RET_DV
ROPE_BASE = 10000.0

SSM_INNER = 2 * D_MODEL
SSM_HEADDIM = 64
SSM_HEADS = SSM_INNER // SSM_HEADDIM
SSM_GROUPS = 8
SSM_HPG = SSM_HEADS // SSM_GROUPS
SSM_STATE = 128
SSM_BC_W = SSM_GROUPS * SSM_STATE
SSM_XBC_W = SSM_INNER + 2 * SSM_BC_W
CONV_K = 5

N_BRANCH = 2
IN_SIZES = (RET_QK_W, RET_QK_W, RET_V_W, RET_V_W, SSM_INNER, SSM_XBC_W, 2 * SSM_HEADS, N_BRANCH * D_MODEL)
IN_W = RET_QK_W * 2 + RET_V_W * 2 + SSM_INNER + SSM_XBC_W + 2 * SSM_HEADS + N_BRANCH * D_MODEL

D_FF = ((8 * D_MODEL // 3 + 255) // 256) * 256
N_EXPERTS = 8
TOP_K = 2
MOE_D_FF = 2 * D_MODEL

kernel_name = "hybrid_retention_ssd_moe_dit"


def rmsnorm(x, w):
    xf = x.astype(jnp.float32)
    y = xf * lax.rsqrt(jnp.mean(xf * xf, axis=-1, keepdims=True) + EPS)
    return y.astype(x.dtype) * w


def modulate(h, shift, scale):
    return h * (1 + scale) + shift


def rope_axis(x, pos):
    half = x.shape[-1] // 2
    inv = ROPE_BASE ** (-jnp.arange(half, dtype=jnp.float32) / half)
    ang = pos.astype(jnp.float32)[:, None] * inv[None, :]
    cos = jnp.cos(ang)[None, :, None, :].astype(x.dtype)
    sin = jnp.sin(ang)[None, :, None, :].astype(x.dtype)
    x1, x2 = x[..., :half], x[..., half:]
    return jnp.concatenate([x1 * cos - x2 * sin, x1 * sin + x2 * cos], axis=-1)


def rope_2d(x, row, col):
    d = x.shape[-1] // 2
    return jnp.concatenate([rope_axis(x[..., :d], row), rope_axis(x[..., d:], col)], axis=-1)


def dwconv_centred(x, w, b):
    ch = x.shape[-1]
    y = lax.conv_general_dilated(x, w[:, None, :].astype(x.dtype), window_strides=(1,),
                                 padding=[(CONV_K // 2, CONV_K // 2)],
                                 dimension_numbers=("NWC", "WIO", "NWC"), feature_group_count=ch)
    return y + b


def chunked_linear_scan(q, k, v, log_a, s0):
    out_dtype = v.dtype
    q, k, v, log_a = (a.astype(jnp.float32) for a in (q, k, v, log_a))
    b, t = q.shape[0], q.shape[1]
    nc = t // CHUNK

    def to_chunks(a):
        return jnp.moveaxis(a.reshape(b, nc, CHUNK, *a.shape[2:]), 1, 0)

    lower = jnp.tril(jnp.ones((CHUNK, CHUNK), dtype=bool))

    def step(s, inp):
        qc, kc, vc, lac = inp
        cum = jnp.cumsum(lac, axis=1)
        cum_t = jnp.moveaxis(cum, 1, -1)
        seg = cum_t[..., :, None] - cum_t[..., None, :]
        decay = jnp.exp(jnp.where(lower, seg, -jnp.inf))
        scores = jnp.einsum("bigd,bjgd->bgij", qc, kc)
        y = jnp.einsum("bghij,bjghv->bighv", scores[:, :, None] * decay, vc)
        y = y + jnp.exp(cum)[..., None] * jnp.einsum("bigd,bghdv->bighv", qc, s)
        last = cum[:, -1]
        w = jnp.exp(last[:, None] - cum)
        s = jnp.exp(last)[..., None, None] * s + jnp.einsum("bjgd,bjghv->bghdv", kc, vc * w[..., None])
        return s, y

    s_fin, ys = lax.scan(step, s0, (to_chunks(q), to_chunks(k), to_chunks(v), to_chunks(log_a)))
    y = jnp.moveaxis(ys, 0, 1).reshape(b, t, *v.shape[2:])
    return y.astype(out_dtype), s_fin


def bidir_prefix_scan(qc, kc, vc_f, vc_b, lac_f, lac_b, ql, kl, vl_f, vl_b, lal_f, lal_b):
    b, g, dk = qc.shape[0], qc.shape[2], qc.shape[3]
    hg, dv = vc_f.shape[3], vc_f.shape[4]
    s0 = jnp.zeros((b, g, hg, dk, dv), jnp.float32)
    flip = lambda a: jnp.flip(a, axis=1)
    yc_f, sc_f = chunked_linear_scan(qc, kc, vc_f, lac_f, s0)
    yc_b, sc_b = chunked_linear_scan(flip(qc), flip(kc), flip(vc_b), flip(lac_b), s0)
    yl_f, _ = chunked_linear_scan(ql, kl, vl_f, lal_f, sc_f)
    yl_b, _ = chunked_linear_scan(flip(ql), flip(kl), flip(vl_b), flip(lal_b), sc_b)
    return yc_f + flip(yc_b), yl_f + flip(yl_b)


def mixer_inputs(h, pos, w_in, conv_w, conv_b, ret_decay_f, ret_decay_b,
                 a_log_f, a_log_b, dt_bias_f, dt_bias_b):
    b, t, _ = h.shape
    splits = np.cumsum(IN_SIZES)[:-1].tolist()
    q, k, v, g, z, xbc, dt, gates = jnp.split(h @ w_in, splits, axis=-1)
    q = q.reshape(b, t, RET_HEADS, RET_DK)
    k = k.reshape(b, t, RET_HEADS, RET_DK) * (RET_DK ** -0.5)
    if pos is not None:
        q = rope_2d(q, pos[0], pos[1])
        k = rope_2d(k, pos[0], pos[1])
    v = v.reshape(b, t, RET_HEADS, 1, RET_DV)
    rla_f = jnp.broadcast_to(-jnp.exp(ret_decay_f.astype(jnp.float32))[:, None], (b, t, RET_HEADS, 1))
    rla_b = jnp.broadcast_to(-jnp.exp(ret_decay_b.astype(jnp.float32))[:, None], (b, t, RET_HEADS, 1))
    xbc = jax.nn.silu(dwconv_centred(xbc, conv_w, conv_b))
    xs, bm, cm = jnp.split(xbc, [SSM_INNER, SSM_INNER + SSM_BC_W], axis=-1)
    xs = xs.reshape(b, t, SSM_GROUPS, SSM_HPG, SSM_HEADDIM)
    bm = bm.reshape(b, t, SSM_GROUPS, SSM_STATE)
    cm = cm.reshape(b, t, SSM_GROUPS, SSM_STATE)
    dtf = dt.astype(jnp.float32)
    dt_f = jax.nn.softplus(dtf[..., :SSM_HEADS] + dt_bias_f).reshape(b, t, SSM_GROUPS, SSM_HPG)
    dt_b = jax.nn.softplus(dtf[..., SSM_HEADS:] + dt_bias_b).reshape(b, t, SSM_GROUPS, SSM_HPG)
    sla_f = dt_f * (-jnp.exp(a_log_f.astype(jnp.float32))).reshape(SSM_GROUPS, SSM_HPG)
    sla_b = dt_b * (-jnp.exp(a_log_b.astype(jnp.float32))).reshape(SSM_GROUPS, SSM_HPG)
    vs_f = xs * dt_f[..., None].astype(xs.dtype)
    vs_b = xs * dt_b[..., None].astype(xs.dtype)
    return (q, k, v, rla_f, rla_b, cm, bm, xs, vs_f, vs_b, sla_f, sla_b, g, z, gates)


def mixer_output(y_ret, y_ssm, xs, g, z, gates, ret_gn_w, ssm_d, ssm_norm_w,
                 w_ret_proj, w_ssm_proj, w_out):
    b, t = g.shape[0], g.shape[1]
    yr = y_ret.reshape(b, t, RET_HEADS, RET_DV).astype(jnp.float32)
    mu = jnp.mean(yr, axis=-1, keepdims=True)
    var = jnp.mean(jnp.square(yr - mu), axis=-1, keepdims=True)
    yr = ((yr - mu) * lax.rsqrt(var + EPS)).reshape(b, t, RET_V_W).astype(g.dtype)
    yr = yr * ret_gn_w * jax.nn.silu(g)
    ys = y_ssm + ssm_d.reshape(SSM_GROUPS, SSM_HPG)[..., None] * xs
    ys = ys.reshape(b, t, SSM_INNER) * jax.nn.silu(z)
    ysf = ys.reshape(b, t, SSM_GROUPS, -1).astype(jnp.float32)
    ysf = ysf * lax.rsqrt(jnp.mean(ysf * ysf, axis=-1, keepdims=True) + EPS)
    ys = ysf.reshape(b, t, SSM_INNER).astype(z.dtype) * ssm_norm_w
    gate_r, gate_s = jnp.split(jax.nn.sigmoid(gates), N_BRANCH, axis=-1)
    merged = gate_r * (yr @ w_ret_proj) + gate_s * (ys @ w_ssm_proj)
    return merged @ w_out


def mixer_sublayer(hc, hl, pos, w_in, conv_w, conv_b, ret_decay_f, ret_decay_b, ret_gn_w,
                   a_log_f, a_log_b, dt_bias_f, dt_bias_b, ssm_d, ssm_norm_w,
                   w_ret_proj, w_ssm_proj, w_out):
    (qc, kc, vc, rlc_f, rlc_b, cc, bc, xsc, vsc_f, vsc_b, slc_f, slc_b, gc, zc, gtc) = mixer_inputs(
        hc, None, w_in, conv_w, conv_b, ret_decay_f, ret_decay_b, a_log_f, a_log_b, dt_bias_f, dt_bias_b)
    (ql, kl, vl, rll_f, rll_b, cl, bl, xsl, vsl_f, vsl_b, sll_f, sll_b, gl, zl, gtl) = mixer_inputs(
        hl, pos, w_in, conv_w, conv_b, ret_decay_f, ret_decay_b, a_log_f, a_log_b, dt_bias_f, dt_bias_b)
    yrc, yrl = bidir_prefix_scan(qc, kc, vc, vc, rlc_f, rlc_b, ql, kl, vl, vl, rll_f, rll_b)
    ysc, ysl = bidir_prefix_scan(cc, bc, vsc_f, vsc_b, slc_f, slc_b, cl, bl, vsl_f, vsl_b, sll_f, sll_b)
    oc = mixer_output(yrc, ysc, xsc, gc, zc, gtc, ret_gn_w, ssm_d, ssm_norm_w, w_ret_proj, w_ssm_proj, w_out)
    ol = mixer_output(yrl, ysl, xsl, gl, zl, gtl, ret_gn_w, ssm_d, ssm_norm_w, w_ret_proj, w_ssm_proj, w_out)
    return oc, ol


def swiglu(h, w_gate, w_up, w_down):
    return (jax.nn.silu(h @ w_gate) * (h @ w_up)) @ w_down


def moe_swiglu(h, router_w, w_gate, w_up, w_down):
    logits = (h @ router_w).astype(jnp.float32)
    top_v, top_i = lax.top_k(logits, TOP_K)
    wts = jax.nn.softmax(top_v, axis=-1)
    combine = jnp.einsum("btk,btke->bte", wts,
                         jax.nn.one_hot(top_i, N_EXPERTS, dtype=jnp.float32)).astype(h.dtype)
    out = jnp.zeros_like(h)
    for e in range(N_EXPERTS):
        out = out + combine[..., e:e + 1] * swiglu(h, w_gate[e], w_up[e], w_down[e])
    return out


def setup_inputs(seed: int = 0) -> dict:
    key = jax.random.key(seed)
    keys = iter(jax.random.split(key, 48))
    f32 = jnp.float32
    L, ND, NM, D = DEPTH, (DEPTH + 1) // 2, DEPTH // 2, D_MODEL

    def nrm(shape, std):
        return jax.random.normal(next(keys), shape, f32) * std

    ret_base = jnp.log(-jnp.log1p(-(2.0 ** (-5.0 - jnp.arange(RET_HEADS, dtype=f32)))))

    def dt_bias():
        dt0 = jnp.exp(jax.random.uniform(next(keys), (L, SSM_HEADS), f32, math.log(1e-3), math.log(1e-1)))
        return dt0 + jnp.log(-jnp.expm1(-dt0))

    return {
        "x": nrm((BATCH, SEQ, D), 1.0),
        "c": nrm((BATCH, D), 1.0),
        "ctx": nrm((BATCH, CTX_LEN, D), 1.0),
        "c_ctx": nrm((D,), 1.0),
        "w_ada": nrm((L, D, 6 * D), 0.5 * D ** -0.5),
        "b_ada": nrm((L, 6 * D), 0.02),
        "norm1_w": 1.0 + nrm((L, D), 0.05),
        "norm2_w": 1.0 + nrm((L, D), 0.05),
        "w_in": nrm((L, D, IN_W), D ** -0.5),
        "conv_w": nrm((L, CONV_K, SSM_XBC_W), CONV_K ** -0.5),
        "conv_b": nrm((L, SSM_XBC_W), 0.02),
        "ret_decay_f": ret_base + nrm((L, RET_HEADS), 0.05),
        "ret_decay_b": ret_base + nrm((L, RET_HEADS), 0.05),
        "ret_gn_w": 1.0 + nrm((L, RET_V_W), 0.05),
        "ssm_a_log_f": jnp.log(jax.random.uniform(next(keys), (L, SSM_HEADS), f32, 1.0, 16.0)),
        "ssm_a_log_b": jnp.log(jax.random.uniform(next(keys), (L, SSM_HEADS), f32, 1.0, 16.0)),
        "ssm_dt_bias_f": dt_bias(),
        "ssm_dt_bias_b": dt_bias(),
        "ssm_d": 1.0 + nrm((L, SSM_HEADS), 0.1),
        "ssm_norm_w": 1.0 + nrm((L, SSM_INNER), 0.05),
        "w_ret_proj": nrm((L, RET_V_W, D), RET_V_W ** -0.5),
        "w_ssm_proj": nrm((L, SSM_INNER, D), SSM_INNER ** -0.5),
        "w_out": nrm((L, D, D), D ** -0.5),
        "ffn_w_gate": nrm((ND, D, D_FF), D ** -0.5),
        "ffn_w_up": nrm((ND, D, D_FF), D ** -0.5),
        "ffn_w_down": nrm((ND, D_FF, D), D_FF ** -0.5),
        "moe_router": nrm((NM, D, N_EXPERTS), D ** -0.5),
        "moe_w_gate": nrm((NM, N_EXPERTS, D, MOE_D_FF), D ** -0.5),
        "moe_w_up": nrm((NM, N_EXPERTS, D, MOE_D_FF), D ** -0.5),
        "moe_w_down": nrm((NM, N_EXPERTS, MOE_D_FF, D), MOE_D_FF ** -0.5),
        "final_norm_w": 1.0 + nrm((D,), 0.05),
    }


def reference(x, c, ctx, c_ctx, w_ada, b_ada, norm1_w, norm2_w, w_in, conv_w, conv_b,
              ret_decay_f, ret_decay_b, ret_gn_w, ssm_a_log_f, ssm_a_log_b, ssm_dt_bias_f,
              ssm_dt_bias_b, ssm_d, ssm_norm_w, w_ret_proj, w_ssm_proj, w_out,
              ffn_w_gate, ffn_w_up, ffn_w_down, moe_router, moe_w_gate, moe_w_up, moe_w_down,
              final_norm_w):
    n = x.shape[1]
    rows = n // GRID_W
    pos = (jnp.repeat(jnp.arange(rows), GRID_W), jnp.tile(jnp.arange(GRID_W), rows))
    silu_c = jax.nn.silu(c)
    silu_cc = jax.nn.silu(c_ctx)
    xl, xc = x, ctx
    for i in range(DEPTH):
        mod_l = (silu_c @ w_ada[i] + b_ada[i])[:, None, :]
        mod_c = silu_cc @ w_ada[i] + b_ada[i]
        sh1_l, sc1_l, g1_l, sh2_l, sc2_l, g2_l = jnp.split(mod_l, 6, axis=-1)
        sh1_c, sc1_c, g1_c, sh2_c, sc2_c, g2_c = jnp.split(mod_c, 6, axis=-1)

        hl = modulate(rmsnorm(xl, norm1_w[i]), sh1_l, sc1_l)
        hc = modulate(rmsnorm(xc, norm1_w[i]), sh1_c, sc1_c)
        oc, ol = mixer_sublayer(hc, hl, pos, w_in[i], conv_w[i], conv_b[i], ret_decay_f[i], ret_decay_b[i],
                                ret_gn_w[i], ssm_a_log_f[i], ssm_a_log_b[i], ssm_dt_bias_f[i],
                                ssm_dt_bias_b[i], ssm_d[i], ssm_norm_w[i], w_ret_proj[i], w_ssm_proj[i],
                                w_out[i])
        xl = xl + g1_l * ol
        xc_mid = xc + g1_c * oc

        j = i // 2

        def channel_mixer(h):
            if i % 2 == 0:
                return swiglu(h, ffn_w_gate[j], ffn_w_up[j], ffn_w_down[j])
            return moe_swiglu(h, moe_router[j], moe_w_gate[j], moe_w_up[j], moe_w_down[j])

        hl = modulate(rmsnorm(xl, norm2_w[i]), sh2_l, sc2_l)
        xl = xl + g2_l * channel_mixer(hl)
        if i < DEPTH - 1:
            hc = modulate(rmsnorm(xc_mid, norm2_w[i]), sh2_c, sc2_c)
            xc = xc_mid + g2_c * channel_mixer(hc)
    return rmsnorm(xl, final_norm_w)
```

```python
import functools

import numpy as np
import jax
import jax.numpy as jnp
from jax import lax
from jax.experimental import pallas as pl
from jax.experimental.pallas import tpu as pltpu

F32 = jnp.float32
BF16 = jnp.bfloat16

D_MODEL = 2048
GRID_W = 64
CHUNK = 128
EPS = 1e-6
ROPE_BASE = 10000.0
RET_HEADS = 8
RET_DK = 128
RET_DV = 256
SSM_INNER = 4096
SSM_HEADS = 64
SSM_GROUPS = 8
SSM_HPG = 8
SSM_GROUP_W = SSM_INNER // SSM_GROUPS
SSM_STATE = 128
XBC_W = 6144
CONV_K = 5
D_FF = 5632
N_EXPERTS = 8
MOE_D_FF = 4096

COL_Q, COL_K, COL_V, COL_G, COL_Z, COL_XBC, COL_GATES = 0, 1024, 2048, 4096, 6144, 10240, 16384
PROJ_W = 20480
ORIG_DT_LO, ORIG_DT_HI = 16384, 16512

LANES = 128
VMEM_LIMIT_BYTES = 56 * 1024 * 1024


def _params(n_axes, vmem=VMEM_LIMIT_BYTES):
    return pltpu.CompilerParams(dimension_semantics=("arbitrary",) * n_axes, vmem_limit_bytes=vmem)


def _silu(x):
    return x * jax.nn.sigmoid(x)


def _split2(x):
    hi = x.astype(BF16)
    lo = (x - hi.astype(F32)).astype(BF16)
    return hi, lo


def _split3(x):
    hi = x.astype(BF16)
    r = x - hi.astype(F32)
    mid = r.astype(BF16)
    lo = (r - mid.astype(F32)).astype(BF16)
    return hi, mid, lo


def _dot(a, b):
    return jnp.dot(a, b, preferred_element_type=F32)


def _dot_nt(a, b):
    return lax.dot_general(a, b, (((1,), (1,)), ((), ())), preferred_element_type=F32)


def _dot_tn(a, b):
    return lax.dot_general(a, b, (((0,), (0,)), ((), ())), preferred_element_type=F32)


def _norm_mod(x, nw, shift, scale):
    y = x * lax.rsqrt(jnp.mean(x * x, axis=-1, keepdims=True) + EPS)
    return (y * nw) * (1.0 + scale) + shift


def _mod_kernel(c_ref, w_ref, b_ref, o_ref):
    s_hi, s_lo = _split2(_silu(c_ref[...]))
    w_hi, w_lo = _split2(w_ref[...])
    o_ref[...] = _dot(s_hi, w_hi) + _dot(s_hi, w_lo) + _dot(s_lo, w_hi) + b_ref[...]


def _modulation(cvec, w_ada, b_ada):
    depth, d, w6 = w_ada.shape
    rows = cvec.shape[0]
    tn = 512
    return pl.pallas_call(
        _mod_kernel,
        grid=(depth, w6 // tn),
        in_specs=[
            pl.BlockSpec((rows, d), lambda l, j: (0, 0)),
            pl.BlockSpec((None, d, tn), lambda l, j: (l, 0, j)),
            pl.BlockSpec((None, 1, tn), lambda l, j: (l, 0, j)),
        ],
        out_specs=pl.BlockSpec((None, rows, tn), lambda l, j: (l, 0, j)),
        out_shape=jax.ShapeDtypeStruct((depth, rows, w6), F32),
        compiler_params=_params(2),
        name="adaln_mod",
    )(cvec, w_ada, b_ada.reshape(depth, 1, w6))


class _Geom:
    def __init__(self, batch, ctx_len, seq):
        self.batch, self.ctx_len, self.seq = batch, ctx_len, seq
        tm = 1024
        while (batch * ctx_len) % tm or seq % tm:
            tm //= 2
        assert tm >= CHUNK and ctx_len % CHUNK == 0 and seq % CHUNK == 0
        self.tm = tm
        self.ctx_rows = batch * ctx_len
        self.rows = self.ctx_rows + batch * seq
        self.n_ctx_tiles = self.ctx_rows // tm
        self.tiles_per_batch = seq // tm
        self.n_tiles = self.rows // tm
        self.nctx = ctx_len // CHUNK
        self.nlat = seq // CHUNK
        self.nc = self.nctx + self.nlat

    def mod_row(self, i):
        return jnp.where(i < self.n_ctx_tiles, 0, 1 + (i - self.n_ctx_tiles) // self.tiles_per_batch)

    def rope_block(self, i):
        return jnp.where(i < self.n_ctx_tiles, 0, 1 + (i - self.n_ctx_tiles) % self.tiles_per_batch)

    def chunk_block(self, b, c):
        return jnp.where(c < self.nctx, b * self.nctx + c,
                         self.batch * self.nctx + b * self.nlat + (c - self.nctx))

    def backward_order(self, s):
        return jnp.where(s < self.nctx, self.nctx - 1 - s, self.nc - 1 - (s - self.nctx))


def _rope_store(acc, cos, sin, o_ref, scale):
    lane = lax.broadcasted_iota(jnp.int32, (acc.shape[0], LANES), 1)
    first_half = (lane % 64) < 32
    for h in range(acc.shape[1] // LANES):
        xh = acc[:, h * LANES:(h + 1) * LANES]
        partner = jnp.where(first_half, pltpu.roll(xh, 96, 1), pltpu.roll(xh, 32, 1))
        o_ref[:, h * LANES:(h + 1) * LANES] = ((xh * cos + partner * sin) * scale).astype(o_ref.dtype)


def _inproj_kernel(x_ref, mod_ref, nw_ref, w_ref, wdt_ref, cos_ref, sin_ref, o_ref, dt_ref, h_ref):
    j = pl.program_id(1)

    @pl.when(j == 0)
    def _():
        h = _norm_mod(x_ref[...], nw_ref[...], mod_ref[0:1, :], mod_ref[1:2, :]).astype(BF16)
        h_ref[...] = h
        dt_ref[...] = _dot(h, wdt_ref[...])

    acc = _dot(h_ref[...], w_ref[...])

    @pl.when(j == 0)
    def _():
        _rope_store(acc, cos_ref[...], sin_ref[...], o_ref, 1.0)

    @pl.when(j == 1)
    def _():
        _rope_store(acc, cos_ref[...], sin_ref[...], o_ref, RET_DK ** -0.5)

    @pl.when(j >= 2)
    def _():
        o_ref[...] = acc.astype(o_ref.dtype)


def _inproj(geo, xs, mod_l, nw, w_main, w_dt, cos_t, sin_t):
    tm, tn = geo.tm, 1024
    return pl.pallas_call(
        _inproj_kernel,
        grid=(geo.n_tiles, PROJ_W // tn),
        in_specs=[
            pl.BlockSpec((tm, D_MODEL), lambda i, j: (i, 0)),
            pl.BlockSpec((None, 6, D_MODEL), lambda i, j: (geo.mod_row(i), 0, 0)),
            pl.BlockSpec((1, D_MODEL), lambda i, j: (0, 0)),
            pl.BlockSpec((D_MODEL, tn), lambda i, j: (0, j)),
            pl.BlockSpec((D_MODEL, LANES), lambda i, j: (0, 0)),
            pl.BlockSpec((tm, LANES), lambda i, j: (geo.rope_block(i), 0)),
            pl.BlockSpec((tm, LANES), lambda i, j: (geo.rope_block(i), 0)),
        ],
        out_specs=[
            pl.BlockSpec((tm, tn), lambda i, j: (i, j)),
            pl.BlockSpec((tm, LANES), lambda i, j: (i, 0)),
        ],
        out_shape=[
            jax.ShapeDtypeStruct((geo.rows, PROJ_W), BF16),
            jax.ShapeDtypeStruct((geo.rows, LANES), F32),
        ],
        scratch_shapes=[pltpu.VMEM((tm, D_MODEL), BF16)],
        compiler_params=_params(2),
        name="in_proj",
    )(xs, mod_l, nw, w_main, w_dt, cos_t, sin_t)


CONV_HALO = 16


def _conv_kernel(prev_ref, main_ref, next_ref, w_ref, b_ref, o_ref, ext_ref, *, geo, rows):
    i = pl.program_id(0)
    start = i * rows
    in_ctx = start < geo.ctx_rows
    seq_len = jnp.where(in_ctx, geo.ctx_len, geo.seq)
    off = jnp.where(in_ctx, start, start - geo.ctx_rows)
    first = (off % seq_len) == 0
    last = ((off + rows) % seq_len) == 0
    ext_ref[0:CONV_HALO, :] = jnp.where(first, 0.0, prev_ref[...].astype(F32))
    ext_ref[CONV_HALO:CONV_HALO + rows, :] = main_ref[...].astype(F32)
    ext_ref[CONV_HALO + rows:, :] = jnp.where(last, 0.0, next_ref[...].astype(F32))
    acc = jnp.broadcast_to(b_ref[...], (rows, b_ref.shape[1]))
    for k in range(CONV_K):
        acc = acc + ext_ref[pl.ds(CONV_HALO + k - CONV_K // 2, rows), :] * w_ref[k:k + 1, :]
    o_ref[...] = _silu(acc).astype(o_ref.dtype)


def _conv_silu(geo, proj, conv_w, conv_b):
    rows = min(256, geo.ctx_len)
    tc = 1024
    col0 = COL_XBC // tc
    n_halo = geo.rows // CONV_HALO
    per = rows // CONV_HALO
    return pl.pallas_call(
        functools.partial(_conv_kernel, geo=geo, rows=rows),
        grid=(geo.rows // rows, XBC_W // tc),
        in_specs=[
            pl.BlockSpec((CONV_HALO, tc), lambda i, j: (jnp.maximum(i * per - 1, 0), col0 + j)),
            pl.BlockSpec((rows, tc), lambda i, j: (i, col0 + j)),
            pl.BlockSpec((CONV_HALO, tc), lambda i, j: (jnp.minimum((i + 1) * per, n_halo - 1), col0 + j)),
            pl.BlockSpec((CONV_K, tc), lambda i, j: (0, j)),
            pl.BlockSpec((1, tc), lambda i, j: (0, j)),
        ],
        out_specs=pl.BlockSpec((rows, tc), lambda i, j: (i, j)),
        out_shape=jax.ShapeDtypeStruct((geo.rows, XBC_W), BF16),
        scratch_shapes=[pltpu.VMEM((rows + 2 * CONV_HALO, tc), F32)],
        compiler_params=_params(2),
        name="ssm_conv",
    )(proj, proj, proj, conv_w, conv_b)


def _decay_kernel(dt_ref, bias_ref, alog_ref, cc_ref, cp_ref, cct_ref, cpt_ref, dtt_ref):
    x = dt_ref[...] + bias_ref[...]
    dtv = jnp.maximum(x, 0.0) + jnp.log1p(jnp.exp(-jnp.abs(x)))
    la = dtv * (-jnp.exp(alog_ref[...]))
    ii = lax.broadcasted_iota(jnp.int32, (CHUNK, CHUNK), 0)
    jj = lax.broadcasted_iota(jnp.int32, (CHUNK, CHUNK), 1)
    lower = (jj <= ii).astype(BF16)
    upper = (jj >= ii).astype(BF16)
    hi, mid, lo = _split3(la)
    cum_f = _dot(lower, hi) + _dot(lower, mid) + _dot(lower, lo)
    cum_b = _dot(upper, hi) + _dot(upper, mid) + _dot(upper, lo)
    lane = lax.broadcasted_iota(jnp.int32, (CHUNK, LANES), 1)
    cc = jnp.where((lane % 16) < 8, cum_f, cum_b)
    cp = cc - jnp.log(dtv)
    cc_ref[...] = cc
    cp_ref[...] = cp
    cct_ref[...] = cc.T
    cpt_ref[...] = cp.T
    dtt_ref[...] = dtv.T


def _decay_tables(geo, dt, bias, alog):
    n_chunks = geo.rows // CHUNK
    col = pl.BlockSpec((CHUNK, LANES), lambda i: (i, 0))
    row = pl.BlockSpec((None, LANES, CHUNK), lambda i: (i, 0, 0))
    vec = pl.BlockSpec((1, LANES), lambda i: (0, 0))
    col_shape = jax.ShapeDtypeStruct((geo.rows, LANES), F32)
    row_shape = jax.ShapeDtypeStruct((n_chunks, LANES, CHUNK), F32)
    return pl.pallas_call(
        _decay_kernel,
        grid=(n_chunks,),
        in_specs=[col, vec, vec],
        out_specs=[col, col, row, row, row],
        out_shape=[col_shape, col_shape, row_shape, row_shape, row_shape],
        compiler_params=_params(1),
        name="ssm_decay",
    )(dt, bias, alog)


def _ssd_kernel(q_ref, k_ref, xs_ref, z_ref, cc_ref, cp_ref, cct_ref, cpt_ref, dtt_ref, dskip_ref,
                nw_ref, o_ref, sf_ref, sb_ref, hist_ref, *, geo):
    g, ph, s = pl.program_id(1), pl.program_id(2), pl.program_id(3)
    c = jnp.where(ph == 0, geo.backward_order(s), s)
    shift = (LANES - 16 * g) % LANES
    cc = pltpu.roll(cc_ref[...], shift, 1)
    cp = pltpu.roll(cp_ref[...], shift, 1)
    k = k_ref[...]
    xs = xs_ref[...]
    lane = lax.broadcasted_iota(jnp.int32, (CHUNK, LANES), 1)
    left = lane < 64
    left_row = lax.broadcasted_iota(jnp.int32, (1, LANES), 1) < 64
    n_pairs = SSM_HPG // 2

    def pair_cols(a, h1, h2):
        return jnp.where(left, a[:, h1:h1 + 1], a[:, h2:h2 + 1])

    @pl.when(jnp.logical_and(ph == 0, s == 0))
    def _():
        sb_ref[...] = jnp.zeros_like(sb_ref)

    @pl.when(jnp.logical_and(ph == 1, s == 0))
    def _():
        sf_ref[...] = jnp.zeros_like(sf_ref)

    @pl.when(ph == 0)
    def _():
        hist_ref[c] = sb_ref[...].astype(BF16)
        w_all = jnp.exp(cc[0:1, :] - cp)
        a_all = jnp.exp(cc[0:1, :])
        for p in range(n_pairs):
            h1, h2 = SSM_HPG + 2 * p, SSM_HPG + 2 * p + 1
            sl = slice(p * LANES, (p + 1) * LANES)
            wb = pair_cols(w_all, h1, h2)
            ab = jnp.where(left_row, a_all[:, h1:h1 + 1], a_all[:, h2:h2 + 1])
            vw = (xs[:, sl].astype(F32) * wb).astype(BF16)
            sb_ref[:, sl] = ab * sb_ref[:, sl] + _dot_tn(k, vw)

    @pl.when(ph == 1)
    def _():
        q = q_ref[...]
        cct = cct_ref[...]
        cpt = cpt_ref[...]
        dtt = dtt_ref[...]
        scores = _dot_nt(q, k)
        ii = lax.broadcasted_iota(jnp.int32, (CHUNK, CHUNK), 0)
        jj = lax.broadcasted_iota(jnp.int32, (CHUNK, CHUNK), 1)
        lower = jj <= ii
        diag = jj == ii
        yf = _dot(q, sf_ref[...].astype(BF16))
        yb = _dot(q, hist_ref[c])
        e_all = jnp.exp(cc)
        w_all = jnp.exp(cc[CHUNK - 1:CHUNK, :] - cp)
        a_all = jnp.exp(cc[CHUNK - 1:CHUNK, :])
        ys = []
        for p in range(n_pairs):
            sl = slice(p * LANES, (p + 1) * LANES)
            vp = xs[:, sl]
            probs = []
            for h in (2 * p, 2 * p + 1):
                hb = SSM_HPG + h
                arg = jnp.where(lower, cc[:, h:h + 1] - cpt[h:h + 1, :], cc[:, hb:hb + 1] - cpt[hb:hb + 1, :])
                e = jnp.exp(arg) + jnp.where(diag, dtt[hb:hb + 1, :], 0.0)
                probs.append((scores * e).astype(BF16))
            zero = jnp.zeros_like(vp)
            rhs = jnp.concatenate([jnp.where(left, vp, zero), jnp.where(left, zero, vp)], axis=0)
            y = _dot(jnp.concatenate(probs, axis=1), rhs)
            h1, h2 = 2 * p, 2 * p + 1
            y = y + pair_cols(e_all, h1, h2) * yf[:, sl] + pair_cols(e_all, SSM_HPG + h1, SSM_HPG + h2) * yb[:, sl]
            ys.append(y)
            wf = pair_cols(w_all, h1, h2)
            af = jnp.where(left_row, a_all[:, h1:h1 + 1], a_all[:, h2:h2 + 1])
            vw = (vp.astype(F32) * wf).astype(BF16)
            sf_ref[:, sl] = af * sf_ref[:, sl] + _dot_tn(k, vw)
        y = jnp.concatenate(ys, axis=1)
        y = y + dskip_ref[...] * xs.astype(F32)
        y = y * _silu(z_ref[...].astype(F32))
        y = y * lax.rsqrt(jnp.mean(y * y, axis=-1, keepdims=True) + EPS)
        o_ref[...] = (y * nw_ref[...]).astype(o_ref.dtype)


def _ssd_scan(geo, proj, xbc, cc, cp, cct, cpt, dtt, dskip, norm_w):
    gw = SSM_GROUP_W

    def blk(b, c):
        return geo.chunk_block(b, c)

    def cur(b, g, ph, s):
        return blk(b, jnp.where(ph == 0, geo.backward_order(s), s))

    def out_blk(b, g, ph, s):
        return blk(b, jnp.where(ph == 0, 0, s))

    row_spec = pl.BlockSpec((None, 16, CHUNK), lambda b, g, ph, s: (cur(b, g, ph, s), g, 0))
    col_spec = pl.BlockSpec((CHUNK, LANES), lambda b, g, ph, s: (cur(b, g, ph, s), 0))
    return pl.pallas_call(
        functools.partial(_ssd_kernel, geo=geo),
        grid=(geo.batch, SSM_GROUPS, 2, geo.nc),
        in_specs=[
            pl.BlockSpec((CHUNK, SSM_STATE), lambda b, g, ph, s: (cur(b, g, ph, s), (SSM_INNER + 1024) // SSM_STATE + g)),
            pl.BlockSpec((CHUNK, SSM_STATE), lambda b, g, ph, s: (cur(b, g, ph, s), SSM_INNER // SSM_STATE + g)),
            pl.BlockSpec((CHUNK, gw), lambda b, g, ph, s: (cur(b, g, ph, s), g)),
            pl.BlockSpec((CHUNK, gw), lambda b, g, ph, s: (cur(b, g, ph, s), COL_Z // gw + g)),
            col_spec, col_spec, row_spec, row_spec, row_spec,
            pl.BlockSpec((1, gw), lambda b, g, ph, s: (0, g)),
            pl.BlockSpec((1, gw), lambda b, g, ph, s: (0, g)),
        ],
        out_specs=pl.BlockSpec((CHUNK, gw), lambda b, g, ph, s: (out_blk(b, g, ph, s), g)),
        out_shape=jax.ShapeDtypeStruct((geo.rows, SSM_INNER), BF16),
        scratch_shapes=[
            pltpu.VMEM((SSM_STATE, gw), F32),
            pltpu.VMEM((SSM_STATE, gw), F32),
            pltpu.VMEM((geo.nc, SSM_STATE, gw), BF16),
        ],
        compiler_params=_params(4),
        name="ssd_scan",
    )(xbc, xbc, xbc, proj, cc, cp, cct, cpt, dtt, dskip, norm_w)


def _ret_kernel(dec_ref, q_ref, k_ref, v_ref, g_ref, gnw_ref, o_ref,
                sf_ref, sb_ref, hist_ref, m_ref, ey_ref, wk_ref, a_ref, *, geo):
    h, ph, s = pl.program_id(1), pl.program_id(2), pl.program_id(3)
    c = jnp.where(ph == 0, geo.backward_order(s), s)
    n = float(CHUNK)

    @pl.when(jnp.logical_and(ph == 0, s == 0))
    def _():
        lam_f = jnp.exp(jnp.full((CHUNK, 1), dec_ref[0, h], F32))
        lam_b = jnp.exp(jnp.full((CHUNK, 1), dec_ref[1, h], F32))
        ii = lax.broadcasted_iota(jnp.int32, (CHUNK, CHUNK), 0)
        jj = lax.broadcasted_iota(jnp.int32, (CHUNK, CHUNK), 1)
        dist = (ii - jj).astype(F32)
        m_ref[...] = (jnp.where(jj <= ii, jnp.exp(-lam_f * dist), 0.0)
                      + jnp.where(jj >= ii, jnp.exp(lam_b * dist), 0.0))
        row = lax.broadcasted_iota(jnp.int32, (CHUNK, 1), 0).astype(F32)
        ey_ref[0] = jnp.broadcast_to(jnp.exp(-lam_f * (row + 1.0)), (CHUNK, RET_DV))
        ey_ref[1] = jnp.broadcast_to(jnp.exp(-lam_b * (n - row)), (CHUNK, RET_DV))
        wk_ref[0] = jnp.broadcast_to(jnp.exp(-lam_f * (n - 1.0 - row)), (CHUNK, RET_DK))
        wk_ref[1] = jnp.broadcast_to(jnp.exp(-lam_b * row), (CHUNK, RET_DK))
        a_ref[0] = jnp.broadcast_to(jnp.exp(-lam_f[0:8, :] * n), (8, RET_DV))
        a_ref[1] = jnp.broadcast_to(jnp.exp(-lam_b[0:8, :] * n), (8, RET_DV))
        sb_ref[...] = jnp.zeros_like(sb_ref)

    @pl.when(jnp.logical_and(ph == 1, s == 0))
    def _():
        sf_ref[...] = jnp.zeros_like(sf_ref)

    k = k_ref[...]
    v = v_ref[...]

    @pl.when(ph == 0)
    def _():
        hist_ref[c] = sb_ref[...].astype(BF16)
        kw = (k.astype(F32) * wk_ref[1]).astype(BF16)
        sb_ref[...] = a_ref[1][0:1, :] * sb_ref[...] + _dot_tn(kw, v)

    @pl.when(ph == 1)
    def _():
        q = q_ref[...]
        probs = (_dot_nt(q, k) * m_ref[...]).astype(BF16)
        y = _dot(probs, v)
        y = y + ey_ref[0] * _dot(q, sf_ref[...].astype(BF16)) + ey_ref[1] * _dot(q, hist_ref[c])
        kw = (k.astype(F32) * wk_ref[0]).astype(BF16)
        sf_ref[...] = a_ref[0][0:1, :] * sf_ref[...] + _dot_tn(kw, v)
        mu = jnp.mean(y, axis=-1, keepdims=True)
        yc = y - mu
        var = jnp.mean(yc * yc, axis=-1, keepdims=True)
        yn = yc * lax.rsqrt(var + EPS)
        o_ref[...] = (yn * gnw_ref[...] * _silu(g_ref[...].astype(F32))).astype(o_ref.dtype)


def _ret_scan(geo, proj, decays, gn_w):
    def cur(b, h, ph, s):
        return geo.chunk_block(b, jnp.where(ph == 0, geo.backward_order(s), s))

    def out_blk(b, h, ph, s):
        return geo.chunk_block(b, jnp.where(ph == 0, 0, s))

    return pl.pallas_call(
        functools.partial(_ret_kernel, geo=geo),
        grid=(geo.batch, RET_HEADS, 2, geo.nc),
        in_specs=[
            pl.BlockSpec(memory_space=pltpu.SMEM),
            pl.BlockSpec((CHUNK, RET_DK), lambda b, h, ph, s: (cur(b, h, ph, s), COL_Q // RET_DK + h)),
            pl.BlockSpec((CHUNK, RET_DK), lambda b, h, ph, s: (cur(b, h, ph, s), COL_K // RET_DK + h)),
            pl.BlockSpec((CHUNK, RET_DV), lambda b, h, ph, s: (cur(b, h, ph, s), COL_V // RET_DV + h)),
            pl.BlockSpec((CHUNK, RET_DV), lambda b, h, ph, s: (cur(b, h, ph, s), COL_G // RET_DV + h)),
            pl.BlockSpec((1, RET_DV), lambda b, h, ph, s: (0, h)),
        ],
        out_specs=pl.BlockSpec((CHUNK, RET_DV), lambda b, h, ph, s: (out_blk(b, h, ph, s), h)),
        out_shape=jax.ShapeDtypeStruct((geo.rows, RET_HEADS * RET_DV), BF16),
        scratch_shapes=[
            pltpu.VMEM((RET_DK, RET_DV), F32),
            pltpu.VMEM((RET_DK, RET_DV), F32),
            pltpu.VMEM((geo.nc, RET_DK, RET_DV), BF16),
            pltpu.VMEM((CHUNK, CHUNK), F32),
            pltpu.VMEM((2, CHUNK, RET_DV), F32),
            pltpu.VMEM((2, CHUNK, RET_DK), F32),
            pltpu.VMEM((2, 8, RET_DV), F32),
        ],
        compiler_params=_params(4),
        name="ret_scan",
    )(decays, proj, proj, proj, proj, gn_w)


def _merge_kernel(yr_ref, ys_ref, wr_ref, ws_ref, gr_ref, gs_ref, o_ref):
    r = _dot(yr_ref[...], wr_ref[...])
    s = _dot(ys_ref[...], ws_ref[...])
    m = jax.nn.sigmoid(gr_ref[...].astype(F32)) * r + jax.nn.sigmoid(gs_ref[...].astype(F32)) * s
    o_ref[...] = m.astype(o_ref.dtype)


def _merge(geo, yr, ys, w_ret, w_ssm, proj):
    tm, tn = geo.tm, 512
    gr0 = COL_GATES // tn
    gs0 = (COL_GATES + D_MODEL) // tn
    return pl.pallas_call(
        _merge_kernel,
        grid=(geo.n_tiles, D_MODEL // tn),
        in_specs=[
            pl.BlockSpec((tm, yr.shape[1]), lambda i, j: (i, 0)),
            pl.BlockSpec((tm, ys.shape[1]), lambda i, j: (i, 0)),
            pl.BlockSpec((yr.shape[1], tn), lambda i, j: (0, j)),
            pl.BlockSpec((ys.shape[1], tn), lambda i, j: (0, j)),
            pl.BlockSpec((tm, tn), lambda i, j: (i, gr0 + j)),
            pl.BlockSpec((tm, tn), lambda i, j: (i, gs0 + j)),
        ],
        out_specs=pl.BlockSpec((tm, tn), lambda i, j: (i, j)),
        out_shape=jax.ShapeDtypeStruct((geo.rows, D_MODEL), BF16),
        compiler_params=_params(2),
        name="branch_merge",
    )(yr, ys, w_ret, w_ssm, proj, proj)


def _residual_kernel(a_ref, w_ref, x_ref, mod_ref, o_ref, *, gate_row):
    o_ref[...] = x_ref[...] + mod_ref[gate_row:gate_row + 1, :] * _dot(a_ref[...], w_ref[...])


def _matmul_residual(geo, a, w, xs, mod_l, gate_row, tile0, n_tiles, tn):
    tm = geo.tm
    kdim = a.shape[1]
    return pl.pallas_call(
        functools.partial(_residual_kernel, gate_row=gate_row),
        grid=(n_tiles, D_MODEL // tn),
        in_specs=[
            pl.BlockSpec((tm, kdim), lambda i, j: (i, 0)),
            pl.BlockSpec((kdim, tn), lambda i, j: (0, j)),
            pl.BlockSpec((tm, tn), lambda i, j: (tile0 + i, j)),
            pl.BlockSpec((None, 6, tn), lambda i, j: (geo.mod_row(tile0 + i), 0, j)),
        ],
        out_specs=pl.BlockSpec((tm, tn), lambda i, j: (i, j)),
        out_shape=jax.ShapeDtypeStruct((n_tiles * tm, D_MODEL), F32),
        compiler_params=_params(2),
        name="proj_residual",
    )(a, w, xs, mod_l)


def _ffn_up_kernel(x_ref, mod_ref, nw_ref, wg_ref, wu_ref, o_ref, h_ref):
    @pl.when(pl.program_id(1) == 0)
    def _():
        h_ref[...] = _norm_mod(x_ref[...], nw_ref[...], mod_ref[3:4, :], mod_ref[4:5, :]).astype(BF16)

    h = h_ref[...]
    o_ref[...] = (_silu(_dot(h, wg_ref[...])) * _dot(h, wu_ref[...])).astype(o_ref.dtype)


def _ffn_up(geo, xs, mod_l, nw, w_gate, w_up):
    tm, tf = geo.tm, 512
    return pl.pallas_call(
        _ffn_up_kernel,
        grid=(geo.n_tiles, D_FF // tf),
        in_specs=[
            pl.BlockSpec((tm, D_MODEL), lambda i, j: (i, 0)),
            pl.BlockSpec((None, 6, D_MODEL), lambda i, j: (geo.mod_row(i), 0, 0)),
            pl.BlockSpec((1, D_MODEL), lambda i, j: (0, 0)),
            pl.BlockSpec((D_MODEL, tf), lambda i, j: (0, j)),
            pl.BlockSpec((D_MODEL, tf), lambda i, j: (0, j)),
        ],
        out_specs=pl.BlockSpec((tm, tf), lambda i, j: (i, j)),
        out_shape=jax.ShapeDtypeStruct((geo.rows, D_FF), BF16),
        scratch_shapes=[pltpu.VMEM((tm, D_MODEL), BF16)],
        compiler_params=_params(2),
        name="ffn_up",
    )(xs, mod_l, nw, w_gate, w_up)


def _moe_kernel(x_ref, mod_ref, nw_ref, router_ref, wg_ref, wu_ref, wd_ref, o_ref, h_ref, comb_ref):
    e, j = pl.program_id(1), pl.program_id(2)
    tm = x_ref.shape[0]
    lane = lax.broadcasted_iota(jnp.int32, (tm, LANES), 1)

    @pl.when(jnp.logical_and(e == 0, j == 0))
    def _():
        h = _norm_mod(x_ref[...], nw_ref[...], mod_ref[3:4, :], mod_ref[4:5, :])
        h_hi, h_lo = _split2(h)
        h_ref[...] = h_hi
        r_hi, r_lo = _split2(router_ref[...])
        logits = _dot(h_hi, r_hi) + _dot(h_hi, r_lo) + _dot(h_lo, r_hi)
        neg = -jnp.inf
        lg = jnp.where(lane < N_EXPERTS, logits, neg)
        m1 = jnp.max(lg, axis=-1, keepdims=True)
        i1 = jnp.min(jnp.where(lg == m1, lane, LANES), axis=-1, keepdims=True)
        lg2 = jnp.where(lane == i1, neg, lg)
        m2 = jnp.max(lg2, axis=-1, keepdims=True)
        i2 = jnp.min(jnp.where(lg2 == m2, lane, LANES), axis=-1, keepdims=True)
        e2 = jnp.exp(m2 - m1)
        w1 = 1.0 / (1.0 + e2)
        w2 = e2 / (1.0 + e2)
        comb_ref[...] = jnp.where(lane == i1, w1, 0.0) + jnp.where(lane == i2, w2, 0.0)
        o_ref[...] = jnp.zeros_like(o_ref)

    h = h_ref[...]
    ce = jnp.sum(jnp.where(lane == e, comb_ref[...], 0.0), axis=-1, keepdims=True)
    hid = (_silu(_dot(h, wg_ref[...])) * _dot(h, wu_ref[...]) * ce).astype(BF16)
    for n in range(0, D_MODEL, 512):
        o_ref[:, n:n + 512] += _dot(hid, wd_ref[:, n:n + 512])

    @pl.when(jnp.logical_and(e == pl.num_programs(1) - 1, j == pl.num_programs(2) - 1))
    def _():
        o_ref[...] = x_ref[...] + mod_ref[5:6, :] * o_ref[...]


def _moe(geo, xs, mod_l, nw, router, w_gate, w_up, w_down, tile0, n_tiles):
    sub = 2 if geo.tm >= 1024 else 1
    tm, tf = geo.tm // sub, 512
    tile0, n_tiles = tile0 * sub, n_tiles * sub
    return pl.pallas_call(
        _moe_kernel,
        grid=(n_tiles, N_EXPERTS, MOE_D_FF // tf),
        in_specs=[
            pl.BlockSpec((tm, D_MODEL), lambda i, e, j: (tile0 + i, 0)),
            pl.BlockSpec((None, 6, D_MODEL), lambda i, e, j: (geo.mod_row((tile0 + i) // sub), 0, 0)),
            pl.BlockSpec((1, D_MODEL), lambda i, e, j: (0, 0)),
            pl.BlockSpec((D_MODEL, LANES), lambda i, e, j: (0, 0)),
            pl.BlockSpec((None, D_MODEL, tf), lambda i, e, j: (e, 0, j)),
            pl.BlockSpec((None, D_MODEL, tf), lambda i, e, j: (e, 0, j)),
            pl.BlockSpec((None, tf, D_MODEL), lambda i, e, j: (e, j, 0)),
        ],
        out_specs=pl.BlockSpec((tm, D_MODEL), lambda i, e, j: (i, 0)),
        out_shape=jax.ShapeDtypeStruct((n_tiles * tm, D_MODEL), F32),
        scratch_shapes=[pltpu.VMEM((tm, D_MODEL), BF16), pltpu.VMEM((tm, LANES), F32)],
        compiler_params=_params(3),
        name="moe_dense",
    )(xs, mod_l, nw, router, w_gate, w_up, w_down)


def _final_norm_kernel(x_ref, w_ref, o_ref):
    x = x_ref[...]
    o_ref[...] = x * lax.rsqrt(jnp.mean(x * x, axis=-1, keepdims=True) + EPS) * w_ref[...]


def _final_norm(x, w, tm):
    rows = x.shape[0]
    return pl.pallas_call(
        _final_norm_kernel,
        grid=(rows // tm,),
        in_specs=[pl.BlockSpec((tm, D_MODEL), lambda i: (i, 0)), pl.BlockSpec((1, D_MODEL), lambda i: (0, 0))],
        out_specs=pl.BlockSpec((tm, D_MODEL), lambda i: (i, 0)),
        out_shape=jax.ShapeDtypeStruct((rows, D_MODEL), F32),
        compiler_params=_params(1),
        name="final_norm",
    )(x, w)


def _rope_tables(geo):
    half = RET_DK // 4
    inv = ROPE_BASE ** (-jnp.arange(half, dtype=F32) / half)
    pos = jnp.arange(geo.seq)
    ang_r = (pos // GRID_W).astype(F32)[:, None] * inv[None, :]
    ang_c = (pos % GRID_W).astype(F32)[:, None] * inv[None, :]
    cos = jnp.concatenate([jnp.cos(ang_r), jnp.cos(ang_r), jnp.cos(ang_c), jnp.cos(ang_c)], axis=1)
    sin = jnp.concatenate([-jnp.sin(ang_r), jnp.sin(ang_r), -jnp.sin(ang_c), jnp.sin(ang_c)], axis=1)
    cos = jnp.concatenate([jnp.ones((geo.tm, LANES), F32), cos], axis=0)
    sin = jnp.concatenate([jnp.zeros((geo.tm, LANES), F32), sin], axis=0)
    return cos, sin


_DT_PERM = np.array([d * SSM_HEADS + g * SSM_HPG + h
                     for g in range(SSM_GROUPS) for d in range(2) for h in range(SSM_HPG)])


def kernel(x, c, ctx, c_ctx, w_ada, b_ada, norm1_w, norm2_w, w_in, conv_w, conv_b, ret_decay_f, ret_decay_b, ret_gn_w, ssm_a_log_f, ssm_a_log_b, ssm_dt_bias_f, ssm_dt_bias_b, ssm_d, ssm_norm_w, w_ret_proj, w_ssm_proj, w_out, ffn_w_gate, ffn_w_up, ffn_w_down, moe_router, moe_w_gate, moe_w_up, moe_w_down, final_norm_w):
    batch, seq, d = x.shape
    ctx_len = ctx.shape[1]
    depth = w_ada.shape[0]
    assert d == D_MODEL and seq % GRID_W == 0
    geo = _Geom(batch, ctx_len, seq)
    tm = geo.tm

    mod_rows = -(-(batch + 1) // 8) * 8
    cvec = jnp.zeros((mod_rows, d), F32).at[0].set(c_ctx).at[1:batch + 1].set(c)
    mod = _modulation(cvec, w_ada, b_ada).reshape(depth, mod_rows, 6, d)
    cos_t, sin_t = _rope_tables(geo)

    xs = jnp.concatenate([ctx.reshape(batch * ctx_len, d), x.reshape(batch * seq, d)], axis=0)
    lat_tile0 = geo.n_ctx_tiles
    n_lat_tiles = geo.n_tiles - geo.n_ctx_tiles

    for i in range(depth):
        mod_l = mod[i]
        w_main = jnp.concatenate([w_in[i][:, :ORIG_DT_LO], w_in[i][:, ORIG_DT_HI:]], axis=1).astype(BF16)
        w_dt = w_in[i][:, ORIG_DT_LO:ORIG_DT_HI][:, _DT_PERM].astype(BF16)
        proj, dt = _inproj(geo, xs, mod_l, norm1_w[i][None, :], w_main, w_dt, cos_t, sin_t)

        xbc = _conv_silu(geo, proj, conv_w[i], conv_b[i][None, :])
        bias = jnp.concatenate([ssm_dt_bias_f[i], ssm_dt_bias_b[i]])[_DT_PERM][None, :]
        alog = jnp.concatenate([ssm_a_log_f[i], ssm_a_log_b[i]])[_DT_PERM][None, :]
        cc, cp, cct, cpt, dtt = _decay_tables(geo, dt, bias, alog)
        dskip = jnp.repeat(ssm_d[i], SSM_INNER // SSM_HEADS)[None, :]
        ys = _ssd_scan(geo, proj, xbc, cc, cp, cct, cpt, dtt, dskip, ssm_norm_w[i][None, :])
        yr = _ret_scan(geo, proj, jnp.stack([ret_decay_f[i], ret_decay_b[i]]), ret_gn_w[i][None, :])

        merged = _merge(geo, yr, ys, w_ret_proj[i].astype(BF16), w_ssm_proj[i].astype(BF16), proj)
        xs = _matmul_residual(geo, merged, w_out[i].astype(BF16), xs, mod_l, 2, 0, geo.n_tiles, 1024)

        j = i // 2
        tile0, n_tiles = (0, geo.n_tiles) if i < depth - 1 else (lat_tile0, n_lat_tiles)
        if i % 2 == 0:
            hid = _ffn_up(geo, xs, mod_l, norm2_w[i][None, :], ffn_w_gate[j].astype(BF16), ffn_w_up[j].astype(BF16))
            hid = hid[tile0 * tm:]
            xs = _matmul_residual(geo, hid, ffn_w_down[j].astype(BF16), xs, mod_l, 5, tile0, n_tiles, 512)
        else:
            router = jnp.zeros((d, LANES), F32).at[:, :N_EXPERTS].set(moe_router[j])
            xs = _moe(geo, xs, mod_l, norm2_w[i][None, :], router, moe_w_gate[j].astype(BF16),
                      moe_w_up[j].astype(BF16), moe_w_down[j].astype(BF16), tile0, n_tiles)

    lat = xs if xs.shape[0] == batch * seq else xs[batch * ctx_len:]
    return _final_norm(lat, final_norm_w[None, :], tm).reshape(batch, seq, d)
```

```python
import functools

import numpy as np
import jax
import jax.numpy as jnp
from jax import lax
from jax.experimental import pallas as pl
from jax.experimental.pallas import tpu as pltpu

F32 = jnp.float32
BF16 = jnp.bfloat16

D_MODEL = 2048
GRID_W = 64
CHUNK = 128
EPS = 1e-6
ROPE_BASE = 10000.0
RET_HEADS = 8
RET_DK = 128
RET_DV = 256
SSM_INNER = 4096
SSM_HEADS = 64
SSM_GROUPS = 8
SSM_HPG = 8
SSM_GROUP_W = SSM_INNER // SSM_GROUPS
SSM_STATE = 128
XBC_W = 6144
CONV_K = 5
D_FF = 5632
N_EXPERTS = 8
MOE_D_FF = 4096

COL_Z, COL_Q, COL_K, COL_V, COL_G, COL_XBC, COL_GATES = 0, 4096, 5120, 6144, 8192, 10240, 16384
PROJ_W = 20480
ORIG_Z_LO, ORIG_Z_HI, ORIG_DT_LO, ORIG_DT_HI = 6144, 10240, 16384, 16512
SCAN_ROWS = 256

LANES = 128
VMEM_LIMIT_BYTES = 56 * 1024 * 1024


def _params(n_axes, vmem=VMEM_LIMIT_BYTES):
    return pltpu.CompilerParams(dimension_semantics=("arbitrary",) * n_axes, vmem_limit_bytes=vmem)


def _silu(x):
    return x * jax.nn.sigmoid(x)


def _split2(x):
    hi = x.astype(BF16)
    lo = (x - hi.astype(F32)).astype(BF16)
    return hi, lo


def _split3(x):
    hi = x.astype(BF16)
    r = x - hi.astype(F32)
    mid = r.astype(BF16)
    lo = (r - mid.astype(F32)).astype(BF16)
    return hi, mid, lo


def _dot(a, b):
    return jnp.dot(a, b, preferred_element_type=F32)


def _dot_nt(a, b):
    return lax.dot_general(a, b, (((1,), (1,)), ((), ())), preferred_element_type=F32)


def _dot_tn(a, b):
    return lax.dot_general(a, b, (((0,), (0,)), ((), ())), preferred_element_type=F32)


def _norm_mod(x, nw, shift, scale):
    y = x * lax.rsqrt(jnp.mean(x * x, axis=-1, keepdims=True) + EPS)
    return (y * nw) * (1.0 + scale) + shift


def _mod_kernel(c_ref, w_ref, b_ref, o_ref):
    s_hi, s_lo = _split2(_silu(c_ref[...]))
    w_hi, w_lo = _split2(w_ref[...])
    o_ref[...] = _dot(s_hi, w_hi) + _dot(s_hi, w_lo) + _dot(s_lo, w_hi) + b_ref[...]


def _modulation(cvec, w_ada, b_ada):
    depth, d, w6 = w_ada.shape
    rows = cvec.shape[0]
    tn = 512
    return pl.pallas_call(
        _mod_kernel,
        grid=(depth, w6 // tn),
        in_specs=[
            pl.BlockSpec((rows, d), lambda l, j: (0, 0)),
            pl.BlockSpec((None, d, tn), lambda l, j: (l, 0, j)),
            pl.BlockSpec((None, 1, tn), lambda l, j: (l, 0, j)),
        ],
        out_specs=pl.BlockSpec((None, rows, tn), lambda l, j: (l, 0, j)),
        out_shape=jax.ShapeDtypeStruct((depth, rows, w6), F32),
        compiler_params=_params(2),
        name="adaln_mod",
    )(cvec, w_ada, b_ada.reshape(depth, 1, w6))


class _Geom:
    def __init__(self, batch, ctx_len, seq):
        self.batch, self.ctx_len, self.seq = batch, ctx_len, seq
        tm = 1024
        while (batch * ctx_len) % tm or seq % tm:
            tm //= 2
        assert tm >= CHUNK and ctx_len % CHUNK == 0 and seq % CHUNK == 0
        self.tm = tm
        self.ctx_rows = batch * ctx_len
        self.rows = self.ctx_rows + batch * seq
        self.n_ctx_tiles = self.ctx_rows // tm
        self.tiles_per_batch = seq // tm
        self.n_tiles = self.rows // tm
        self.step_rows = min(SCAN_ROWS, ctx_len)
        assert ctx_len % self.step_rows == 0 and seq % self.step_rows == 0 and self.step_rows % CHUNK == 0
        self.ns_ctx = ctx_len // self.step_rows
        self.ns_lat = seq // self.step_rows
        self.ns = self.ns_ctx + self.ns_lat

    def mod_row(self, i):
        return jnp.where(i < self.n_ctx_tiles, 0, 1 + (i - self.n_ctx_tiles) // self.tiles_per_batch)

    def rope_block(self, i):
        return jnp.where(i < self.n_ctx_tiles, 0, 1 + (i - self.n_ctx_tiles) % self.tiles_per_batch)

    def step_block(self, b, c):
        return jnp.where(c < self.ns_ctx, b * self.ns_ctx + c,
                         self.batch * self.ns_ctx + b * self.ns_lat + (c - self.ns_ctx))

    def backward_order(self, s):
        return jnp.where(s < self.ns_ctx, self.ns_ctx - 1 - s, self.ns - 1 - (s - self.ns_ctx))


def _rope_store(acc, cos, sin, o_ref, scale):
    lane = lax.broadcasted_iota(jnp.int32, (acc.shape[0], LANES), 1)
    first_half = (lane % 64) < 32
    for h in range(acc.shape[1] // LANES):
        xh = acc[:, h * LANES:(h + 1) * LANES]
        partner = jnp.where(first_half, pltpu.roll(xh, 96, 1), pltpu.roll(xh, 32, 1))
        o_ref[:, h * LANES:(h + 1) * LANES] = ((xh * cos + partner * sin) * scale).astype(o_ref.dtype)


def _inproj_kernel(x_ref, mod_ref, nw_ref, w_ref, wdt_ref, cos_ref, sin_ref, o_ref, dt_ref, h_ref):
    j = pl.program_id(1)

    @pl.when(j == 0)
    def _():
        h = _norm_mod(x_ref[...], nw_ref[...], mod_ref[0:1, :], mod_ref[1:2, :]).astype(BF16)
        h_ref[...] = h
        dt_ref[...] = _dot(h, wdt_ref[...])

    acc = _dot(h_ref[...], w_ref[...])

    jq, jk = COL_Q // acc.shape[1], COL_K // acc.shape[1]

    @pl.when(j == jq)
    def _():
        _rope_store(acc, cos_ref[...], sin_ref[...], o_ref, 1.0)

    @pl.when(j == jk)
    def _():
        _rope_store(acc, cos_ref[...], sin_ref[...], o_ref, RET_DK ** -0.5)

    @pl.when(jnp.logical_and(j != jq, j != jk))
    def _():
        o_ref[...] = acc.astype(o_ref.dtype)


def _inproj(geo, xs, mod_l, nw, w_main, w_dt, cos_t, sin_t):
    tm, tn = geo.tm, 1024
    return pl.pallas_call(
        _inproj_kernel,
        grid=(geo.n_tiles, PROJ_W // tn),
        in_specs=[
            pl.BlockSpec((tm, D_MODEL), lambda i, j: (i, 0)),
            pl.BlockSpec((None, 6, D_MODEL), lambda i, j: (geo.mod_row(i), 0, 0)),
            pl.BlockSpec((1, D_MODEL), lambda i, j: (0, 0)),
            pl.BlockSpec((D_MODEL, tn), lambda i, j: (0, j)),
            pl.BlockSpec((D_MODEL, LANES), lambda i, j: (0, 0)),
            pl.BlockSpec((tm, LANES), lambda i, j: (geo.rope_block(i), 0)),
            pl.BlockSpec((tm, LANES), lambda i, j: (geo.rope_block(i), 0)),
        ],
        out_specs=[
            pl.BlockSpec((tm, tn), lambda i, j: (i, j)),
            pl.BlockSpec((tm, LANES), lambda i, j: (i, 0)),
        ],
        out_shape=[
            jax.ShapeDtypeStruct((geo.rows, PROJ_W), BF16),
            jax.ShapeDtypeStruct((geo.rows, LANES), F32),
        ],
        scratch_shapes=[pltpu.VMEM((tm, D_MODEL), BF16)],
        compiler_params=_params(2),
        name="in_proj",
    )(xs, mod_l, nw, w_main, w_dt, cos_t, sin_t)


CONV_HALO = 16


def _conv_kernel(prev_ref, main_ref, next_ref, w_ref, b_ref, o_ref, ext_ref, *, geo, rows):
    i = pl.program_id(0)
    start = i * rows
    in_ctx = start < geo.ctx_rows
    seq_len = jnp.where(in_ctx, geo.ctx_len, geo.seq)
    off = jnp.where(in_ctx, start, start - geo.ctx_rows)
    first = (off % seq_len) == 0
    last = ((off + rows) % seq_len) == 0
    ext_ref[0:CONV_HALO, :] = jnp.where(first, 0.0, prev_ref[...].astype(F32))
    ext_ref[CONV_HALO:CONV_HALO + rows, :] = main_ref[...].astype(F32)
    ext_ref[CONV_HALO + rows:, :] = jnp.where(last, 0.0, next_ref[...].astype(F32))
    acc = jnp.broadcast_to(b_ref[...], (rows, b_ref.shape[1]))
    for k in range(CONV_K):
        acc = acc + ext_ref[pl.ds(CONV_HALO + k - CONV_K // 2, rows), :] * w_ref[k:k + 1, :]
    o_ref[...] = _silu(acc).astype(o_ref.dtype)


def _conv_silu(geo, proj, conv_w, conv_b):
    rows = min(256, geo.ctx_len)
    tc = 1024
    col0 = COL_XBC // tc
    n_halo = geo.rows // CONV_HALO
    per = rows // CONV_HALO
    return pl.pallas_call(
        functools.partial(_conv_kernel, geo=geo, rows=rows),
        grid=(geo.rows // rows, XBC_W // tc),
        in_specs=[
            pl.BlockSpec((CONV_HALO, tc), lambda i, j: (jnp.maximum(i * per - 1, 0), col0 + j)),
            pl.BlockSpec((rows, tc), lambda i, j: (i, col0 + j)),
            pl.BlockSpec((CONV_HALO, tc), lambda i, j: (jnp.minimum((i + 1) * per, n_halo - 1), col0 + j)),
            pl.BlockSpec((CONV_K, tc), lambda i, j: (0, j)),
            pl.BlockSpec((1, tc), lambda i, j: (0, j)),
        ],
        out_specs=pl.BlockSpec((rows, tc), lambda i, j: (i, j)),
        out_shape=jax.ShapeDtypeStruct((geo.rows, XBC_W), BF16),
        scratch_shapes=[pltpu.VMEM((rows + 2 * CONV_HALO, tc), F32)],
        compiler_params=_params(2),
        name="ssm_conv",
    )(proj, proj, proj, conv_w, conv_b)


def _decay_kernel(dt_ref, bias_ref, alog_ref, cc_ref, cp_ref, cpt_ref, dtt_ref):
    x = dt_ref[...] + bias_ref[...]
    dtv = jnp.maximum(x, 0.0) + jnp.log1p(jnp.exp(-jnp.abs(x)))
    la = dtv * (-jnp.exp(alog_ref[...]))
    ii = lax.broadcasted_iota(jnp.int32, (CHUNK, CHUNK), 0)
    jj = lax.broadcasted_iota(jnp.int32, (CHUNK, CHUNK), 1)
    lower = (jj <= ii).astype(BF16)
    upper = (jj >= ii).astype(BF16)
    hi, mid, lo = _split3(la)
    cum_f = _dot(lower, hi) + _dot(lower, mid) + _dot(lower, lo)
    cum_b = _dot(upper, hi) + _dot(upper, mid) + _dot(upper, lo)
    lane = lax.broadcasted_iota(jnp.int32, (CHUNK, LANES), 1)
    cc = jnp.where((lane % 16) < 8, cum_f, cum_b)
    cp = cc - jnp.log(dtv)
    cc_ref[...] = cc
    cp_ref[...] = cp
    cpt_ref[...] = cp.T
    dtt_ref[...] = dtv.T


def _decay_tables(geo, dt, bias, alog):
    n_chunks = geo.rows // CHUNK
    col = pl.BlockSpec((CHUNK, LANES), lambda i: (i, 0))
    row = pl.BlockSpec((None, LANES, CHUNK), lambda i: (i, 0, 0))
    vec = pl.BlockSpec((1, LANES), lambda i: (0, 0))
    col_shape = jax.ShapeDtypeStruct((geo.rows, LANES), F32)
    row_shape = jax.ShapeDtypeStruct((n_chunks, LANES, CHUNK), F32)
    return pl.pallas_call(
        _decay_kernel,
        grid=(n_chunks,),
        in_specs=[col, vec, vec],
        out_specs=[col, col, row, row],
        out_shape=[col_shape, col_shape, row_shape, row_shape],
        compiler_params=_params(1),
        name="ssm_decay",
    )(dt, bias, alog)


def _ssd_kernel(q_ref, k_ref, xs_ref, z_ref, cc_ref, cp_ref, cct_ref, cpt_ref, dtt_ref, dskip_ref,
                nw_ref, o_ref, sf_ref, sb_ref, hist_ref, *, geo):
    g, ph, s = pl.program_id(1), pl.program_id(2), pl.program_id(3)
    c = jnp.where(ph == 0, geo.backward_order(s), s)
    shift = (LANES - 16 * g) % LANES
    cc = pltpu.roll(cc_ref[...], shift, 1)
    cp = pltpu.roll(cp_ref[...], shift, 1)
    k = k_ref[...]
    xs = xs_ref[...]
    lane = lax.broadcasted_iota(jnp.int32, (CHUNK, LANES), 1)
    left = lane < 64
    left_row = lax.broadcasted_iota(jnp.int32, (1, LANES), 1) < 64
    n_pairs = SSM_HPG // 2

    def pair_cols(a, h1, h2):
        return jnp.where(left, a[:, h1:h1 + 1], a[:, h2:h2 + 1])

    @pl.when(jnp.logical_and(ph == 0, s == 0))
    def _():
        sb_ref[...] = jnp.zeros_like(sb_ref)

    @pl.when(jnp.logical_and(ph == 1, s == 0))
    def _():
        sf_ref[...] = jnp.zeros_like(sf_ref)

    @pl.when(ph == 0)
    def _():
        hist_ref[c] = sb_ref[...].astype(BF16)
        w_all = jnp.exp(cc[0:1, :] - cp)
        a_all = jnp.exp(cc[0:1, :])
        for p in range(n_pairs):
            h1, h2 = SSM_HPG + 2 * p, SSM_HPG + 2 * p + 1
            sl = slice(p * LANES, (p + 1) * LANES)
            wb = pair_cols(w_all, h1, h2)
            ab = jnp.where(left_row, a_all[:, h1:h1 + 1], a_all[:, h2:h2 + 1])
            vw = (xs[:, sl].astype(F32) * wb).astype(BF16)
            sb_ref[:, sl] = ab * sb_ref[:, sl] + _dot_tn(k, vw)

    @pl.when(ph == 1)
    def _():
        q = q_ref[...]
        cct = cct_ref[...]
        cpt = cpt_ref[...]
        dtt = dtt_ref[...]
        scores = _dot_nt(q, k)
        ii = lax.broadcasted_iota(jnp.int32, (CHUNK, CHUNK), 0)
        jj = lax.broadcasted_iota(jnp.int32, (CHUNK, CHUNK), 1)
        lower = jj <= ii
        diag = jj == ii
        yf = _dot(q, sf_ref[...].astype(BF16))
        yb = _dot(q, hist_ref[c])
        e_all = jnp.exp(cc)
        w_all = jnp.exp(cc[CHUNK - 1:CHUNK, :] - cp)
        a_all = jnp.exp(cc[CHUNK - 1:CHUNK, :])
        ys = []
        for p in range(n_pairs):
            sl = slice(p * LANES, (p + 1) * LANES)
            vp = xs[:, sl]
            probs = []
            for h in (2 * p, 2 * p + 1):
                hb = SSM_HPG + h
                arg = jnp.where(lower, cc[:, h:h + 1] - cpt[h:h + 1, :], cc[:, hb:hb + 1] - cpt[hb:hb + 1, :])
                e = jnp.exp(arg) + jnp.where(diag, dtt[hb:hb + 1, :], 0.0)
                probs.append((scores * e).astype(BF16))
            zero = jnp.zeros_like(vp)
            rhs = jnp.concatenate([jnp.where(left, vp, zero), jnp.where(left, zero, vp)], axis=0)
            y = _dot(jnp.concatenate(probs, axis=1), rhs)
            h1, h2 = 2 * p, 2 * p + 1
            y = y + pair_cols(e_all, h1, h2) * yf[:, sl] + pair_cols(e_all, SSM_HPG + h1, SSM_HPG + h2) * yb[:, sl]
            ys.append(y)
            wf = pair_cols(w_all, h1, h2)
            af = jnp.where(left_row, a_all[:, h1:h1 + 1], a_all[:, h2:h2 + 1])
            vw = (vp.astype(F32) * wf).astype(BF16)
            sf_ref[:, sl] = af * sf_ref[:, sl] + _dot_tn(k, vw)
        y = jnp.concatenate(ys, axis=1)
        y = y + dskip_ref[...] * xs.astype(F32)
        y = y * _silu(z_ref[...].astype(F32))
        y = y * lax.rsqrt(jnp.mean(y * y, axis=-1, keepdims=True) + EPS)
        o_ref[...] = (y * nw_ref[...]).astype(o_ref.dtype)


def _ssd_scan(geo, proj, xbc, cc, cp, cct, cpt, dtt, dskip, norm_w):
    gw = SSM_GROUP_W

    def blk(b, c):
        return geo.chunk_block(b, c)

    def cur(b, g, ph, s):
        return blk(b, jnp.where(ph == 0, geo.backward_order(s), s))

    def out_blk(b, g, ph, s):
        return blk(b, jnp.where(ph == 0, 0, s))

    row_spec = pl.BlockSpec((None, 16, CHUNK), lambda b, g, ph, s: (cur(b, g, ph, s), g, 0))
    col_spec = pl.BlockSpec((CHUNK, LANES), lambda b, g, ph, s: (cur(b, g, ph, s), 0))
    return pl.pallas_call(
        functools.partial(_ssd_kernel, geo=geo),
        grid=(geo.batch, SSM_GROUPS, 2, geo.nc),
        in_specs=[
            pl.BlockSpec((CHUNK, SSM_STATE), lambda b, g, ph, s: (cur(b, g, ph, s), (SSM_INNER + 1024) // SSM_STATE + g)),
            pl.BlockSpec((CHUNK, SSM_STATE), lambda b, g, ph, s: (cur(b, g, ph, s), SSM_INNER // SSM_STATE + g)),
            pl.BlockSpec((CHUNK, gw), lambda b, g, ph, s: (cur(b, g, ph, s), g)),
            pl.BlockSpec((CHUNK, gw), lambda b, g, ph, s: (cur(b, g, ph, s), COL_Z // gw + g)),
            col_spec, col_spec, row_spec, row_spec, row_spec,
            pl.BlockSpec((1, gw), lambda b, g, ph, s: (0, g)),
            pl.BlockSpec((1, gw), lambda b, g, ph, s: (0, g)),
        ],
        out_specs=pl.BlockSpec((CHUNK, gw), lambda b, g, ph, s: (out_blk(b, g, ph, s), g)),
        out_shape=jax.ShapeDtypeStruct((geo.rows, SSM_INNER), BF16),
        scratch_shapes=[
            pltpu.VMEM((SSM_STATE, gw), F32),
            pltpu.VMEM((SSM_STATE, gw), F32),
            pltpu.VMEM((geo.nc, SSM_STATE, gw), BF16),
        ],
        compiler_params=_params(4),
        name="ssd_scan",
    )(xbc, xbc, xbc, proj, cc, cp, cct, cpt, dtt, dskip, norm_w)


def _ret_kernel(dec_ref, q_ref, k_ref, v_ref, g_ref, gnw_ref, o_ref,
                sf_ref, sb_ref, hist_ref, m_ref, ey_ref, wk_ref, a_ref, *, geo):
    h, ph, s = pl.program_id(1), pl.program_id(2), pl.program_id(3)
    c = jnp.where(ph == 0, geo.backward_order(s), s)
    n = float(CHUNK)

    @pl.when(jnp.logical_and(ph == 0, s == 0))
    def _():
        lam_f = jnp.exp(jnp.full((CHUNK, 1), dec_ref[0, h], F32))
        lam_b = jnp.exp(jnp.full((CHUNK, 1), dec_ref[1, h], F32))
        ii = lax.broadcasted_iota(jnp.int32, (CHUNK, CHUNK), 0)
        jj = lax.broadcasted_iota(jnp.int32, (CHUNK, CHUNK), 1)
        dist = (ii - jj).astype(F32)
        m_ref[...] = (jnp.where(jj <= ii, jnp.exp(-lam_f * dist), 0.0)
                      + jnp.where(jj >= ii, jnp.exp(lam_b * dist), 0.0))
        row = lax.broadcasted_iota(jnp.int32, (CHUNK, 1), 0).astype(F32)
        ey_ref[0] = jnp.broadcast_to(jnp.exp(-lam_f * (row + 1.0)), (CHUNK, RET_DV))
        ey_ref[1] = jnp.broadcast_to(jnp.exp(-lam_b * (n - row)), (CHUNK, RET_DV))
        wk_ref[0] = jnp.broadcast_to(jnp.exp(-lam_f * (n - 1.0 - row)), (CHUNK, RET_DK))
        wk_ref[1] = jnp.broadcast_to(jnp.exp(-lam_b * row), (CHUNK, RET_DK))
        a_ref[0] = jnp.broadcast_to(jnp.exp(-lam_f[0:8, :] * n), (8, RET_DV))
        a_ref[1] = jnp.broadcast_to(jnp.exp(-lam_b[0:8, :] * n), (8, RET_DV))
        sb_ref[...] = jnp.zeros_like(sb_ref)

    @pl.when(jnp.logical_and(ph == 1, s == 0))
    def _():
        sf_ref[...] = jnp.zeros_like(sf_ref)

    k = k_ref[...]
    v = v_ref[...]

    @pl.when(ph == 0)
    def _():
        hist_ref[c] = sb_ref[...].astype(BF16)
        kw = (k.astype(F32) * wk_ref[1]).astype(BF16)
        sb_ref[...] = a_ref[1][0:1, :] * sb_ref[...] + _dot_tn(kw, v)

    @pl.when(ph == 1)
    def _():
        q = q_ref[...]
        probs = (_dot_nt(q, k) * m_ref[...]).astype(BF16)
        y = _dot(probs, v)
        y = y + ey_ref[0] * _dot(q, sf_ref[...].astype(BF16)) + ey_ref[1] * _dot(q, hist_ref[c])
        kw = (k.astype(F32) * wk_ref[0]).astype(BF16)
        sf_ref[...] = a_ref[0][0:1, :] * sf_ref[...] + _dot_tn(kw, v)
        mu = jnp.mean(y, axis=-1, keepdims=True)
        yc = y - mu
        var = jnp.mean(yc * yc, axis=-1, keepdims=True)
        yn = yc * lax.rsqrt(var + EPS)
        o_ref[...] = (yn * gnw_ref[...] * _silu(g_ref[...].astype(F32))).astype(o_ref.dtype)


def _ret_scan(geo, proj, decays, gn_w):
    def cur(b, h, ph, s):
        return geo.chunk_block(b, jnp.where(ph == 0, geo.backward_order(s), s))

    def out_blk(b, h, ph, s):
        return geo.chunk_block(b, jnp.where(ph == 0, 0, s))

    return pl.pallas_call(
        functools.partial(_ret_kernel, geo=geo),
        grid=(geo.batch, RET_HEADS, 2, geo.nc),
        in_specs=[
            pl.BlockSpec(memory_space=pltpu.SMEM),
            pl.BlockSpec((CHUNK, RET_DK), lambda b, h, ph, s: (cur(b, h, ph, s), COL_Q // RET_DK + h)),
            pl.BlockSpec((CHUNK, RET_DK), lambda b, h, ph, s: (cur(b, h, ph, s), COL_K // RET_DK + h)),
            pl.BlockSpec((CHUNK, RET_DV), lambda b, h, ph, s: (cur(b, h, ph, s), COL_V // RET_DV + h)),
            pl.BlockSpec((CHUNK, RET_DV), lambda b, h, ph, s: (cur(b, h, ph, s), COL_G // RET_DV + h)),
            pl.BlockSpec((1, RET_DV), lambda b, h, ph, s: (0, h)),
        ],
        out_specs=pl.BlockSpec((CHUNK, RET_DV), lambda b, h, ph, s: (out_blk(b, h, ph, s), h)),
        out_shape=jax.ShapeDtypeStruct((geo.rows, RET_HEADS * RET_DV), BF16),
        scratch_shapes=[
            pltpu.VMEM((RET_DK, RET_DV), F32),
            pltpu.VMEM((RET_DK, RET_DV), F32),
            pltpu.VMEM((geo.nc, RET_DK, RET_DV), BF16),
            pltpu.VMEM((CHUNK, CHUNK), F32),
            pltpu.VMEM((2, CHUNK, RET_DV), F32),
            pltpu.VMEM((2, CHUNK, RET_DK), F32),
            pltpu.VMEM((2, 8, RET_DV), F32),
        ],
        compiler_params=_params(4),
        name="ret_scan",
    )(decays, proj, proj, proj, proj, gn_w)


def _lanes2(a):
    return jnp.concatenate([a, a], axis=1)


def _ret_tables(dec_ref, direction, rows, wk_ref, a_ref, e_ref=None, m_ref=None):
    n = float(rows)
    row = lax.broadcasted_iota(jnp.int32, (rows, LANES), 0).astype(F32)
    for h in range(RET_HEADS):
        lam = jnp.exp(jnp.full((rows, LANES), dec_ref[direction, h], F32))
        if direction == 0:
            wk_ref[h] = jnp.exp(-lam * (n - 1.0 - row))
            if e_ref is not None:
                e_ref[0, h] = jnp.exp(-lam * (row + 1.0))
        else:
            wk_ref[h] = jnp.exp(-lam * row)
        a_ref[h] = _lanes2(jnp.exp(-lam[0:8, :] * n))


def _ret_state_kernel(dec_ref, k_ref, v_ref, hist_ref, sb_ref, wk_ref, a_ref, *, rows):
    @pl.when(pl.program_id(1) == 0)
    def _():
        sb_ref[...] = jnp.zeros_like(sb_ref)
        _ret_tables(dec_ref, 1, rows, wk_ref, a_ref)

    def body(h, carry):
        kh = k_ref[:, pl.ds(pl.multiple_of(h * RET_DK, RET_DK), RET_DK)]
        vh = v_ref[:, pl.ds(pl.multiple_of(h * RET_DV, RET_DV), RET_DV)]
        hist_ref[h] = sb_ref[h].astype(BF16)
        kw = (kh.astype(F32) * wk_ref[h]).astype(BF16)
        sb_ref[h] = a_ref[h][0:1, :] * sb_ref[h] + _dot_tn(kw, vh)
        return carry

    lax.fori_loop(0, RET_HEADS, body, 0)


def _ret_out_kernel(dec_ref, q_ref, k_ref, v_ref, g_ref, hist_ref, gnw_ref, o_ref,
                    sf_ref, wk_ref, a_ref, e_ref, m_ref, *, rows):
    @pl.when(pl.program_id(1) == 0)
    def _():
        sf_ref[...] = jnp.zeros_like(sf_ref)
        _ret_tables(dec_ref, 0, rows, wk_ref, a_ref, e_ref)
        n = float(rows)
        row = lax.broadcasted_iota(jnp.int32, (rows, LANES), 0).astype(F32)
        ii = lax.broadcasted_iota(jnp.int32, (rows, rows), 0)
        jj = lax.broadcasted_iota(jnp.int32, (rows, rows), 1)
        dist = (ii - jj).astype(F32)
        for h in range(RET_HEADS):
            lam_f = jnp.exp(jnp.full((rows, 1), dec_ref[0, h], F32))
            lam_b = jnp.exp(jnp.full((rows, 1), dec_ref[1, h], F32))
            m_ref[h] = (jnp.where(jj <= ii, jnp.exp(-lam_f * dist), 0.0)
                        + jnp.where(jj >= ii, jnp.exp(lam_b * dist), 0.0))
            e_ref[1, h] = jnp.exp(-jnp.exp(jnp.full((rows, LANES), dec_ref[1, h], F32)) * (n - row))

    def body(h, carry):
        ok = pl.multiple_of(h * RET_DK, RET_DK)
        ov = pl.multiple_of(h * RET_DV, RET_DV)
        qh = q_ref[:, pl.ds(ok, RET_DK)]
        kh = k_ref[:, pl.ds(ok, RET_DK)]
        vh = v_ref[:, pl.ds(ov, RET_DV)]
        probs = (_dot_nt(qh, kh) * m_ref[h]).astype(BF16)
        y = _dot(probs, vh)
        y = y + _lanes2(e_ref[0, h]) * _dot(qh, sf_ref[h].astype(BF16)) + _lanes2(e_ref[1, h]) * _dot(qh, hist_ref[h])
        kw = (kh.astype(F32) * wk_ref[h]).astype(BF16)
        sf_ref[h] = a_ref[h][0:1, :] * sf_ref[h] + _dot_tn(kw, vh)
        mu = jnp.mean(y, axis=-1, keepdims=True)
        yc = y - mu
        yn = yc * lax.rsqrt(jnp.mean(yc * yc, axis=-1, keepdims=True) + EPS)
        gate = _silu(g_ref[:, pl.ds(ov, RET_DV)].astype(F32))
        o_ref[:, pl.ds(ov, RET_DV)] = (yn * gnw_ref[:, pl.ds(ov, RET_DV)] * gate).astype(o_ref.dtype)
        return carry

    lax.fori_loop(0, RET_HEADS, body, 0)


def _ret_scan2(geo, proj, decays, gn_w):
    rows = geo.step_rows
    qk_w, v_w = RET_HEADS * RET_DK, RET_HEADS * RET_DV

    def bwd(b, s):
        return geo.step_block(b, geo.backward_order(s))

    def fwd(b, s):
        return geo.step_block(b, s)

    smem = pl.BlockSpec(memory_space=pltpu.SMEM)
    hist_shape = (geo.batch, geo.ns, RET_HEADS, RET_DK, RET_DV)
    hist = pl.pallas_call(
        functools.partial(_ret_state_kernel, rows=rows),
        grid=(geo.batch, geo.ns),
        in_specs=[
            smem,
            pl.BlockSpec((rows, qk_w), lambda b, s: (bwd(b, s), COL_K // qk_w)),
            pl.BlockSpec((rows, v_w), lambda b, s: (bwd(b, s), COL_V // v_w)),
        ],
        out_specs=pl.BlockSpec((None, None) + hist_shape[2:], lambda b, s: (b, geo.backward_order(s), 0, 0, 0)),
        out_shape=jax.ShapeDtypeStruct(hist_shape, BF16),
        scratch_shapes=[
            pltpu.VMEM((RET_HEADS, RET_DK, RET_DV), F32),
            pltpu.VMEM((RET_HEADS, rows, LANES), F32),
            pltpu.VMEM((RET_HEADS, 8, RET_DV), F32),
        ],
        compiler_params=_params(2),
        name="ret_state",
    )(decays, proj, proj)
    return pl.pallas_call(
        functools.partial(_ret_out_kernel, rows=rows),
        grid=(geo.batch, geo.ns),
        in_specs=[
            smem,
            pl.BlockSpec((rows, qk_w), lambda b, s: (fwd(b, s), COL_Q // qk_w)),
            pl.BlockSpec((rows, qk_w), lambda b, s: (fwd(b, s), COL_K // qk_w)),
            pl.BlockSpec((rows, v_w), lambda b, s: (fwd(b, s), COL_V // v_w)),
            pl.BlockSpec((rows, v_w), lambda b, s: (fwd(b, s), COL_G // v_w)),
            pl.BlockSpec((None, None) + hist_shape[2:], lambda b, s: (b, s, 0, 0, 0)),
            pl.BlockSpec((1, v_w), lambda b, s: (0, 0)),
        ],
        out_specs=pl.BlockSpec((rows, v_w), lambda b, s: (fwd(b, s), 0)),
        out_shape=jax.ShapeDtypeStruct((geo.rows, v_w), BF16),
        scratch_shapes=[
            pltpu.VMEM((RET_HEADS, RET_DK, RET_DV), F32),
            pltpu.VMEM((RET_HEADS, rows, LANES), F32),
            pltpu.VMEM((RET_HEADS, 8, RET_DV), F32),
            pltpu.VMEM((2, RET_HEADS, rows, LANES), F32),
            pltpu.VMEM((RET_HEADS, rows, rows), F32),
        ],
        compiler_params=_params(2),
        name="ret_out",
    )(decays, proj, proj, proj, proj, hist, gn_w)


def _group_cols(ref, rows, g):
    return pltpu.roll(ref[rows, :], (LANES - 16 * g) % LANES, 1)


def _pair_cols(left, a, h1, h2):
    return jnp.where(left, a[:, h1:h1 + 1], a[:, h2:h2 + 1])


def _ssd_state_kernel(k_ref, xs_ref, cc_ref, cp_ref, hist_ref, sb_ref, *, cps):
    @pl.when(pl.program_id(1) == 0)
    def _():
        sb_ref[...] = jnp.zeros_like(sb_ref)

    lane = lax.broadcasted_iota(jnp.int32, (CHUNK, LANES), 1)
    left = lane < 64
    left_row = lax.broadcasted_iota(jnp.int32, (1, LANES), 1) < 64

    def body(g, carry):
        ok = pl.multiple_of(g * SSM_STATE, SSM_STATE)
        ov = pl.multiple_of(g * SSM_GROUP_W, SSM_GROUP_W)
        for ci in reversed(range(cps)):
            rows = pl.ds(ci * CHUNK, CHUNK)
            cc = _group_cols(cc_ref, rows, g)
            cp = _group_cols(cp_ref, rows, g)
            k = k_ref[rows, pl.ds(ok, SSM_STATE)]
            hist_ref[ci, g] = sb_ref[g].astype(BF16)
            w_all = jnp.exp(cc[0:1, :] - cp)
            a_all = jnp.exp(cc[0:1, :])
            for p in range(SSM_HPG // 2):
                h1, h2 = SSM_HPG + 2 * p, SSM_HPG + 2 * p + 1
                sl = slice(p * LANES, (p + 1) * LANES)
                vp = xs_ref[rows, pl.ds(pl.multiple_of(ov + p * LANES, LANES), LANES)]
                vw = (vp.astype(F32) * _pair_cols(left, w_all, h1, h2)).astype(BF16)
                ab = jnp.where(left_row, a_all[:, h1:h1 + 1], a_all[:, h2:h2 + 1])
                sb_ref[g, :, sl] = ab * sb_ref[g, :, sl] + _dot_tn(k, vw)
        return carry

    lax.fori_loop(0, SSM_GROUPS, body, 0)


def _ssd_out_kernel(q_ref, k_ref, xs_ref, z_ref, cc_ref, cp_ref, cpt_ref, dtt_ref, hist_ref,
                    dskip_ref, nw_ref, o_ref, sf_ref, *, cps):
    @pl.when(pl.program_id(1) == 0)
    def _():
        sf_ref[...] = jnp.zeros_like(sf_ref)

    lane = lax.broadcasted_iota(jnp.int32, (CHUNK, LANES), 1)
    left = lane < 64
    left_row = lax.broadcasted_iota(jnp.int32, (1, LANES), 1) < 64
    ii = lax.broadcasted_iota(jnp.int32, (CHUNK, CHUNK), 0)
    jj = lax.broadcasted_iota(jnp.int32, (CHUNK, CHUNK), 1)
    lower = jj <= ii
    diag = jj == ii

    def body(g, carry):
        ok = pl.multiple_of(g * SSM_STATE, SSM_STATE)
        ov = pl.multiple_of(g * SSM_GROUP_W, SSM_GROUP_W)
        og = pl.multiple_of(g * 16, 16)
        for ci in range(cps):
            rows = pl.ds(ci * CHUNK, CHUNK)
            cc = _group_cols(cc_ref, rows, g)
            cp = _group_cols(cp_ref, rows, g)
            cpt = cpt_ref[ci, pl.ds(og, 16), :]
            dtt = dtt_ref[ci, pl.ds(og, 16), :]
            q = q_ref[rows, pl.ds(ok, SSM_STATE)]
            k = k_ref[rows, pl.ds(ok, SSM_STATE)]
            scores = _dot_nt(q, k)
            yf = _dot(q, sf_ref[g].astype(BF16))
            yb = _dot(q, hist_ref[ci, g])
            w_all = jnp.exp(cc[CHUNK - 1:CHUNK, :] - cp)
            a_all = jnp.exp(cc[CHUNK - 1:CHUNK, :])
            ys = []
            for p in range(SSM_HPG // 2):
                h1, h2 = 2 * p, 2 * p + 1
                sl = slice(p * LANES, (p + 1) * LANES)
                cols = pl.ds(pl.multiple_of(ov + p * LANES, LANES), LANES)
                vp = xs_ref[rows, cols]
                probs, ef, eb = [], [], []
                for h in (h1, h2):
                    hb = SSM_HPG + h
                    cf = jnp.broadcast_to(cc[:, h:h + 1], (CHUNK, CHUNK))
                    cb = jnp.broadcast_to(cc[:, hb:hb + 1], (CHUNK, CHUNK))
                    arg = jnp.where(lower, cf - cpt[h:h + 1, :], cb - cpt[hb:hb + 1, :])
                    e = jnp.exp(arg) + jnp.where(diag, dtt[hb:hb + 1, :], 0.0)
                    probs.append((scores * e).astype(BF16))
                    ef.append(jnp.exp(cf))
                    eb.append(jnp.exp(cb))
                zero = jnp.zeros_like(vp)
                rhs = jnp.concatenate([jnp.where(left, vp, zero), jnp.where(left, zero, vp)], axis=0)
                y = _dot(jnp.concatenate(probs, axis=1), rhs)
                y = (y + jnp.where(left, ef[0], ef[1]) * yf[:, sl] + jnp.where(left, eb[0], eb[1]) * yb[:, sl])
                vw = (vp.astype(F32) * _pair_cols(left, w_all, h1, h2)).astype(BF16)
                af = jnp.where(left_row, a_all[:, h1:h1 + 1], a_all[:, h2:h2 + 1])
                sf_ref[g, :, sl] = af * sf_ref[g, :, sl] + _dot_tn(k, vw)
                ys.append(y + dskip_ref[:, cols] * vp.astype(F32))
            gcols = pl.ds(ov, SSM_GROUP_W)
            y = jnp.concatenate(ys, axis=1) * _silu(z_ref[rows, gcols].astype(F32))
            y = y * lax.rsqrt(jnp.mean(y * y, axis=-1, keepdims=True) + EPS)
            o_ref[rows, gcols] = (y * nw_ref[:, gcols]).astype(o_ref.dtype)
        return carry

    lax.fori_loop(0, SSM_GROUPS, body, 0)


def _ssd_scan2(geo, proj, xbc, cc, cp, cpt, dtt, dskip, norm_w):
    rows = geo.step_rows
    cps = rows // CHUNK
    bc_w = SSM_GROUPS * SSM_STATE

    def bwd(b, s):
        return geo.step_block(b, geo.backward_order(s))

    def fwd(b, s):
        return geo.step_block(b, s)

    hist_shape = (geo.batch, geo.ns * cps, SSM_GROUPS, SSM_STATE, SSM_GROUP_W)
    hist_block = (None, cps) + hist_shape[2:]
    hist = pl.pallas_call(
        functools.partial(_ssd_state_kernel, cps=cps),
        grid=(geo.batch, geo.ns),
        in_specs=[
            pl.BlockSpec((rows, bc_w), lambda b, s: (bwd(b, s), SSM_INNER // bc_w)),
            pl.BlockSpec((rows, SSM_INNER), lambda b, s: (bwd(b, s), 0)),
            pl.BlockSpec((rows, LANES), lambda b, s: (bwd(b, s), 0)),
            pl.BlockSpec((rows, LANES), lambda b, s: (bwd(b, s), 0)),
        ],
        out_specs=pl.BlockSpec(hist_block, lambda b, s: (b, geo.backward_order(s), 0, 0, 0)),
        out_shape=jax.ShapeDtypeStruct(hist_shape, BF16),
        scratch_shapes=[pltpu.VMEM((SSM_GROUPS, SSM_STATE, SSM_GROUP_W), F32)],
        compiler_params=_params(2),
        name="ssd_state",
    )(xbc, xbc, cc, cp)
    col = pl.BlockSpec((rows, LANES), lambda b, s: (fwd(b, s), 0))
    row = pl.BlockSpec((cps, LANES, CHUNK), lambda b, s: (fwd(b, s), 0, 0))
    return pl.pallas_call(
        functools.partial(_ssd_out_kernel, cps=cps),
        grid=(geo.batch, geo.ns),
        in_specs=[
            pl.BlockSpec((rows, bc_w), lambda b, s: (fwd(b, s), SSM_INNER // bc_w + 1)),
            pl.BlockSpec((rows, bc_w), lambda b, s: (fwd(b, s), SSM_INNER // bc_w)),
            pl.BlockSpec((rows, SSM_INNER), lambda b, s: (fwd(b, s), 0)),
            pl.BlockSpec((rows, SSM_INNER), lambda b, s: (fwd(b, s), COL_Z // SSM_INNER)),
            col, col, row, row,
            pl.BlockSpec(hist_block, lambda b, s: (b, s, 0, 0, 0)),
            pl.BlockSpec((1, SSM_INNER), lambda b, s: (0, 0)),
            pl.BlockSpec((1, SSM_INNER), lambda b, s: (0, 0)),
        ],
        out_specs=pl.BlockSpec((rows, SSM_INNER), lambda b, s: (fwd(b, s), 0)),
        out_shape=jax.ShapeDtypeStruct((geo.rows, SSM_INNER), BF16),
        scratch_shapes=[pltpu.VMEM((SSM_GROUPS, SSM_STATE, SSM_GROUP_W), F32)],
        compiler_params=_params(2),
        name="ssd_out",
    )(xbc, xbc, xbc, proj, cc, cp, cpt, dtt, hist, dskip, norm_w)


def _merge_kernel(yr_ref, ys_ref, wr_ref, ws_ref, gr_ref, gs_ref, o_ref):
    r = _dot(yr_ref[...], wr_ref[...])
    s = _dot(ys_ref[...], ws_ref[...])
    m = jax.nn.sigmoid(gr_ref[...].astype(F32)) * r + jax.nn.sigmoid(gs_ref[...].astype(F32)) * s
    o_ref[...] = m.astype(o_ref.dtype)


def _merge(geo, yr, ys, w_ret, w_ssm, proj):
    tm, tn = geo.tm, 512
    gr0 = COL_GATES // tn
    gs0 = (COL_GATES + D_MODEL) // tn
    return pl.pallas_call(
        _merge_kernel,
        grid=(geo.n_tiles, D_MODEL // tn),
        in_specs=[
            pl.BlockSpec((tm, yr.shape[1]), lambda i, j: (i, 0)),
            pl.BlockSpec((tm, ys.shape[1]), lambda i, j: (i, 0)),
            pl.BlockSpec((yr.shape[1], tn), lambda i, j: (0, j)),
            pl.BlockSpec((ys.shape[1], tn), lambda i, j: (0, j)),
            pl.BlockSpec((tm, tn), lambda i, j: (i, gr0 + j)),
            pl.BlockSpec((tm, tn), lambda i, j: (i, gs0 + j)),
        ],
        out_specs=pl.BlockSpec((tm, tn), lambda i, j: (i, j)),
        out_shape=jax.ShapeDtypeStruct((geo.rows, D_MODEL), BF16),
        compiler_params=_params(2),
        name="branch_merge",
    )(yr, ys, w_ret, w_ssm, proj, proj)


def _residual_kernel(a_ref, w_ref, x_ref, mod_ref, o_ref, *, gate_row):
    o_ref[...] = x_ref[...] + mod_ref[gate_row:gate_row + 1, :] * _dot(a_ref[...], w_ref[...])


def _matmul_residual(geo, a, w, xs, mod_l, gate_row, tile0, n_tiles, tn):
    tm = geo.tm
    kdim = a.shape[1]
    return pl.pallas_call(
        functools.partial(_residual_kernel, gate_row=gate_row),
        grid=(n_tiles, D_MODEL // tn),
        in_specs=[
            pl.BlockSpec((tm, kdim), lambda i, j: (i, 0)),
            pl.BlockSpec((kdim, tn), lambda i, j: (0, j)),
            pl.BlockSpec((tm, tn), lambda i, j: (tile0 + i, j)),
            pl.BlockSpec((None, 6, tn), lambda i, j: (geo.mod_row(tile0 + i), 0, j)),
        ],
        out_specs=pl.BlockSpec((tm, tn), lambda i, j: (i, j)),
        out_shape=jax.ShapeDtypeStruct((n_tiles * tm, D_MODEL), F32),
        compiler_params=_params(2),
        name="proj_residual",
    )(a, w, xs, mod_l)


def _ffn_up_kernel(x_ref, mod_ref, nw_ref, wg_ref, wu_ref, o_ref, h_ref):
    @pl.when(pl.program_id(1) == 0)
    def _():
        h_ref[...] = _norm_mod(x_ref[...], nw_ref[...], mod_ref[3:4, :], mod_ref[4:5, :]).astype(BF16)

    h = h_ref[...]
    o_ref[...] = (_silu(_dot(h, wg_ref[...])) * _dot(h, wu_ref[...])).astype(o_ref.dtype)


def _ffn_up(geo, xs, mod_l, nw, w_gate, w_up):
    tm, tf = geo.tm, 512
    return pl.pallas_call(
        _ffn_up_kernel,
        grid=(geo.n_tiles, D_FF // tf),
        in_specs=[
            pl.BlockSpec((tm, D_MODEL), lambda i, j: (i, 0)),
            pl.BlockSpec((None, 6, D_MODEL), lambda i, j: (geo.mod_row(i), 0, 0)),
            pl.BlockSpec((1, D_MODEL), lambda i, j: (0, 0)),
            pl.BlockSpec((D_MODEL, tf), lambda i, j: (0, j)),
            pl.BlockSpec((D_MODEL, tf), lambda i, j: (0, j)),
        ],
        out_specs=pl.BlockSpec((tm, tf), lambda i, j: (i, j)),
        out_shape=jax.ShapeDtypeStruct((geo.rows, D_FF), BF16),
        scratch_shapes=[pltpu.VMEM((tm, D_MODEL), BF16)],
        compiler_params=_params(2),
        name="ffn_up",
    )(xs, mod_l, nw, w_gate, w_up)


def _moe_kernel(x_ref, mod_ref, nw_ref, router_ref, wg_ref, wu_ref, wd_ref, o_ref, h_ref, comb_ref):
    e, j = pl.program_id(1), pl.program_id(2)
    tm = x_ref.shape[0]
    lane = lax.broadcasted_iota(jnp.int32, (tm, LANES), 1)

    @pl.when(jnp.logical_and(e == 0, j == 0))
    def _():
        h = _norm_mod(x_ref[...], nw_ref[...], mod_ref[3:4, :], mod_ref[4:5, :])
        h_hi, h_lo = _split2(h)
        h_ref[...] = h_hi
        r_hi, r_lo = _split2(router_ref[...])
        logits = _dot(h_hi, r_hi) + _dot(h_hi, r_lo) + _dot(h_lo, r_hi)
        neg = -jnp.inf
        lg = jnp.where(lane < N_EXPERTS, logits, neg)
        m1 = jnp.max(lg, axis=-1, keepdims=True)
        i1 = jnp.min(jnp.where(lg == m1, lane, LANES), axis=-1, keepdims=True)
        lg2 = jnp.where(lane == i1, neg, lg)
        m2 = jnp.max(lg2, axis=-1, keepdims=True)
        i2 = jnp.min(jnp.where(lg2 == m2, lane, LANES), axis=-1, keepdims=True)
        e2 = jnp.exp(m2 - m1)
        w1 = 1.0 / (1.0 + e2)
        w2 = e2 / (1.0 + e2)
        comb_ref[...] = jnp.where(lane == i1, w1, 0.0) + jnp.where(lane == i2, w2, 0.0)
        o_ref[...] = jnp.zeros_like(o_ref)

    h = h_ref[...]
    ce = jnp.sum(jnp.where(lane == e, comb_ref[...], 0.0), axis=-1, keepdims=True)
    hid = (_silu(_dot(h, wg_ref[...])) * _dot(h, wu_ref[...]) * ce).astype(BF16)
    for n in range(0, D_MODEL, 512):
        o_ref[:, n:n + 512] += _dot(hid, wd_ref[:, n:n + 512])

    @pl.when(jnp.logical_and(e == pl.num_programs(1) - 1, j == pl.num_programs(2) - 1))
    def _():
        o_ref[...] = x_ref[...] + mod_ref[5:6, :] * o_ref[...]


def _moe(geo, xs, mod_l, nw, router, w_gate, w_up, w_down, tile0, n_tiles):
    sub = 2 if geo.tm >= 1024 else 1
    tm, tf = geo.tm // sub, 512
    tile0, n_tiles = tile0 * sub, n_tiles * sub
    return pl.pallas_call(
        _moe_kernel,
        grid=(n_tiles, N_EXPERTS, MOE_D_FF // tf),
        in_specs=[
            pl.BlockSpec((tm, D_MODEL), lambda i, e, j: (tile0 + i, 0)),
            pl.BlockSpec((None, 6, D_MODEL), lambda i, e, j: (geo.mod_row((tile0 + i) // sub), 0, 0)),
            pl.BlockSpec((1, D_MODEL), lambda i, e, j: (0, 0)),
            pl.BlockSpec((D_MODEL, LANES), lambda i, e, j: (0, 0)),
            pl.BlockSpec((None, D_MODEL, tf), lambda i, e, j: (e, 0, j)),
            pl.BlockSpec((None, D_MODEL, tf), lambda i, e, j: (e, 0, j)),
            pl.BlockSpec((None, tf, D_MODEL), lambda i, e, j: (e, j, 0)),
        ],
        out_specs=pl.BlockSpec((tm, D_MODEL), lambda i, e, j: (i, 0)),
        out_shape=jax.ShapeDtypeStruct((n_tiles * tm, D_MODEL), F32),
        scratch_shapes=[pltpu.VMEM((tm, D_MODEL), BF16), pltpu.VMEM((tm, LANES), F32)],
        compiler_params=_params(3),
        name="moe_dense",
    )(xs, mod_l, nw, router, w_gate, w_up, w_down)


MOE_ROW_TILE = 512
ROUTE_W1, ROUTE_W2, ROUTE_E1, ROUTE_E2 = 8, 9, 10, 11
DMA_ROWS = 256


def _route_kernel(x_ref, mod_ref, nw_ref, router_ref, h_ref, route_ref, cnt_ref, tri_ref, carry_ref):
    i = pl.program_id(0)
    tm = x_ref.shape[0]
    lane = lax.broadcasted_iota(jnp.int32, (tm, LANES), 1)

    @pl.when(i == 0)
    def _():
        ii = lax.broadcasted_iota(jnp.int32, (tm, tm), 0)
        jj = lax.broadcasted_iota(jnp.int32, (tm, tm), 1)
        tri_ref[...] = (jj < ii).astype(BF16)
        carry_ref[...] = jnp.zeros_like(carry_ref)

    h = _norm_mod(x_ref[...], nw_ref[...], mod_ref[3:4, :], mod_ref[4:5, :])
    h_hi, h_lo = _split2(h)
    h_ref[...] = h_hi
    r_hi, r_lo = _split2(router_ref[...])
    logits = _dot(h_hi, r_hi) + _dot(h_hi, r_lo) + _dot(h_lo, r_hi)
    neg = -jnp.inf
    lg = jnp.where(lane < N_EXPERTS, logits, neg)
    m1 = jnp.max(lg, axis=-1, keepdims=True)
    i1 = jnp.min(jnp.where(lg == m1, lane, LANES), axis=-1, keepdims=True)
    lg2 = jnp.where(lane == i1, neg, lg)
    m2 = jnp.max(lg2, axis=-1, keepdims=True)
    i2 = jnp.min(jnp.where(lg2 == m2, lane, LANES), axis=-1, keepdims=True)
    e2 = jnp.exp(m2 - m1)
    w1 = 1.0 / (1.0 + e2)
    w2 = e2 / (1.0 + e2)
    chosen = jnp.logical_or(lane == i1, lane == i2)
    mask = jnp.where(chosen, 1.0, 0.0)
    pos = _dot(tri_ref[...], mask.astype(BF16)) + carry_ref[0:1, :]
    carry_ref[0:1, :] = carry_ref[0:1, :] + jnp.sum(mask, axis=0, keepdims=True)
    rec = jnp.where(lane < N_EXPERTS, pos, 0.0)
    rec = jnp.where(lane == ROUTE_W1, w1, rec)
    rec = jnp.where(lane == ROUTE_W2, w2, rec)
    rec = jnp.where(lane == ROUTE_E1, i1.astype(F32), rec)
    rec = jnp.where(lane == ROUTE_E2, i2.astype(F32), rec)
    route_ref[...] = rec
    cnt_ref[...] = carry_ref[...]


def _slots_kernel(route_ref, start_ref, o_ref):
    rec = route_ref[...]
    lane = lax.broadcasted_iota(jnp.int32, rec.shape, 1)
    slot = rec + start_ref[...]
    e1 = rec[:, ROUTE_E1:ROUTE_E1 + 1].astype(jnp.int32)
    e2 = rec[:, ROUTE_E2:ROUTE_E2 + 1].astype(jnp.int32)
    d1 = jnp.sum(jnp.where(lane == e1, slot, 0.0), axis=-1, keepdims=True)
    d2 = jnp.sum(jnp.where(lane == e2, slot, 0.0), axis=-1, keepdims=True)
    o_ref[...] = jnp.where(lane == 0, d1, jnp.where(lane == 1, d2, 0.0)).astype(jnp.int32)


def _row_copy(src_ref, dst_ref, src_row, dst_row, sem):
    return pltpu.make_async_copy(src_ref.at[src_row], dst_ref.at[dst_row], sem)


def _dispatch_kernel(d1_ref, d2_ref, h_ref, zeros_ref, o_ref, sem):
    del zeros_ref
    base = pl.program_id(0) * DMA_ROWS

    def issue(r, carry):
        _row_copy(h_ref, o_ref, base + r, d1_ref[r], sem).start()
        _row_copy(h_ref, o_ref, base + r, d2_ref[r], sem).start()
        return carry

    def drain(r, carry):
        _row_copy(h_ref, o_ref, 0, 0, sem).wait()
        _row_copy(h_ref, o_ref, 0, 0, sem).wait()
        return carry

    lax.fori_loop(0, DMA_ROWS, issue, 0)
    lax.fori_loop(0, DMA_ROWS, drain, 0)


def _collect_kernel(d1_ref, d2_ref, y_ref, o_ref, sem):
    base = pl.program_id(0) * DMA_ROWS

    def issue(r, carry):
        _row_copy(y_ref, o_ref.at[0], d1_ref[r], base + r, sem).start()
        _row_copy(y_ref, o_ref.at[1], d2_ref[r], base + r, sem).start()
        return carry

    def drain(r, carry):
        _row_copy(y_ref, o_ref.at[0], 0, 0, sem).wait()
        _row_copy(y_ref, o_ref.at[0], 0, 0, sem).wait()
        return carry

    lax.fori_loop(0, DMA_ROWS, issue, 0)
    lax.fori_loop(0, DMA_ROWS, drain, 0)


def _expert_up_kernel(te_ref, nu_ref, x_ref, wg_ref, wu_ref, o_ref):
    del te_ref
    live = pl.program_id(1) < nu_ref[0]

    @pl.when(live)
    def _():
        x = x_ref[...]
        o_ref[...] = (_silu(_dot(x, wg_ref[...])) * _dot(x, wu_ref[...])).astype(o_ref.dtype)

    @pl.when(jnp.logical_not(live))
    def _():
        o_ref[...] = jnp.zeros_like(o_ref)


def _expert_down_kernel(te_ref, nu_ref, h_ref, wd_ref, o_ref):
    del te_ref
    live = pl.program_id(1) < nu_ref[0]

    @pl.when(live)
    def _():
        o_ref[...] = _dot(h_ref[...], wd_ref[...]).astype(o_ref.dtype)

    @pl.when(jnp.logical_not(live))
    def _():
        o_ref[...] = jnp.zeros_like(o_ref)


def _combine_kernel(x_ref, y1_ref, y2_ref, route_ref, mod_ref, o_ref):
    rec = route_ref[...]
    w1 = rec[:, ROUTE_W1:ROUTE_W1 + 1]
    w2 = rec[:, ROUTE_W2:ROUTE_W2 + 1]
    y = w1 * y1_ref[...].astype(F32) + w2 * y2_ref[...].astype(F32)
    o_ref[...] = x_ref[...] + mod_ref[5:6, :] * y


def _moe_sparse(geo, xs, mod_l, nw, router, w_gate, w_up, w_down, tile0, n_tiles):
    tm, tg = geo.tm, MOE_ROW_TILE
    rows = n_tiles * tm
    n_slots = 2 * rows + N_EXPERTS * tg
    n_gt = n_slots // tg
    row_tile = pl.BlockSpec((tm, D_MODEL), lambda i: (i, 0))
    rec_tile = pl.BlockSpec((tm, LANES), lambda i: (i, 0))

    h2, route, counts = pl.pallas_call(
        _route_kernel,
        grid=(n_tiles,),
        in_specs=[
            pl.BlockSpec((tm, D_MODEL), lambda i: (tile0 + i, 0)),
            pl.BlockSpec((None, 6, D_MODEL), lambda i: (geo.mod_row(tile0 + i), 0, 0)),
            pl.BlockSpec((1, D_MODEL), lambda i: (0, 0)),
            pl.BlockSpec((D_MODEL, LANES), lambda i: (0, 0)),
        ],
        out_specs=[row_tile, rec_tile, pl.BlockSpec((8, LANES), lambda i: (0, 0))],
        out_shape=[
            jax.ShapeDtypeStruct((rows, D_MODEL), BF16),
            jax.ShapeDtypeStruct((rows, LANES), F32),
            jax.ShapeDtypeStruct((8, LANES), F32),
        ],
        scratch_shapes=[pltpu.VMEM((tm, tm), BF16), pltpu.VMEM((8, LANES), F32)],
        compiler_params=_params(1),
        name="moe_route",
    )(xs, mod_l, nw, router)

    cnt = counts[0, :N_EXPERTS].astype(jnp.int32)
    padded = ((cnt + tg - 1) // tg) * tg
    ends = jnp.cumsum(padded)
    starts = ends - padded
    n_used = (ends[-1] // tg).astype(jnp.int32).reshape(1)
    tile_expert = jnp.searchsorted(ends, jnp.minimum(jnp.arange(n_gt), n_used[0] - 1) * tg, side="right")
    tile_expert = jnp.minimum(tile_expert, N_EXPERTS - 1).astype(jnp.int32)
    start_row = jnp.zeros((1, LANES), F32).at[0, :N_EXPERTS].set(starts.astype(F32))

    dest = pl.pallas_call(
        _slots_kernel,
        grid=(n_tiles,),
        in_specs=[rec_tile, pl.BlockSpec((1, LANES), lambda i: (0, 0))],
        out_specs=rec_tile,
        out_shape=jax.ShapeDtypeStruct((rows, LANES), jnp.int32),
        compiler_params=_params(1),
        name="moe_slots",
    )(route, start_row)
    d1, d2 = dest[:, 0], dest[:, 1]

    idx_spec = pl.BlockSpec((DMA_ROWS,), lambda i: (i,), memory_space=pltpu.SMEM)
    any_spec = pl.BlockSpec(memory_space=pl.ANY)
    xg = pl.pallas_call(
        _dispatch_kernel,
        grid=(rows // DMA_ROWS,),
        in_specs=[idx_spec, idx_spec, any_spec, any_spec],
        out_specs=any_spec,
        out_shape=jax.ShapeDtypeStruct((n_slots, 1, D_MODEL), BF16),
        scratch_shapes=[pltpu.SemaphoreType.DMA(())],
        input_output_aliases={3: 0},
        compiler_params=_params(1),
        name="moe_dispatch",
    )(d1, d2, h2.reshape(rows, 1, D_MODEL), jnp.zeros((n_slots, 1, D_MODEL), BF16))
    xg = xg.reshape(n_slots, D_MODEL)

    tf = 1024
    hg = pl.pallas_call(
        _expert_up_kernel,
        grid_spec=pltpu.PrefetchScalarGridSpec(
            num_scalar_prefetch=2,
            grid=(MOE_D_FF // tf, n_gt),
            in_specs=[
                pl.BlockSpec((tg, D_MODEL), lambda j, r, te, nu: (r, 0)),
                pl.BlockSpec((None, D_MODEL, tf), lambda j, r, te, nu: (te[r], 0, j)),
                pl.BlockSpec((None, D_MODEL, tf), lambda j, r, te, nu: (te[r], 0, j)),
            ],
            out_specs=pl.BlockSpec((tg, tf), lambda j, r, te, nu: (r, j)),
        ),
        out_shape=jax.ShapeDtypeStruct((n_slots, MOE_D_FF), BF16),
        compiler_params=_params(2),
        name="moe_expert_up",
    )(tile_expert, n_used, xg, w_gate, w_up)

    tn = 1024
    yg = pl.pallas_call(
        _expert_down_kernel,
        grid_spec=pltpu.PrefetchScalarGridSpec(
            num_scalar_prefetch=2,
            grid=(D_MODEL // tn, n_gt),
            in_specs=[
                pl.BlockSpec((tg, MOE_D_FF), lambda j, r, te, nu: (r, 0)),
                pl.BlockSpec((None, MOE_D_FF, tn), lambda j, r, te, nu: (te[r], 0, j)),
            ],
            out_specs=pl.BlockSpec((tg, tn), lambda j, r, te, nu: (r, j)),
        ),
        out_shape=jax.ShapeDtypeStruct((n_slots, D_MODEL), BF16),
        compiler_params=_params(2),
        name="moe_expert_down",
    )(tile_expert, n_used, hg, w_down)

    y12 = pl.pallas_call(
        _collect_kernel,
        grid=(rows // DMA_ROWS,),
        in_specs=[idx_spec, idx_spec, any_spec],
        out_specs=any_spec,
        out_shape=jax.ShapeDtypeStruct((2, rows, 1, D_MODEL), BF16),
        scratch_shapes=[pltpu.SemaphoreType.DMA(())],
        compiler_params=_params(1),
        name="moe_collect",
    )(d1, d2, yg.reshape(n_slots, 1, D_MODEL))
    y12 = y12.reshape(2, rows, D_MODEL)

    sub = 2 if tm >= 1024 else 1
    tc = tm // sub
    return pl.pallas_call(
        _combine_kernel,
        grid=(n_tiles * sub,),
        in_specs=[
            pl.BlockSpec((tc, D_MODEL), lambda i: (tile0 * sub + i, 0)),
            pl.BlockSpec((None, tc, D_MODEL), lambda i: (0, i, 0)),
            pl.BlockSpec((None, tc, D_MODEL), lambda i: (1, i, 0)),
            pl.BlockSpec((tc, LANES), lambda i: (i, 0)),
            pl.BlockSpec((None, 6, D_MODEL), lambda i: (geo.mod_row(tile0 + i // sub), 0, 0)),
        ],
        out_specs=pl.BlockSpec((tc, D_MODEL), lambda i: (i, 0)),
        out_shape=jax.ShapeDtypeStruct((rows, D_MODEL), F32),
        compiler_params=_params(1),
        name="moe_combine",
    )(xs, y12, y12, route, mod_l)


def _final_norm_kernel(x_ref, w_ref, o_ref):
    x = x_ref[...]
    o_ref[...] = x * lax.rsqrt(jnp.mean(x * x, axis=-1, keepdims=True) + EPS) * w_ref[...]


def _final_norm(x, w, tm):
    rows = x.shape[0]
    return pl.pallas_call(
        _final_norm_kernel,
        grid=(rows // tm,),
        in_specs=[pl.BlockSpec((tm, D_MODEL), lambda i: (i, 0)), pl.BlockSpec((1, D_MODEL), lambda i: (0, 0))],
        out_specs=pl.BlockSpec((tm, D_MODEL), lambda i: (i, 0)),
        out_shape=jax.ShapeDtypeStruct((rows, D_MODEL), F32),
        compiler_params=_params(1),
        name="final_norm",
    )(x, w)


def _rope_tables(geo):
    half = RET_DK // 4
    inv = ROPE_BASE ** (-jnp.arange(half, dtype=F32) / half)
    pos = jnp.arange(geo.seq)
    ang_r = (pos // GRID_W).astype(F32)[:, None] * inv[None, :]
    ang_c = (pos % GRID_W).astype(F32)[:, None] * inv[None, :]
    cos = jnp.concatenate([jnp.cos(ang_r), jnp.cos(ang_r), jnp.cos(ang_c), jnp.cos(ang_c)], axis=1)
    sin = jnp.concatenate([-jnp.sin(ang_r), jnp.sin(ang_r), -jnp.sin(ang_c), jnp.sin(ang_c)], axis=1)
    cos = jnp.concatenate([jnp.ones((geo.tm, LANES), F32), cos], axis=0)
    sin = jnp.concatenate([jnp.zeros((geo.tm, LANES), F32), sin], axis=0)
    return cos, sin


_DT_PERM = np.array([d * SSM_HEADS + g * SSM_HPG + h
                     for g in range(SSM_GROUPS) for d in range(2) for h in range(SSM_HPG)])


def kernel(x, c, ctx, c_ctx, w_ada, b_ada, norm1_w, norm2_w, w_in, conv_w, conv_b, ret_decay_f, ret_decay_b, ret_gn_w, ssm_a_log_f, ssm_a_log_b, ssm_dt_bias_f, ssm_dt_bias_b, ssm_d, ssm_norm_w, w_ret_proj, w_ssm_proj, w_out, ffn_w_gate, ffn_w_up, ffn_w_down, moe_router, moe_w_gate, moe_w_up, moe_w_down, final_norm_w):
    batch, seq, d = x.shape
    ctx_len = ctx.shape[1]
    depth = w_ada.shape[0]
    assert d == D_MODEL and seq % GRID_W == 0
    geo = _Geom(batch, ctx_len, seq)
    tm = geo.tm

    mod_rows = -(-(batch + 1) // 8) * 8
    cvec = jnp.zeros((mod_rows, d), F32).at[0].set(c_ctx).at[1:batch + 1].set(c)
    mod = _modulation(cvec, w_ada, b_ada).reshape(depth, mod_rows, 6, d)
    cos_t, sin_t = _rope_tables(geo)

    xs = jnp.concatenate([ctx.reshape(batch * ctx_len, d), x.reshape(batch * seq, d)], axis=0)
    lat_tile0 = geo.n_ctx_tiles
    n_lat_tiles = geo.n_tiles - geo.n_ctx_tiles

    for i in range(depth):
        mod_l = mod[i]
        w_main = jnp.concatenate([w_in[i][:, ORIG_Z_LO:ORIG_Z_HI], w_in[i][:, :ORIG_Z_LO],
                                  w_in[i][:, ORIG_Z_HI:ORIG_DT_LO], w_in[i][:, ORIG_DT_HI:]], axis=1).astype(BF16)
        w_dt = w_in[i][:, ORIG_DT_LO:ORIG_DT_HI][:, _DT_PERM].astype(BF16)
        proj, dt = _inproj(geo, xs, mod_l, norm1_w[i][None, :], w_main, w_dt, cos_t, sin_t)

        xbc = _conv_silu(geo, proj, conv_w[i], conv_b[i][None, :])
        bias = jnp.concatenate([ssm_dt_bias_f[i], ssm_dt_bias_b[i]])[_DT_PERM][None, :]
        alog = jnp.concatenate([ssm_a_log_f[i], ssm_a_log_b[i]])[_DT_PERM][None, :]
        cc, cp, cpt, dtt = _decay_tables(geo, dt, bias, alog)
        dskip = jnp.repeat(ssm_d[i], SSM_INNER // SSM_HEADS)[None, :]
        ys = _ssd_scan2(geo, proj, xbc, cc, cp, cpt, dtt, dskip, ssm_norm_w[i][None, :])
        yr = _ret_scan2(geo, proj, jnp.stack([ret_decay_f[i], ret_decay_b[i]]), ret_gn_w[i][None, :])

        merged = _merge(geo, yr, ys, w_ret_proj[i].astype(BF16), w_ssm_proj[i].astype(BF16), proj)
        xs = _matmul_residual(geo, merged, w_out[i].astype(BF16), xs, mod_l, 2, 0, geo.n_tiles, 1024)

        j = i // 2
        tile0, n_tiles = (0, geo.n_tiles) if i < depth - 1 else (lat_tile0, n_lat_tiles)
        if i % 2 == 0:
            hid = _ffn_up(geo, xs, mod_l, norm2_w[i][None, :], ffn_w_gate[j].astype(BF16), ffn_w_up[j].astype(BF16))
            hid = hid[tile0 * tm:]
            xs = _matmul_residual(geo, hid, ffn_w_down[j].astype(BF16), xs, mod_l, 5, tile0, n_tiles, 512)
        else:
            router = jnp.zeros((d, LANES), F32).at[:, :N_EXPERTS].set(moe_router[j])
            xs = _moe_sparse(geo, xs, mod_l, norm2_w[i][None, :], router, moe_w_gate[j].astype(BF16),
                             moe_w_up[j].astype(BF16), moe_w_down[j].astype(BF16), tile0, n_tiles)

    lat = xs if xs.shape[0] == batch * seq else xs[batch * ctx_len:]
    return _final_norm(lat, final_norm_w[None, :], tm).reshape(batch, seq, d)
```

```python
import functools

import numpy as np
import jax
import jax.numpy as jnp
from jax import lax
from jax.experimental import pallas as pl
from jax.experimental.pallas import tpu as pltpu

F32 = jnp.float32
BF16 = jnp.bfloat16

D_MODEL = 2048
GRID_W = 64
CHUNK = 128
EPS = 1e-6
ROPE_BASE = 10000.0
RET_HEADS = 8
RET_DK = 128
RET_DV = 256
SSM_INNER = 4096
SSM_HEADS = 64
SSM_GROUPS = 8
SSM_HPG = 8
SSM_GROUP_W = SSM_INNER // SSM_GROUPS
SSM_STATE = 128
XBC_W = 6144
CONV_K = 5
D_FF = 5632
N_EXPERTS = 8
MOE_D_FF = 4096

COL_Z, COL_Q, COL_K, COL_V, COL_G, COL_XBC, COL_GATES = 0, 4096, 5120, 6144, 8192, 10240, 16384
PROJ_W = 20480
ORIG_Z_LO, ORIG_Z_HI, ORIG_DT_LO, ORIG_DT_HI = 6144, 10240, 16384, 16512
SCAN_ROWS = 256

LANES = 128
VMEM_LIMIT_BYTES = 56 * 1024 * 1024


def _params(n_axes, vmem=VMEM_LIMIT_BYTES):
    return pltpu.CompilerParams(dimension_semantics=("arbitrary",) * n_axes, vmem_limit_bytes=vmem)


def _silu(x):
    return x * jax.nn.sigmoid(x)


def _split2(x):
    hi = x.astype(BF16)
    lo = (x - hi.astype(F32)).astype(BF16)
    return hi, lo


def _split3(x):
    hi = x.astype(BF16)
    r = x - hi.astype(F32)
    mid = r.astype(BF16)
    lo = (r - mid.astype(F32)).astype(BF16)
    return hi, mid, lo


def _dot(a, b):
    return jnp.dot(a, b, preferred_element_type=F32)


def _dot_nt(a, b):
    return lax.dot_general(a, b, (((1,), (1,)), ((), ())), preferred_element_type=F32)


def _dot_tn(a, b):
    return lax.dot_general(a, b, (((0,), (0,)), ((), ())), preferred_element_type=F32)


def _norm_mod(x, nw, shift, scale):
    y = x * lax.rsqrt(jnp.mean(x * x, axis=-1, keepdims=True) + EPS)
    return (y * nw) * (1.0 + scale) + shift


def _mod_kernel(c_ref, w_ref, b_ref, o_ref):
    s_hi, s_lo = _split2(_silu(c_ref[...]))
    w_hi, w_lo = _split2(w_ref[...])
    o_ref[...] = _dot(s_hi, w_hi) + _dot(s_hi, w_lo) + _dot(s_lo, w_hi) + b_ref[...]


def _modulation(cvec, w_ada, b_ada):
    depth, d, w6 = w_ada.shape
    rows = cvec.shape[0]
    tn = 512
    return pl.pallas_call(
        _mod_kernel,
        grid=(depth, w6 // tn),
        in_specs=[
            pl.BlockSpec((rows, d), lambda l, j: (0, 0)),
            pl.BlockSpec((None, d, tn), lambda l, j: (l, 0, j)),
            pl.BlockSpec((None, 1, tn), lambda l, j: (l, 0, j)),
        ],
        out_specs=pl.BlockSpec((None, rows, tn), lambda l, j: (l, 0, j)),
        out_shape=jax.ShapeDtypeStruct((depth, rows, w6), F32),
        compiler_params=_params(2),
        name="adaln_mod",
    )(cvec, w_ada, b_ada.reshape(depth, 1, w6))


class _Geom:
    def __init__(self, batch, ctx_len, seq):
        self.batch, self.ctx_len, self.seq = batch, ctx_len, seq
        tm = 1024
        while (batch * ctx_len) % tm or seq % tm:
            tm //= 2
        assert tm >= CHUNK and ctx_len % CHUNK == 0 and seq % CHUNK == 0
        self.tm = tm
        self.ctx_rows = batch * ctx_len
        self.rows = self.ctx_rows + batch * seq
        self.n_ctx_tiles = self.ctx_rows // tm
        self.tiles_per_batch = seq // tm
        self.n_tiles = self.rows // tm
        self.step_rows = min(SCAN_ROWS, ctx_len)
        assert ctx_len % self.step_rows == 0 and seq % self.step_rows == 0 and self.step_rows % CHUNK == 0
        self.ns_ctx = ctx_len // self.step_rows
        self.ns_lat = seq // self.step_rows
        self.ns = self.ns_ctx + self.ns_lat

    def mod_row(self, i):
        return jnp.where(i < self.n_ctx_tiles, 0, 1 + (i - self.n_ctx_tiles) // self.tiles_per_batch)

    def rope_block(self, i):
        return jnp.where(i < self.n_ctx_tiles, 0, 1 + (i - self.n_ctx_tiles) % self.tiles_per_batch)

    def step_block(self, b, c):
        return jnp.where(c < self.ns_ctx, b * self.ns_ctx + c,
                         self.batch * self.ns_ctx + b * self.ns_lat + (c - self.ns_ctx))

    def backward_order(self, s):
        return jnp.where(s < self.ns_ctx, self.ns_ctx - 1 - s, self.ns - 1 - (s - self.ns_ctx))


def _rope_store(acc, cos, sin, o_ref, scale):
    lane = lax.broadcasted_iota(jnp.int32, (acc.shape[0], LANES), 1)
    first_half = (lane % 64) < 32
    for h in range(acc.shape[1] // LANES):
        xh = acc[:, h * LANES:(h + 1) * LANES]
        partner = jnp.where(first_half, pltpu.roll(xh, 96, 1), pltpu.roll(xh, 32, 1))
        o_ref[:, h * LANES:(h + 1) * LANES] = ((xh * cos + partner * sin) * scale).astype(o_ref.dtype)


def _inproj_kernel(x_ref, mod_ref, nw_ref, w_ref, wdt_ref, cos_ref, sin_ref, o_ref, dt_ref, h_ref):
    j = pl.program_id(1)

    @pl.when(j == 0)
    def _():
        h = _norm_mod(x_ref[...], nw_ref[...], mod_ref[0:1, :], mod_ref[1:2, :]).astype(BF16)
        h_ref[...] = h
        dt_ref[...] = _dot(h, wdt_ref[...])

    acc = _dot(h_ref[...], w_ref[...])

    jq, jk = COL_Q // acc.shape[1], COL_K // acc.shape[1]

    @pl.when(j == jq)
    def _():
        _rope_store(acc, cos_ref[...], sin_ref[...], o_ref, 1.0)

    @pl.when(j == jk)
    def _():
        _rope_store(acc, cos_ref[...], sin_ref[...], o_ref, RET_DK ** -0.5)

    @pl.when(jnp.logical_and(j != jq, j != jk))
    def _():
        o_ref[...] = acc.astype(o_ref.dtype)


def _inproj(geo, xs, mod_l, nw, w_main, w_dt, cos_t, sin_t):
    tm, tn = geo.tm, 1024
    return pl.pallas_call(
        _inproj_kernel,
        grid=(geo.n_tiles, PROJ_W // tn),
        in_specs=[
            pl.BlockSpec((tm, D_MODEL), lambda i, j: (i, 0)),
            pl.BlockSpec((None, 6, D_MODEL), lambda i, j: (geo.mod_row(i), 0, 0)),
            pl.BlockSpec((1, D_MODEL), lambda i, j: (0, 0)),
            pl.BlockSpec((D_MODEL, tn), lambda i, j: (0, j)),
            pl.BlockSpec((D_MODEL, LANES), lambda i, j: (0, 0)),
            pl.BlockSpec((tm, LANES), lambda i, j: (geo.rope_block(i), 0)),
            pl.BlockSpec((tm, LANES), lambda i, j: (geo.rope_block(i), 0)),
        ],
        out_specs=[
            pl.BlockSpec((tm, tn), lambda i, j: (i, j)),
            pl.BlockSpec((tm, LANES), lambda i, j: (i, 0)),
        ],
        out_shape=[
            jax.ShapeDtypeStruct((geo.rows, PROJ_W), BF16),
            jax.ShapeDtypeStruct((geo.rows, LANES), F32),
        ],
        scratch_shapes=[pltpu.VMEM((tm, D_MODEL), BF16)],
        compiler_params=_params(2),
        name="in_proj",
    )(xs, mod_l, nw, w_main, w_dt, cos_t, sin_t)


CONV_HALO = 16


def _conv_kernel(prev_ref, main_ref, next_ref, w_ref, b_ref, o_ref, ext_ref, *, geo, rows):
    i = pl.program_id(0)
    start = i * rows
    in_ctx = start < geo.ctx_rows
    seq_len = jnp.where(in_ctx, geo.ctx_len, geo.seq)
    off = jnp.where(in_ctx, start, start - geo.ctx_rows)
    first = (off % seq_len) == 0
    last = ((off + rows) % seq_len) == 0
    ext_ref[0:CONV_HALO, :] = jnp.where(first, 0.0, prev_ref[...].astype(F32))
    ext_ref[CONV_HALO:CONV_HALO + rows, :] = main_ref[...].astype(F32)
    ext_ref[CONV_HALO + rows:, :] = jnp.where(last, 0.0, next_ref[...].astype(F32))
    acc = jnp.broadcast_to(b_ref[...], (rows, b_ref.shape[1]))
    for k in range(CONV_K):
        acc = acc + ext_ref[pl.ds(CONV_HALO + k - CONV_K // 2, rows), :] * w_ref[k:k + 1, :]
    o_ref[...] = _silu(acc).astype(o_ref.dtype)


def _conv_silu(geo, proj, conv_w, conv_b):
    rows = min(256, geo.ctx_len)
    tc = 1024
    col0 = COL_XBC // tc
    n_halo = geo.rows // CONV_HALO
    per = rows // CONV_HALO
    return pl.pallas_call(
        functools.partial(_conv_kernel, geo=geo, rows=rows),
        grid=(geo.rows // rows, XBC_W // tc),
        in_specs=[
            pl.BlockSpec((CONV_HALO, tc), lambda i, j: (jnp.maximum(i * per - 1, 0), col0 + j)),
            pl.BlockSpec((rows, tc), lambda i, j: (i, col0 + j)),
            pl.BlockSpec((CONV_HALO, tc), lambda i, j: (jnp.minimum((i + 1) * per, n_halo - 1), col0 + j)),
            pl.BlockSpec((CONV_K, tc), lambda i, j: (0, j)),
            pl.BlockSpec((1, tc), lambda i, j: (0, j)),
        ],
        out_specs=pl.BlockSpec((rows, tc), lambda i, j: (i, j)),
        out_shape=jax.ShapeDtypeStruct((geo.rows, XBC_W), BF16),
        scratch_shapes=[pltpu.VMEM((rows + 2 * CONV_HALO, tc), F32)],
        compiler_params=_params(2),
        name="ssm_conv",
    )(proj, proj, proj, conv_w, conv_b)


def _decay_kernel(dt_ref, bias_ref, alog_ref, cc_ref, cp_ref, cpt_ref, dtt_ref):
    x = dt_ref[...] + bias_ref[...]
    dtv = jnp.maximum(x, 0.0) + jnp.log1p(jnp.exp(-jnp.abs(x)))
    la = dtv * (-jnp.exp(alog_ref[...]))
    ii = lax.broadcasted_iota(jnp.int32, (CHUNK, CHUNK), 0)
    jj = lax.broadcasted_iota(jnp.int32, (CHUNK, CHUNK), 1)
    lower = (jj <= ii).astype(BF16)
    upper = (jj >= ii).astype(BF16)
    hi, mid, lo = _split3(la)
    cum_f = _dot(lower, hi) + _dot(lower, mid) + _dot(lower, lo)
    cum_b = _dot(upper, hi) + _dot(upper, mid) + _dot(upper, lo)
    lane = lax.broadcasted_iota(jnp.int32, (CHUNK, LANES), 1)
    cc = jnp.where((lane % 16) < 8, cum_f, cum_b)
    cp = cc - jnp.log(dtv)
    cc_ref[...] = cc
    cp_ref[...] = cp
    cpt_ref[...] = cp.T
    dtt_ref[...] = dtv.T


def _decay_tables(geo, dt, bias, alog):
    n_chunks = geo.rows // CHUNK
    col = pl.BlockSpec((CHUNK, LANES), lambda i: (i, 0))
    row = pl.BlockSpec((None, LANES, CHUNK), lambda i: (i, 0, 0))
    vec = pl.BlockSpec((1, LANES), lambda i: (0, 0))
    col_shape = jax.ShapeDtypeStruct((geo.rows, LANES), F32)
    row_shape = jax.ShapeDtypeStruct((n_chunks, LANES, CHUNK), F32)
    return pl.pallas_call(
        _decay_kernel,
        grid=(n_chunks,),
        in_specs=[col, vec, vec],
        out_specs=[col, col, row, row],
        out_shape=[col_shape, col_shape, row_shape, row_shape],
        compiler_params=_params(1),
        name="ssm_decay",
    )(dt, bias, alog)


def _ssd_kernel(q_ref, k_ref, xs_ref, z_ref, cc_ref, cp_ref, cct_ref, cpt_ref, dtt_ref, dskip_ref,
                nw_ref, o_ref, sf_ref, sb_ref, hist_ref, *, geo):
    g, ph, s = pl.program_id(1), pl.program_id(2), pl.program_id(3)
    c = jnp.where(ph == 0, geo.backward_order(s), s)
    shift = (LANES - 16 * g) % LANES
    cc = pltpu.roll(cc_ref[...], shift, 1)
    cp = pltpu.roll(cp_ref[...], shift, 1)
    k = k_ref[...]
    xs = xs_ref[...]
    lane = lax.broadcasted_iota(jnp.int32, (CHUNK, LANES), 1)
    left = lane < 64
    left_row = lax.broadcasted_iota(jnp.int32, (1, LANES), 1) < 64
    n_pairs = SSM_HPG // 2

    def pair_cols(a, h1, h2):
        return jnp.where(left, a[:, h1:h1 + 1], a[:, h2:h2 + 1])

    @pl.when(jnp.logical_and(ph == 0, s == 0))
    def _():
        sb_ref[...] = jnp.zeros_like(sb_ref)

    @pl.when(jnp.logical_and(ph == 1, s == 0))
    def _():
        sf_ref[...] = jnp.zeros_like(sf_ref)

    @pl.when(ph == 0)
    def _():
        hist_ref[c] = sb_ref[...].astype(BF16)
        w_all = jnp.exp(cc[0:1, :] - cp)
        a_all = jnp.exp(cc[0:1, :])
        for p in range(n_pairs):
            h1, h2 = SSM_HPG + 2 * p, SSM_HPG + 2 * p + 1
            sl = slice(p * LANES, (p + 1) * LANES)
            wb = pair_cols(w_all, h1, h2)
            ab = jnp.where(left_row, a_all[:, h1:h1 + 1], a_all[:, h2:h2 + 1])
            vw = (xs[:, sl].astype(F32) * wb).astype(BF16)
            sb_ref[:, sl] = ab * sb_ref[:, sl] + _dot_tn(k, vw)

    @pl.when(ph == 1)
    def _():
        q = q_ref[...]
        cct = cct_ref[...]
        cpt = cpt_ref[...]
        dtt = dtt_ref[...]
        scores = _dot_nt(q, k)
        ii = lax.broadcasted_iota(jnp.int32, (CHUNK, CHUNK), 0)
        jj = lax.broadcasted_iota(jnp.int32, (CHUNK, CHUNK), 1)
        lower = jj <= ii
        diag = jj == ii
        yf = _dot(q, sf_ref[...].astype(BF16))
        yb = _dot(q, hist_ref[c])
        e_all = jnp.exp(cc)
        w_all = jnp.exp(cc[CHUNK - 1:CHUNK, :] - cp)
        a_all = jnp.exp(cc[CHUNK - 1:CHUNK, :])
        ys = []
        for p in range(n_pairs):
            sl = slice(p * LANES, (p + 1) * LANES)
            vp = xs[:, sl]
            probs = []
            for h in (2 * p, 2 * p + 1):
                hb = SSM_HPG + h
                arg = jnp.where(lower, cc[:, h:h + 1] - cpt[h:h + 1, :], cc[:, hb:hb + 1] - cpt[hb:hb + 1, :])
                e = jnp.exp(arg) + jnp.where(diag, dtt[hb:hb + 1, :], 0.0)
                probs.append((scores * e).astype(BF16))
            zero = jnp.zeros_like(vp)
            rhs = jnp.concatenate([jnp.where(left, vp, zero), jnp.where(left, zero, vp)], axis=0)
            y = _dot(jnp.concatenate(probs, axis=1), rhs)
            h1, h2 = 2 * p, 2 * p + 1
            y = y + pair_cols(e_all, h1, h2) * yf[:, sl] + pair_cols(e_all, SSM_HPG + h1, SSM_HPG + h2) * yb[:, sl]
            ys.append(y)
            wf = pair_cols(w_all, h1, h2)
            af = jnp.where(left_row, a_all[:, h1:h1 + 1], a_all[:, h2:h2 + 1])
            vw = (vp.astype(F32) * wf).astype(BF16)
            sf_ref[:, sl] = af * sf_ref[:, sl] + _dot_tn(k, vw)
        y = jnp.concatenate(ys, axis=1)
        y = y + dskip_ref[...] * xs.astype(F32)
        y = y * _silu(z_ref[...].astype(F32))
        y = y * lax.rsqrt(jnp.mean(y * y, axis=-1, keepdims=True) + EPS)
        o_ref[...] = (y * nw_ref[...]).astype(o_ref.dtype)


def _ssd_scan(geo, proj, xbc, cc, cp, cct, cpt, dtt, dskip, norm_w):
    gw = SSM_GROUP_W

    def blk(b, c):
        return geo.chunk_block(b, c)

    def cur(b, g, ph, s):
        return blk(b, jnp.where(ph == 0, geo.backward_order(s), s))

    def out_blk(b, g, ph, s):
        return blk(b, jnp.where(ph == 0, 0, s))

    row_spec = pl.BlockSpec((None, 16, CHUNK), lambda b, g, ph, s: (cur(b, g, ph, s), g, 0))
    col_spec = pl.BlockSpec((CHUNK, LANES), lambda b, g, ph, s: (cur(b, g, ph, s), 0))
    return pl.pallas_call(
        functools.partial(_ssd_kernel, geo=geo),
        grid=(geo.batch, SSM_GROUPS, 2, geo.nc),
        in_specs=[
            pl.BlockSpec((CHUNK, SSM_STATE), lambda b, g, ph, s: (cur(b, g, ph, s), (SSM_INNER + 1024) // SSM_STATE + g)),
            pl.BlockSpec((CHUNK, SSM_STATE), lambda b, g, ph, s: (cur(b, g, ph, s), SSM_INNER // SSM_STATE + g)),
            pl.BlockSpec((CHUNK, gw), lambda b, g, ph, s: (cur(b, g, ph, s), g)),
            pl.BlockSpec((CHUNK, gw), lambda b, g, ph, s: (cur(b, g, ph, s), COL_Z // gw + g)),
            col_spec, col_spec, row_spec, row_spec, row_spec,
            pl.BlockSpec((1, gw), lambda b, g, ph, s: (0, g)),
            pl.BlockSpec((1, gw), lambda b, g, ph, s: (0, g)),
        ],
        out_specs=pl.BlockSpec((CHUNK, gw), lambda b, g, ph, s: (out_blk(b, g, ph, s), g)),
        out_shape=jax.ShapeDtypeStruct((geo.rows, SSM_INNER), BF16),
        scratch_shapes=[
            pltpu.VMEM((SSM_STATE, gw), F32),
            pltpu.VMEM((SSM_STATE, gw), F32),
            pltpu.VMEM((geo.nc, SSM_STATE, gw), BF16),
        ],
        compiler_params=_params(4),
        name="ssd_scan",
    )(xbc, xbc, xbc, proj, cc, cp, cct, cpt, dtt, dskip, norm_w)


def _ret_kernel(dec_ref, q_ref, k_ref, v_ref, g_ref, gnw_ref, o_ref,
                sf_ref, sb_ref, hist_ref, m_ref, ey_ref, wk_ref, a_ref, *, geo):
    h, ph, s = pl.program_id(1), pl.program_id(2), pl.program_id(3)
    c = jnp.where(ph == 0, geo.backward_order(s), s)
    n = float(CHUNK)

    @pl.when(jnp.logical_and(ph == 0, s == 0))
    def _():
        lam_f = jnp.exp(jnp.full((CHUNK, 1), dec_ref[0, h], F32))
        lam_b = jnp.exp(jnp.full((CHUNK, 1), dec_ref[1, h], F32))
        ii = lax.broadcasted_iota(jnp.int32, (CHUNK, CHUNK), 0)
        jj = lax.broadcasted_iota(jnp.int32, (CHUNK, CHUNK), 1)
        dist = (ii - jj).astype(F32)
        m_ref[...] = (jnp.where(jj <= ii, jnp.exp(-lam_f * dist), 0.0)
                      + jnp.where(jj >= ii, jnp.exp(lam_b * dist), 0.0))
        row = lax.broadcasted_iota(jnp.int32, (CHUNK, 1), 0).astype(F32)
        ey_ref[0] = jnp.broadcast_to(jnp.exp(-lam_f * (row + 1.0)), (CHUNK, RET_DV))
        ey_ref[1] = jnp.broadcast_to(jnp.exp(-lam_b * (n - row)), (CHUNK, RET_DV))
        wk_ref[0] = jnp.broadcast_to(jnp.exp(-lam_f * (n - 1.0 - row)), (CHUNK, RET_DK))
        wk_ref[1] = jnp.broadcast_to(jnp.exp(-lam_b * row), (CHUNK, RET_DK))
        a_ref[0] = jnp.broadcast_to(jnp.exp(-lam_f[0:8, :] * n), (8, RET_DV))
        a_ref[1] = jnp.broadcast_to(jnp.exp(-lam_b[0:8, :] * n), (8, RET_DV))
        sb_ref[...] = jnp.zeros_like(sb_ref)

    @pl.when(jnp.logical_and(ph == 1, s == 0))
    def _():
        sf_ref[...] = jnp.zeros_like(sf_ref)

    k = k_ref[...]
    v = v_ref[...]

    @pl.when(ph == 0)
    def _():
        hist_ref[c] = sb_ref[...].astype(BF16)
        kw = (k.astype(F32) * wk_ref[1]).astype(BF16)
        sb_ref[...] = a_ref[1][0:1, :] * sb_ref[...] + _dot_tn(kw, v)

    @pl.when(ph == 1)
    def _():
        q = q_ref[...]
        probs = (_dot_nt(q, k) * m_ref[...]).astype(BF16)
        y = _dot(probs, v)
        y = y + ey_ref[0] * _dot(q, sf_ref[...].astype(BF16)) + ey_ref[1] * _dot(q, hist_ref[c])
        kw = (k.astype(F32) * wk_ref[0]).astype(BF16)
        sf_ref[...] = a_ref[0][0:1, :] * sf_ref[...] + _dot_tn(kw, v)
        mu = jnp.mean(y, axis=-1, keepdims=True)
        yc = y - mu
        var = jnp.mean(yc * yc, axis=-1, keepdims=True)
        yn = yc * lax.rsqrt(var + EPS)
        o_ref[...] = (yn * gnw_ref[...] * _silu(g_ref[...].astype(F32))).astype(o_ref.dtype)


def _ret_scan(geo, proj, decays, gn_w):
    def cur(b, h, ph, s):
        return geo.chunk_block(b, jnp.where(ph == 0, geo.backward_order(s), s))

    def out_blk(b, h, ph, s):
        return geo.chunk_block(b, jnp.where(ph == 0, 0, s))

    return pl.pallas_call(
        functools.partial(_ret_kernel, geo=geo),
        grid=(geo.batch, RET_HEADS, 2, geo.nc),
        in_specs=[
            pl.BlockSpec(memory_space=pltpu.SMEM),
            pl.BlockSpec((CHUNK, RET_DK), lambda b, h, ph, s: (cur(b, h, ph, s), COL_Q // RET_DK + h)),
            pl.BlockSpec((CHUNK, RET_DK), lambda b, h, ph, s: (cur(b, h, ph, s), COL_K // RET_DK + h)),
            pl.BlockSpec((CHUNK, RET_DV), lambda b, h, ph, s: (cur(b, h, ph, s), COL_V // RET_DV + h)),
            pl.BlockSpec((CHUNK, RET_DV), lambda b, h, ph, s: (cur(b, h, ph, s), COL_G // RET_DV + h)),
            pl.BlockSpec((1, RET_DV), lambda b, h, ph, s: (0, h)),
        ],
        out_specs=pl.BlockSpec((CHUNK, RET_DV), lambda b, h, ph, s: (out_blk(b, h, ph, s), h)),
        out_shape=jax.ShapeDtypeStruct((geo.rows, RET_HEADS * RET_DV), BF16),
        scratch_shapes=[
            pltpu.VMEM((RET_DK, RET_DV), F32),
            pltpu.VMEM((RET_DK, RET_DV), F32),
            pltpu.VMEM((geo.nc, RET_DK, RET_DV), BF16),
            pltpu.VMEM((CHUNK, CHUNK), F32),
            pltpu.VMEM((2, CHUNK, RET_DV), F32),
            pltpu.VMEM((2, CHUNK, RET_DK), F32),
            pltpu.VMEM((2, 8, RET_DV), F32),
        ],
        compiler_params=_params(4),
        name="ret_scan",
    )(decays, proj, proj, proj, proj, gn_w)


def _lanes2(a):
    return jnp.concatenate([a, a], axis=1)


def _ret_tables(dec_ref, direction, rows, wk_ref, a_ref, e_ref=None, m_ref=None):
    n = float(rows)
    row = lax.broadcasted_iota(jnp.int32, (rows, LANES), 0).astype(F32)
    for h in range(RET_HEADS):
        lam = jnp.exp(jnp.full((rows, LANES), dec_ref[direction, h], F32))
        if direction == 0:
            wk_ref[h] = jnp.exp(-lam * (n - 1.0 - row))
            if e_ref is not None:
                e_ref[0, h] = jnp.exp(-lam * (row + 1.0))
        else:
            wk_ref[h] = jnp.exp(-lam * row)
        a_ref[h] = _lanes2(jnp.exp(-lam[0:8, :] * n))


def _ret_state_kernel(dec_ref, k_ref, v_ref, hist_ref, sb_ref, wk_ref, a_ref, *, rows):
    @pl.when(pl.program_id(1) == 0)
    def _():
        sb_ref[...] = jnp.zeros_like(sb_ref)
        _ret_tables(dec_ref, 1, rows, wk_ref, a_ref)

    def body(h, carry):
        kh = k_ref[:, pl.ds(pl.multiple_of(h * RET_DK, RET_DK), RET_DK)]
        vh = v_ref[:, pl.ds(pl.multiple_of(h * RET_DV, RET_DV), RET_DV)]
        hist_ref[h] = sb_ref[h].astype(BF16)
        kw = (kh.astype(F32) * wk_ref[h]).astype(BF16)
        sb_ref[h] = a_ref[h][0:1, :] * sb_ref[h] + _dot_tn(kw, vh)
        return carry

    lax.fori_loop(0, RET_HEADS, body, 0)


def _ret_out_kernel(dec_ref, q_ref, k_ref, v_ref, g_ref, hist_ref, gnw_ref, o_ref,
                    sf_ref, wk_ref, a_ref, e_ref, m_ref, *, rows):
    @pl.when(pl.program_id(1) == 0)
    def _():
        sf_ref[...] = jnp.zeros_like(sf_ref)
        _ret_tables(dec_ref, 0, rows, wk_ref, a_ref, e_ref)
        n = float(rows)
        row = lax.broadcasted_iota(jnp.int32, (rows, LANES), 0).astype(F32)
        ii = lax.broadcasted_iota(jnp.int32, (rows, rows), 0)
        jj = lax.broadcasted_iota(jnp.int32, (rows, rows), 1)
        dist = (ii - jj).astype(F32)
        for h in range(RET_HEADS):
            lam_f = jnp.exp(jnp.full((rows, 1), dec_ref[0, h], F32))
            lam_b = jnp.exp(jnp.full((rows, 1), dec_ref[1, h], F32))
            m_ref[h] = (jnp.where(jj <= ii, jnp.exp(-lam_f * dist), 0.0)
                        + jnp.where(jj >= ii, jnp.exp(lam_b * dist), 0.0))
            e_ref[1, h] = jnp.exp(-jnp.exp(jnp.full((rows, LANES), dec_ref[1, h], F32)) * (n - row))

    def body(h, carry):
        ok = pl.multiple_of(h * RET_DK, RET_DK)
        ov = pl.multiple_of(h * RET_DV, RET_DV)
        qh = q_ref[:, pl.ds(ok, RET_DK)]
        kh = k_ref[:, pl.ds(ok, RET_DK)]
        vh = v_ref[:, pl.ds(ov, RET_DV)]
        probs = (_dot_nt(qh, kh) * m_ref[h]).astype(BF16)
        y = _dot(probs, vh)
        y = y + _lanes2(e_ref[0, h]) * _dot(qh, sf_ref[h].astype(BF16)) + _lanes2(e_ref[1, h]) * _dot(qh, hist_ref[h])
        kw = (kh.astype(F32) * wk_ref[h]).astype(BF16)
        sf_ref[h] = a_ref[h][0:1, :] * sf_ref[h] + _dot_tn(kw, vh)
        mu = jnp.mean(y, axis=-1, keepdims=True)
        yc = y - mu
        yn = yc * lax.rsqrt(jnp.mean(yc * yc, axis=-1, keepdims=True) + EPS)
        gate = _silu(g_ref[:, pl.ds(ov, RET_DV)].astype(F32))
        o_ref[:, pl.ds(ov, RET_DV)] = (yn * gnw_ref[:, pl.ds(ov, RET_DV)] * gate).astype(o_ref.dtype)
        return carry

    lax.fori_loop(0, RET_HEADS, body, 0)


def _ret_scan2(geo, proj, decays, gn_w):
    rows = geo.step_rows
    qk_w, v_w = RET_HEADS * RET_DK, RET_HEADS * RET_DV

    def bwd(b, s):
        return geo.step_block(b, geo.backward_order(s))

    def fwd(b, s):
        return geo.step_block(b, s)

    smem = pl.BlockSpec(memory_space=pltpu.SMEM)
    hist_shape = (geo.batch, geo.ns, RET_HEADS, RET_DK, RET_DV)
    hist = pl.pallas_call(
        functools.partial(_ret_state_kernel, rows=rows),
        grid=(geo.batch, geo.ns),
        in_specs=[
            smem,
            pl.BlockSpec((rows, qk_w), lambda b, s: (bwd(b, s), COL_K // qk_w)),
            pl.BlockSpec((rows, v_w), lambda b, s: (bwd(b, s), COL_V // v_w)),
        ],
        out_specs=pl.BlockSpec((None, None) + hist_shape[2:], lambda b, s: (b, geo.backward_order(s), 0, 0, 0)),
        out_shape=jax.ShapeDtypeStruct(hist_shape, BF16),
        scratch_shapes=[
            pltpu.VMEM((RET_HEADS, RET_DK, RET_DV), F32),
            pltpu.VMEM((RET_HEADS, rows, LANES), F32),
            pltpu.VMEM((RET_HEADS, 8, RET_DV), F32),
        ],
        compiler_params=_params(2),
        name="ret_state",
    )(decays, proj, proj)
    return pl.pallas_call(
        functools.partial(_ret_out_kernel, rows=rows),
        grid=(geo.batch, geo.ns),
        in_specs=[
            smem,
            pl.BlockSpec((rows, qk_w), lambda b, s: (fwd(b, s), COL_Q // qk_w)),
            pl.BlockSpec((rows, qk_w), lambda b, s: (fwd(b, s), COL_K // qk_w)),
            pl.BlockSpec((rows, v_w), lambda b, s: (fwd(b, s), COL_V // v_w)),
            pl.BlockSpec((rows, v_w), lambda b, s: (fwd(b, s), COL_G // v_w)),
            pl.BlockSpec((None, None) + hist_shape[2:], lambda b, s: (b, s, 0, 0, 0)),
            pl.BlockSpec((1, v_w), lambda b, s: (0, 0)),
        ],
        out_specs=pl.BlockSpec((rows, v_w), lambda b, s: (fwd(b, s), 0)),
        out_shape=jax.ShapeDtypeStruct((geo.rows, v_w), BF16),
        scratch_shapes=[
            pltpu.VMEM((RET_HEADS, RET_DK, RET_DV), F32),
            pltpu.VMEM((RET_HEADS, rows, LANES), F32),
            pltpu.VMEM((RET_HEADS, 8, RET_DV), F32),
            pltpu.VMEM((2, RET_HEADS, rows, LANES), F32),
            pltpu.VMEM((RET_HEADS, rows, rows), F32),
        ],
        compiler_params=_params(2),
        name="ret_out",
    )(decays, proj, proj, proj, proj, hist, gn_w)


def _group_cols(ref, rows, g):
    return pltpu.roll(ref[rows, :], (LANES - 16 * g) % LANES, 1)


def _pair_cols(left, a, h1, h2):
    return jnp.where(left, a[:, h1:h1 + 1], a[:, h2:h2 + 1])


def _ssd_state_kernel(k_ref, xs_ref, cc_ref, cp_ref, hist_ref, sb_ref, *, cps):
    @pl.when(pl.program_id(1) == 0)
    def _():
        sb_ref[...] = jnp.zeros_like(sb_ref)

    lane = lax.broadcasted_iota(jnp.int32, (CHUNK, LANES), 1)
    left = lane < 64
    left_row = lax.broadcasted_iota(jnp.int32, (1, LANES), 1) < 64

    def body(g, carry):
        ok = pl.multiple_of(g * SSM_STATE, SSM_STATE)
        ov = pl.multiple_of(g * SSM_GROUP_W, SSM_GROUP_W)
        for ci in reversed(range(cps)):
            rows = pl.ds(ci * CHUNK, CHUNK)
            cc = _group_cols(cc_ref, rows, g)
            cp = _group_cols(cp_ref, rows, g)
            k = k_ref[rows, pl.ds(ok, SSM_STATE)]
            hist_ref[ci, g] = sb_ref[g].astype(BF16)
            w_all = jnp.exp(cc[0:1, :] - cp)
            a_all = jnp.exp(cc[0:1, :])
            for p in range(SSM_HPG // 2):
                h1, h2 = SSM_HPG + 2 * p, SSM_HPG + 2 * p + 1
                sl = slice(p * LANES, (p + 1) * LANES)
                vp = xs_ref[rows, pl.ds(pl.multiple_of(ov + p * LANES, LANES), LANES)]
                vw = (vp.astype(F32) * _pair_cols(left, w_all, h1, h2)).astype(BF16)
                ab = jnp.where(left_row, a_all[:, h1:h1 + 1], a_all[:, h2:h2 + 1])
                sb_ref[g, :, sl] = ab * sb_ref[g, :, sl] + _dot_tn(k, vw)
        return carry

    lax.fori_loop(0, SSM_GROUPS, body, 0)


def _ssd_out_kernel(q_ref, k_ref, xs_ref, z_ref, cc_ref, cp_ref, cpt_ref, dtt_ref, hist_ref,
                    dskip_ref, nw_ref, o_ref, sf_ref, *, cps):
    @pl.when(pl.program_id(1) == 0)
    def _():
        sf_ref[...] = jnp.zeros_like(sf_ref)

    lane = lax.broadcasted_iota(jnp.int32, (CHUNK, LANES), 1)
    left = lane < 64
    left_row = lax.broadcasted_iota(jnp.int32, (1, LANES), 1) < 64
    ii = lax.broadcasted_iota(jnp.int32, (CHUNK, CHUNK), 0)
    jj = lax.broadcasted_iota(jnp.int32, (CHUNK, CHUNK), 1)
    lower = jj <= ii
    diag = jj == ii

    def body(g, carry):
        ok = pl.multiple_of(g * SSM_STATE, SSM_STATE)
        ov = pl.multiple_of(g * SSM_GROUP_W, SSM_GROUP_W)
        og = pl.multiple_of(g * 16, 16)
        for ci in range(cps):
            rows = pl.ds(ci * CHUNK, CHUNK)
            cc = _group_cols(cc_ref, rows, g)
            cp = _group_cols(cp_ref, rows, g)
            cpt = cpt_ref[ci, pl.ds(og, 16), :]
            dtt = dtt_ref[ci, pl.ds(og, 16), :]
            q = q_ref[rows, pl.ds(ok, SSM_STATE)]
            k = k_ref[rows, pl.ds(ok, SSM_STATE)]
            scores = _dot_nt(q, k)
            yf = _dot(q, sf_ref[g].astype(BF16))
            yb = _dot(q, hist_ref[ci, g])
            w_all = jnp.exp(cc[CHUNK - 1:CHUNK, :] - cp)
            a_all = jnp.exp(cc[CHUNK - 1:CHUNK, :])
            ys = []
            for p in range(SSM_HPG // 2):
                h1, h2 = 2 * p, 2 * p + 1
                sl = slice(p * LANES, (p + 1) * LANES)
                cols = pl.ds(pl.multiple_of(ov + p * LANES, LANES), LANES)
                vp = xs_ref[rows, cols]
                probs, ef, eb = [], [], []
                for h in (h1, h2):
                    hb = SSM_HPG + h
                    cf = jnp.broadcast_to(cc[:, h:h + 1], (CHUNK, CHUNK))
                    cb = jnp.broadcast_to(cc[:, hb:hb + 1], (CHUNK, CHUNK))
                    arg = jnp.where(lower, cf - cpt[h:h + 1, :], cb - cpt[hb:hb + 1, :])
                    e = jnp.exp(arg) + jnp.where(diag, dtt[hb:hb + 1, :], 0.0)
                    probs.append((scores * e).astype(BF16))
                    ef.append(jnp.exp(cf))
                    eb.append(jnp.exp(cb))
                zero = jnp.zeros_like(vp)
                rhs = jnp.concatenate([jnp.where(left, vp, zero), jnp.where(left, zero, vp)], axis=0)
                y = _dot(jnp.concatenate(probs, axis=1), rhs)
                y = (y + jnp.where(left, ef[0], ef[1]) * yf[:, sl] + jnp.where(left, eb[0], eb[1]) * yb[:, sl])
                vw = (vp.astype(F32) * _pair_cols(left, w_all, h1, h2)).astype(BF16)
                af = jnp.where(left_row, a_all[:, h1:h1 + 1], a_all[:, h2:h2 + 1])
                sf_ref[g, :, sl] = af * sf_ref[g, :, sl] + _dot_tn(k, vw)
                ys.append(y + dskip_ref[:, cols] * vp.astype(F32))
            gcols = pl.ds(ov, SSM_GROUP_W)
            y = jnp.concatenate(ys, axis=1) * _silu(z_ref[rows, gcols].astype(F32))
            y = y * lax.rsqrt(jnp.mean(y * y, axis=-1, keepdims=True) + EPS)
            o_ref[rows, gcols] = (y * nw_ref[:, gcols]).astype(o_ref.dtype)
        return carry

    lax.fori_loop(0, SSM_GROUPS, body, 0)


def _ssd_scan2(geo, proj, xbc, cc, cp, cpt, dtt, dskip, norm_w):
    rows = geo.step_rows
    cps = rows // CHUNK
    bc_w = SSM_GROUPS * SSM_STATE

    def bwd(b, s):
        return geo.step_block(b, geo.backward_order(s))

    def fwd(b, s):
        return geo.step_block(b, s)

    hist_shape = (geo.batch, geo.ns * cps, SSM_GROUPS, SSM_STATE, SSM_GROUP_W)
    hist_block = (None, cps) + hist_shape[2:]
    hist = pl.pallas_call(
        functools.partial(_ssd_state_kernel, cps=cps),
        grid=(geo.batch, geo.ns),
        in_specs=[
            pl.BlockSpec((rows, bc_w), lambda b, s: (bwd(b, s), SSM_INNER // bc_w)),
            pl.BlockSpec((rows, SSM_INNER), lambda b, s: (bwd(b, s), 0)),
            pl.BlockSpec((rows, LANES), lambda b, s: (bwd(b, s), 0)),
            pl.BlockSpec((rows, LANES), lambda b, s: (bwd(b, s), 0)),
        ],
        out_specs=pl.BlockSpec(hist_block, lambda b, s: (b, geo.backward_order(s), 0, 0, 0)),
        out_shape=jax.ShapeDtypeStruct(hist_shape, BF16),
        scratch_shapes=[pltpu.VMEM((SSM_GROUPS, SSM_STATE, SSM_GROUP_W), F32)],
        compiler_params=_params(2),
        name="ssd_state",
    )(xbc, xbc, cc, cp)
    col = pl.BlockSpec((rows, LANES), lambda b, s: (fwd(b, s), 0))
    row = pl.BlockSpec((cps, LANES, CHUNK), lambda b, s: (fwd(b, s), 0, 0))
    return pl.pallas_call(
        functools.partial(_ssd_out_kernel, cps=cps),
        grid=(geo.batch, geo.ns),
        in_specs=[
            pl.BlockSpec((rows, bc_w), lambda b, s: (fwd(b, s), SSM_INNER // bc_w + 1)),
            pl.BlockSpec((rows, bc_w), lambda b, s: (fwd(b, s), SSM_INNER // bc_w)),
            pl.BlockSpec((rows, SSM_INNER), lambda b, s: (fwd(b, s), 0)),
            pl.BlockSpec((rows, SSM_INNER), lambda b, s: (fwd(b, s), COL_Z // SSM_INNER)),
            col, col, row, row,
            pl.BlockSpec(hist_block, lambda b, s: (b, s, 0, 0, 0)),
            pl.BlockSpec((1, SSM_INNER), lambda b, s: (0, 0)),
            pl.BlockSpec((1, SSM_INNER), lambda b, s: (0, 0)),
        ],
        out_specs=pl.BlockSpec((rows, SSM_INNER), lambda b, s: (fwd(b, s), 0)),
        out_shape=jax.ShapeDtypeStruct((geo.rows, SSM_INNER), BF16),
        scratch_shapes=[pltpu.VMEM((SSM_GROUPS, SSM_STATE, SSM_GROUP_W), F32)],
        compiler_params=_params(2),
        name="ssd_out",
    )(xbc, xbc, xbc, proj, cc, cp, cpt, dtt, hist, dskip, norm_w)


def _merge_kernel(yr_ref, ys_ref, wr_ref, ws_ref, gr_ref, gs_ref, o_ref):
    r = _dot(yr_ref[...], wr_ref[...])
    s = _dot(ys_ref[...], ws_ref[...])
    m = jax.nn.sigmoid(gr_ref[...].astype(F32)) * r + jax.nn.sigmoid(gs_ref[...].astype(F32)) * s
    o_ref[...] = m.astype(o_ref.dtype)


def _merge(geo, yr, ys, w_ret, w_ssm, proj):
    tm, tn = geo.tm, 512
    gr0 = COL_GATES // tn
    gs0 = (COL_GATES + D_MODEL) // tn
    return pl.pallas_call(
        _merge_kernel,
        grid=(geo.n_tiles, D_MODEL // tn),
        in_specs=[
            pl.BlockSpec((tm, yr.shape[1]), lambda i, j: (i, 0)),
            pl.BlockSpec((tm, ys.shape[1]), lambda i, j: (i, 0)),
            pl.BlockSpec((yr.shape[1], tn), lambda i, j: (0, j)),
            pl.BlockSpec((ys.shape[1], tn), lambda i, j: (0, j)),
            pl.BlockSpec((tm, tn), lambda i, j: (i, gr0 + j)),
            pl.BlockSpec((tm, tn), lambda i, j: (i, gs0 + j)),
        ],
        out_specs=pl.BlockSpec((tm, tn), lambda i, j: (i, j)),
        out_shape=jax.ShapeDtypeStruct((geo.rows, D_MODEL), BF16),
        compiler_params=_params(2),
        name="branch_merge",
    )(yr, ys, w_ret, w_ssm, proj, proj)


def _residual_kernel(a_ref, w_ref, x_ref, mod_ref, o_ref, *, gate_row):
    o_ref[...] = x_ref[...] + mod_ref[gate_row:gate_row + 1, :] * _dot(a_ref[...], w_ref[...])


def _matmul_residual(geo, a, w, xs, mod_l, gate_row, tile0, n_tiles, tn):
    tm = geo.tm
    kdim = a.shape[1]
    return pl.pallas_call(
        functools.partial(_residual_kernel, gate_row=gate_row),
        grid=(n_tiles, D_MODEL // tn),
        in_specs=[
            pl.BlockSpec((tm, kdim), lambda i, j: (i, 0)),
            pl.BlockSpec((kdim, tn), lambda i, j: (0, j)),
            pl.BlockSpec((tm, tn), lambda i, j: (tile0 + i, j)),
            pl.BlockSpec((None, 6, tn), lambda i, j: (geo.mod_row(tile0 + i), 0, j)),
        ],
        out_specs=pl.BlockSpec((tm, tn), lambda i, j: (i, j)),
        out_shape=jax.ShapeDtypeStruct((n_tiles * tm, D_MODEL), F32),
        compiler_params=_params(2),
        name="proj_residual",
    )(a, w, xs, mod_l)


def _ffn_up_kernel(x_ref, mod_ref, nw_ref, wg_ref, wu_ref, o_ref, h_ref):
    @pl.when(pl.program_id(1) == 0)
    def _():
        h_ref[...] = _norm_mod(x_ref[...], nw_ref[...], mod_ref[3:4, :], mod_ref[4:5, :]).astype(BF16)

    h = h_ref[...]
    o_ref[...] = (_silu(_dot(h, wg_ref[...])) * _dot(h, wu_ref[...])).astype(o_ref.dtype)


def _ffn_up(geo, xs, mod_l, nw, w_gate, w_up):
    tm, tf = geo.tm, 512
    return pl.pallas_call(
        _ffn_up_kernel,
        grid=(geo.n_tiles, D_FF // tf),
        in_specs=[
            pl.BlockSpec((tm, D_MODEL), lambda i, j: (i, 0)),
            pl.BlockSpec((None, 6, D_MODEL), lambda i, j: (geo.mod_row(i), 0, 0)),
            pl.BlockSpec((1, D_MODEL), lambda i, j: (0, 0)),
            pl.BlockSpec((D_MODEL, tf), lambda i, j: (0, j)),
            pl.BlockSpec((D_MODEL, tf), lambda i, j: (0, j)),
        ],
        out_specs=pl.BlockSpec((tm, tf), lambda i, j: (i, j)),
        out_shape=jax.ShapeDtypeStruct((geo.rows, D_FF), BF16),
        scratch_shapes=[pltpu.VMEM((tm, D_MODEL), BF16)],
        compiler_params=_params(2),
        name="ffn_up",
    )(xs, mod_l, nw, w_gate, w_up)


def _moe_kernel(x_ref, mod_ref, nw_ref, router_ref, wg_ref, wu_ref, wd_ref, o_ref, h_ref, comb_ref):
    e, j = pl.program_id(1), pl.program_id(2)
    tm = x_ref.shape[0]
    lane = lax.broadcasted_iota(jnp.int32, (tm, LANES), 1)

    @pl.when(jnp.logical_and(e == 0, j == 0))
    def _():
        h = _norm_mod(x_ref[...], nw_ref[...], mod_ref[3:4, :], mod_ref[4:5, :])
        h_hi, h_lo = _split2(h)
        h_ref[...] = h_hi
        r_hi, r_lo = _split2(router_ref[...])
        logits = _dot(h_hi, r_hi) + _dot(h_hi, r_lo) + _dot(h_lo, r_hi)
        neg = -jnp.inf
        lg = jnp.where(lane < N_EXPERTS, logits, neg)
        m1 = jnp.max(lg, axis=-1, keepdims=True)
        i1 = jnp.min(jnp.where(lg == m1, lane, LANES), axis=-1, keepdims=True)
        lg2 = jnp.where(lane == i1, neg, lg)
        m2 = jnp.max(lg2, axis=-1, keepdims=True)
        i2 = jnp.min(jnp.where(lg2 == m2, lane, LANES), axis=-1, keepdims=True)
        e2 = jnp.exp(m2 - m1)
        w1 = 1.0 / (1.0 + e2)
        w2 = e2 / (1.0 + e2)
        comb_ref[...] = jnp.where(lane == i1, w1, 0.0) + jnp.where(lane == i2, w2, 0.0)
        o_ref[...] = jnp.zeros_like(o_ref)

    h = h_ref[...]
    ce = jnp.sum(jnp.where(lane == e, comb_ref[...], 0.0), axis=-1, keepdims=True)
    hid = (_silu(_dot(h, wg_ref[...])) * _dot(h, wu_ref[...]) * ce).astype(BF16)
    for n in range(0, D_MODEL, 512):
        o_ref[:, n:n + 512] += _dot(hid, wd_ref[:, n:n + 512])

    @pl.when(jnp.logical_and(e == pl.num_programs(1) - 1, j == pl.num_programs(2) - 1))
    def _():
        o_ref[...] = x_ref[...] + mod_ref[5:6, :] * o_ref[...]


def _moe(geo, xs, mod_l, nw, router, w_gate, w_up, w_down, tile0, n_tiles):
    sub = 2 if geo.tm >= 1024 else 1
    tm, tf = geo.tm // sub, 512
    tile0, n_tiles = tile0 * sub, n_tiles * sub
    return pl.pallas_call(
        _moe_kernel,
        grid=(n_tiles, N_EXPERTS, MOE_D_FF // tf),
        in_specs=[
            pl.BlockSpec((tm, D_MODEL), lambda i, e, j: (tile0 + i, 0)),
            pl.BlockSpec((None, 6, D_MODEL), lambda i, e, j: (geo.mod_row((tile0 + i) // sub), 0, 0)),
            pl.BlockSpec((1, D_MODEL), lambda i, e, j: (0, 0)),
            pl.BlockSpec((D_MODEL, LANES), lambda i, e, j: (0, 0)),
            pl.BlockSpec((None, D_MODEL, tf), lambda i, e, j: (e, 0, j)),
            pl.BlockSpec((None, D_MODEL, tf), lambda i, e, j: (e, 0, j)),
            pl.BlockSpec((None, tf, D_MODEL), lambda i, e, j: (e, j, 0)),
        ],
        out_specs=pl.BlockSpec((tm, D_MODEL), lambda i, e, j: (i, 0)),
        out_shape=jax.ShapeDtypeStruct((n_tiles * tm, D_MODEL), F32),
        scratch_shapes=[pltpu.VMEM((tm, D_MODEL), BF16), pltpu.VMEM((tm, LANES), F32)],
        compiler_params=_params(3),
        name="moe_dense",
    )(xs, mod_l, nw, router, w_gate, w_up, w_down)


MOE_ROW_TILE = 512
ROUTE_W1, ROUTE_W2, ROUTE_E1, ROUTE_E2 = 8, 9, 10, 11
DMA_ROWS = 256


def _route_kernel(x_ref, mod_ref, nw_ref, router_ref, h_ref, route_ref, cnt_ref, tri_ref, carry_ref):
    i = pl.program_id(0)
    tm = x_ref.shape[0]
    lane = lax.broadcasted_iota(jnp.int32, (tm, LANES), 1)

    @pl.when(i == 0)
    def _():
        ii = lax.broadcasted_iota(jnp.int32, (tm, tm), 0)
        jj = lax.broadcasted_iota(jnp.int32, (tm, tm), 1)
        tri_ref[...] = (jj < ii).astype(BF16)
        carry_ref[...] = jnp.zeros_like(carry_ref)

    h = _norm_mod(x_ref[...], nw_ref[...], mod_ref[3:4, :], mod_ref[4:5, :])
    h_hi, h_lo = _split2(h)
    h_ref[...] = h_hi.astype(F32)
    r_hi, r_lo = _split2(router_ref[...])
    logits = _dot(h_hi, r_hi) + _dot(h_hi, r_lo) + _dot(h_lo, r_hi)
    neg = -jnp.inf
    lg = jnp.where(lane < N_EXPERTS, logits, neg)
    m1 = jnp.max(lg, axis=-1, keepdims=True)
    i1 = jnp.min(jnp.where(lg == m1, lane, LANES), axis=-1, keepdims=True)
    lg2 = jnp.where(lane == i1, neg, lg)
    m2 = jnp.max(lg2, axis=-1, keepdims=True)
    i2 = jnp.min(jnp.where(lg2 == m2, lane, LANES), axis=-1, keepdims=True)
    e2 = jnp.exp(m2 - m1)
    w1 = 1.0 / (1.0 + e2)
    w2 = e2 / (1.0 + e2)
    chosen = jnp.logical_or(lane == i1, lane == i2)
    mask = jnp.where(chosen, 1.0, 0.0)
    pos = _dot(tri_ref[...], mask.astype(BF16)) + carry_ref[0:1, :]
    carry_ref[0:1, :] = carry_ref[0:1, :] + jnp.sum(mask, axis=0, keepdims=True)
    rec = jnp.where(lane < N_EXPERTS, pos, 0.0)
    rec = jnp.where(lane == ROUTE_W1, w1, rec)
    rec = jnp.where(lane == ROUTE_W2, w2, rec)
    rec = jnp.where(lane == ROUTE_E1, i1.astype(F32), rec)
    rec = jnp.where(lane == ROUTE_E2, i2.astype(F32), rec)
    route_ref[...] = rec
    cnt_ref[...] = carry_ref[...]


def _slots_kernel(route_ref, start_ref, o_ref):
    rec = route_ref[...]
    lane = lax.broadcasted_iota(jnp.int32, rec.shape, 1)
    slot = rec + start_ref[...]
    e1 = rec[:, ROUTE_E1:ROUTE_E1 + 1].astype(jnp.int32)
    e2 = rec[:, ROUTE_E2:ROUTE_E2 + 1].astype(jnp.int32)
    d1 = jnp.sum(jnp.where(lane == e1, slot, 0.0), axis=-1, keepdims=True)
    d2 = jnp.sum(jnp.where(lane == e2, slot, 0.0), axis=-1, keepdims=True)
    o_ref[...] = jnp.where(lane == 0, d1, jnp.where(lane == 1, d2, 0.0)).astype(jnp.int32)


def _row_copy(src_ref, dst_ref, src_row, dst_row, sem):
    return pltpu.make_async_copy(src_ref.at[src_row], dst_ref.at[dst_row], sem)


def _dispatch_kernel(d1_ref, d2_ref, h_ref, zeros_ref, o_ref, sem):
    del zeros_ref
    base = pl.program_id(0) * DMA_ROWS

    def issue(r, carry):
        _row_copy(h_ref, o_ref, base + r, d1_ref[r], sem).start()
        _row_copy(h_ref, o_ref, base + r, d2_ref[r], sem).start()
        return carry

    def drain(r, carry):
        _row_copy(h_ref, o_ref, 0, 0, sem).wait()
        _row_copy(h_ref, o_ref, 0, 0, sem).wait()
        return carry

    lax.fori_loop(0, DMA_ROWS, issue, 0)
    lax.fori_loop(0, DMA_ROWS, drain, 0)


def _collect_kernel(d1_ref, d2_ref, y_ref, o_ref, sem):
    base = pl.program_id(0) * DMA_ROWS

    def issue(r, carry):
        _row_copy(y_ref, o_ref.at[0], d1_ref[r], base + r, sem).start()
        _row_copy(y_ref, o_ref.at[1], d2_ref[r], base + r, sem).start()
        return carry

    def drain(r, carry):
        _row_copy(y_ref, o_ref.at[0], 0, 0, sem).wait()
        _row_copy(y_ref, o_ref.at[0], 0, 0, sem).wait()
        return carry

    lax.fori_loop(0, DMA_ROWS, issue, 0)
    lax.fori_loop(0, DMA_ROWS, drain, 0)


def _expert_up_kernel(te_ref, nu_ref, x_ref, wg_ref, wu_ref, o_ref):
    del te_ref
    live = pl.program_id(1) < nu_ref[0]

    @pl.when(live)
    def _():
        x = x_ref[...].astype(BF16)
        o_ref[...] = (_silu(_dot(x, wg_ref[...])) * _dot(x, wu_ref[...])).astype(o_ref.dtype)

    @pl.when(jnp.logical_not(live))
    def _():
        o_ref[...] = jnp.zeros_like(o_ref)


def _expert_down_kernel(te_ref, nu_ref, h_ref, wd_ref, o_ref):
    del te_ref
    live = pl.program_id(1) < nu_ref[0]

    @pl.when(live)
    def _():
        o_ref[...] = _dot(h_ref[...], wd_ref[...]).astype(o_ref.dtype)

    @pl.when(jnp.logical_not(live))
    def _():
        o_ref[...] = jnp.zeros_like(o_ref)


def _combine_kernel(x_ref, y1_ref, y2_ref, route_ref, mod_ref, o_ref):
    rec = route_ref[...]
    w1 = rec[:, ROUTE_W1:ROUTE_W1 + 1]
    w2 = rec[:, ROUTE_W2:ROUTE_W2 + 1]
    y = w1 * y1_ref[...].astype(F32) + w2 * y2_ref[...].astype(F32)
    o_ref[...] = x_ref[...] + mod_ref[5:6, :] * y


def _moe_sparse(geo, xs, mod_l, nw, router, w_gate, w_up, w_down, tile0, n_tiles):
    tm, tg = geo.tm, MOE_ROW_TILE
    rows = n_tiles * tm
    n_slots = 2 * rows + N_EXPERTS * tg
    n_gt = n_slots // tg
    row_tile = pl.BlockSpec((tm, D_MODEL), lambda i: (i, 0))
    rec_tile = pl.BlockSpec((tm, LANES), lambda i: (i, 0))

    h2, route, counts = pl.pallas_call(
        _route_kernel,
        grid=(n_tiles,),
        in_specs=[
            pl.BlockSpec((tm, D_MODEL), lambda i: (tile0 + i, 0)),
            pl.BlockSpec((None, 6, D_MODEL), lambda i: (geo.mod_row(tile0 + i), 0, 0)),
            pl.BlockSpec((1, D_MODEL), lambda i: (0, 0)),
            pl.BlockSpec((D_MODEL, LANES), lambda i: (0, 0)),
        ],
        out_specs=[row_tile, rec_tile, pl.BlockSpec((8, LANES), lambda i: (0, 0))],
        out_shape=[
            jax.ShapeDtypeStruct((rows, D_MODEL), F32),
            jax.ShapeDtypeStruct((rows, LANES), F32),
            jax.ShapeDtypeStruct((8, LANES), F32),
        ],
        scratch_shapes=[pltpu.VMEM((tm, tm), BF16), pltpu.VMEM((8, LANES), F32)],
        compiler_params=_params(1),
        name="moe_route",
    )(xs, mod_l, nw, router)

    cnt = counts[0, :N_EXPERTS].astype(jnp.int32)
    padded = ((cnt + tg - 1) // tg) * tg
    ends = jnp.cumsum(padded)
    starts = ends - padded
    n_used = (ends[-1] // tg).astype(jnp.int32).reshape(1)
    tile_expert = jnp.searchsorted(ends, jnp.minimum(jnp.arange(n_gt), n_used[0] - 1) * tg, side="right")
    tile_expert = jnp.minimum(tile_expert, N_EXPERTS - 1).astype(jnp.int32)
    start_row = jnp.zeros((1, LANES), F32).at[0, :N_EXPERTS].set(starts.astype(F32))

    dest = pl.pallas_call(
        _slots_kernel,
        grid=(n_tiles,),
        in_specs=[rec_tile, pl.BlockSpec((1, LANES), lambda i: (0, 0))],
        out_specs=rec_tile,
        out_shape=jax.ShapeDtypeStruct((rows, LANES), jnp.int32),
        compiler_params=_params(1),
        name="moe_slots",
    )(route, start_row)
    d1, d2 = dest[:, 0], dest[:, 1]

    idx_spec = pl.BlockSpec((DMA_ROWS,), lambda i: (i,), memory_space=pltpu.SMEM)
    any_spec = pl.BlockSpec(memory_space=pl.ANY)
    xg = pl.pallas_call(
        _dispatch_kernel,
        grid=(rows // DMA_ROWS,),
        in_specs=[idx_spec, idx_spec, any_spec, any_spec],
        out_specs=any_spec,
        out_shape=jax.ShapeDtypeStruct((n_slots, 1, D_MODEL), F32),
        scratch_shapes=[pltpu.SemaphoreType.DMA(())],
        input_output_aliases={3: 0},
        compiler_params=_params(1),
        name="moe_dispatch",
    )(d1, d2, h2.reshape(rows, 1, D_MODEL), jnp.zeros((n_slots, 1, D_MODEL), F32))
    xg = xg.reshape(n_slots, D_MODEL)

    tf = 1024
    hg = pl.pallas_call(
        _expert_up_kernel,
        grid_spec=pltpu.PrefetchScalarGridSpec(
            num_scalar_prefetch=2,
            grid=(MOE_D_FF // tf, n_gt),
            in_specs=[
                pl.BlockSpec((tg, D_MODEL), lambda j, r, te, nu: (r, 0)),
                pl.BlockSpec((None, D_MODEL, tf), lambda j, r, te, nu: (te[r], 0, j)),
                pl.BlockSpec((None, D_MODEL, tf), lambda j, r, te, nu: (te[r], 0, j)),
            ],
            out_specs=pl.BlockSpec((tg, tf), lambda j, r, te, nu: (r, j)),
        ),
        out_shape=jax.ShapeDtypeStruct((n_slots, MOE_D_FF), BF16),
        compiler_params=_params(2),
        name="moe_expert_up",
    )(tile_expert, n_used, xg, w_gate, w_up)

    tn = 1024
    yg = pl.pallas_call(
        _expert_down_kernel,
        grid_spec=pltpu.PrefetchScalarGridSpec(
            num_scalar_prefetch=2,
            grid=(D_MODEL // tn, n_gt),
            in_specs=[
                pl.BlockSpec((tg, MOE_D_FF), lambda j, r, te, nu: (r, 0)),
                pl.BlockSpec((None, MOE_D_FF, tn), lambda j, r, te, nu: (te[r], 0, j)),
            ],
            out_specs=pl.BlockSpec((tg, tn), lambda j, r, te, nu: (r, j)),
        ),
        out_shape=jax.ShapeDtypeStruct((n_slots, D_MODEL), F32),
        compiler_params=_params(2),
        name="moe_expert_down",
    )(tile_expert, n_used, hg, w_down)

    y12 = pl.pallas_call(
        _collect_kernel,
        grid=(rows // DMA_ROWS,),
        in_specs=[idx_spec, idx_spec, any_spec],
        out_specs=any_spec,
        out_shape=jax.ShapeDtypeStruct((2, rows, 1, D_MODEL), F32),
        scratch_shapes=[pltpu.SemaphoreType.DMA(())],
        compiler_params=_params(1),
        name="moe_collect",
    )(d1, d2, yg.reshape(n_slots, 1, D_MODEL))
    y12 = y12.reshape(2, rows, D_MODEL)

    sub = 2 if tm >= 1024 else 1
    tc = tm // sub
    return pl.pallas_call(
        _combine_kernel,
        grid=(n_tiles * sub,),
        in_specs=[
            pl.BlockSpec((tc, D_MODEL), lambda i: (tile0 * sub + i, 0)),
            pl.BlockSpec((None, tc, D_MODEL), lambda i: (0, i, 0)),
            pl.BlockSpec((None, tc, D_MODEL), lambda i: (1, i, 0)),
            pl.BlockSpec((tc, LANES), lambda i: (i, 0)),
            pl.BlockSpec((None, 6, D_MODEL), lambda i: (geo.mod_row(tile0 + i // sub), 0, 0)),
        ],
        out_specs=pl.BlockSpec((tc, D_MODEL), lambda i: (i, 0)),
        out_shape=jax.ShapeDtypeStruct((rows, D_MODEL), F32),
        compiler_params=_params(1),
        name="moe_combine",
    )(xs, y12, y12, route, mod_l)


def _final_norm_kernel(x_ref, w_ref, o_ref):
    x = x_ref[...]
    o_ref[...] = x * lax.rsqrt(jnp.mean(x * x, axis=-1, keepdims=True) + EPS) * w_ref[...]


def _final_norm(x, w, tm):
    rows = x.shape[0]
    return pl.pallas_call(
        _final_norm_kernel,
        grid=(rows // tm,),
        in_specs=[pl.BlockSpec((tm, D_MODEL), lambda i: (i, 0)), pl.BlockSpec((1, D_MODEL), lambda i: (0, 0))],
        out_specs=pl.BlockSpec((tm, D_MODEL), lambda i: (i, 0)),
        out_shape=jax.ShapeDtypeStruct((rows, D_MODEL), F32),
        compiler_params=_params(1),
        name="final_norm",
    )(x, w)


def _rope_tables(geo):
    half = RET_DK // 4
    inv = ROPE_BASE ** (-jnp.arange(half, dtype=F32) / half)
    pos = jnp.arange(geo.seq)
    ang_r = (pos // GRID_W).astype(F32)[:, None] * inv[None, :]
    ang_c = (pos % GRID_W).astype(F32)[:, None] * inv[None, :]
    cos = jnp.concatenate([jnp.cos(ang_r), jnp.cos(ang_r), jnp.cos(ang_c), jnp.cos(ang_c)], axis=1)
    sin = jnp.concatenate([-jnp.sin(ang_r), jnp.sin(ang_r), -jnp.sin(ang_c), jnp.sin(ang_c)], axis=1)
    cos = jnp.concatenate([jnp.ones((geo.tm, LANES), F32), cos], axis=0)
    sin = jnp.concatenate([jnp.zeros((geo.tm, LANES), F32), sin], axis=0)
    return cos, sin


_DT_PERM = np.array([d * SSM_HEADS + g * SSM_HPG + h
                     for g in range(SSM_GROUPS) for d in range(2) for h in range(SSM_HPG)])


def kernel(x, c, ctx, c_ctx, w_ada, b_ada, norm1_w, norm2_w, w_in, conv_w, conv_b, ret_decay_f, ret_decay_b, ret_gn_w, ssm_a_log_f, ssm_a_log_b, ssm_dt_bias_f, ssm_dt_bias_b, ssm_d, ssm_norm_w, w_ret_proj, w_ssm_proj, w_out, ffn_w_gate, ffn_w_up, ffn_w_down, moe_router, moe_w_gate, moe_w_up, moe_w_down, final_norm_w):
    batch, seq, d = x.shape
    ctx_len = ctx.shape[1]
    depth = w_ada.shape[0]
    assert d == D_MODEL and seq % GRID_W == 0
    geo = _Geom(batch, ctx_len, seq)
    tm = geo.tm

    mod_rows = -(-(batch + 1) // 8) * 8
    cvec = jnp.zeros((mod_rows, d), F32).at[0].set(c_ctx).at[1:batch + 1].set(c)
    mod = _modulation(cvec, w_ada, b_ada).reshape(depth, mod_rows, 6, d)
    cos_t, sin_t = _rope_tables(geo)

    xs = jnp.concatenate([ctx.reshape(batch * ctx_len, d), x.reshape(batch * seq, d)], axis=0)
    lat_tile0 = geo.n_ctx_tiles
    n_lat_tiles = geo.n_tiles - geo.n_ctx_tiles

    for i in range(depth):
        mod_l = mod[i]
        w_main = jnp.concatenate([w_in[i][:, ORIG_Z_LO:ORIG_Z_HI], w_in[i][:, :ORIG_Z_LO],
                                  w_in[i][:, ORIG_Z_HI:ORIG_DT_LO], w_in[i][:, ORIG_DT_HI:]], axis=1).astype(BF16)
        w_dt = w_in[i][:, ORIG_DT_LO:ORIG_DT_HI][:, _DT_PERM].astype(BF16)
        proj, dt = _inproj(geo, xs, mod_l, norm1_w[i][None, :], w_main, w_dt, cos_t, sin_t)

        xbc = _conv_silu(geo, proj, conv_w[i], conv_b[i][None, :])
        bias = jnp.concatenate([ssm_dt_bias_f[i], ssm_dt_bias_b[i]])[_DT_PERM][None, :]
        alog = jnp.concatenate([ssm_a_log_f[i], ssm_a_log_b[i]])[_DT_PERM][None, :]
        cc, cp, cpt, dtt = _decay_tables(geo, dt, bias, alog)
        dskip = jnp.repeat(ssm_d[i], SSM_INNER // SSM_HEADS)[None, :]
        ys = _ssd_scan2(geo, proj, xbc, cc, cp, cpt, dtt, dskip, ssm_norm_w[i][None, :])
        yr = _ret_scan2(geo, proj, jnp.stack([ret_decay_f[i], ret_decay_b[i]]), ret_gn_w[i][None, :])

        merged = _merge(geo, yr, ys, w_ret_proj[i].astype(BF16), w_ssm_proj[i].astype(BF16), proj)
        xs = _matmul_residual(geo, merged, w_out[i].astype(BF16), xs, mod_l, 2, 0, geo.n_tiles, 1024)

        j = i // 2
        tile0, n_tiles = (0, geo.n_tiles) if i < depth - 1 else (lat_tile0, n_lat_tiles)
        if i % 2 == 0:
            hid = _ffn_up(geo, xs, mod_l, norm2_w[i][None, :], ffn_w_gate[j].astype(BF16), ffn_w_up[j].astype(BF16))
            hid = hid[tile0 * tm:]
            xs = _matmul_residual(geo, hid, ffn_w_down[j].astype(BF16), xs, mod_l, 5, tile0, n_tiles, 512)
        else:
            router = jnp.zeros((d, LANES), F32).at[:, :N_EXPERTS].set(moe_router[j])
            xs = _moe_sparse(geo, xs, mod_l, norm2_w[i][None, :], router, moe_w_gate[j].astype(BF16),
                             moe_w_up[j].astype(BF16), moe_w_down[j].astype(BF16), tile0, n_tiles)

    lat = xs if xs.shape[0] == batch * seq else xs[batch * ctx_len:]
    return _final_norm(lat, final_norm_w[None, :], tm).reshape(batch, seq, d)
```

```python
import functools

import numpy as np
import jax
import jax.numpy as jnp
from jax import lax
from jax.experimental import pallas as pl
from jax.experimental.pallas import tpu as pltpu

F32 = jnp.float32
BF16 = jnp.bfloat16

D_MODEL = 2048
GRID_W = 64
CHUNK = 128
EPS = 1e-6
ROPE_BASE = 10000.0
RET_HEADS = 8
RET_DK = 128
RET_DV = 256
SSM_INNER = 4096
SSM_HEADS = 64
SSM_GROUPS = 8
SSM_HPG = 8
SSM_GROUP_W = SSM_INNER // SSM_GROUPS
SSM_STATE = 128
XBC_W = 6144
CONV_K = 5
D_FF = 5632
N_EXPERTS = 8
MOE_D_FF = 4096

COL_Z, COL_Q, COL_K, COL_V, COL_G, COL_XBC, COL_GATES = 0, 4096, 5120, 6144, 8192, 10240, 16384
PROJ_W = 20480
ORIG_Z_LO, ORIG_Z_HI, ORIG_DT_LO, ORIG_DT_HI = 6144, 10240, 16384, 16512
SCAN_ROWS = 256

LANES = 128
VMEM_LIMIT_BYTES = 56 * 1024 * 1024


def _params(n_axes, vmem=VMEM_LIMIT_BYTES):
    return pltpu.CompilerParams(dimension_semantics=("arbitrary",) * n_axes, vmem_limit_bytes=vmem)


def _silu(x):
    return x * jax.nn.sigmoid(x)


def _split2(x):
    hi = x.astype(BF16)
    lo = (x - hi.astype(F32)).astype(BF16)
    return hi, lo


def _split3(x):
    hi = x.astype(BF16)
    r = x - hi.astype(F32)
    mid = r.astype(BF16)
    lo = (r - mid.astype(F32)).astype(BF16)
    return hi, mid, lo


def _dot(a, b):
    return jnp.dot(a, b, preferred_element_type=F32)


def _dot_nt(a, b):
    return lax.dot_general(a, b, (((1,), (1,)), ((), ())), preferred_element_type=F32)


def _dot_tn(a, b):
    return lax.dot_general(a, b, (((0,), (0,)), ((), ())), preferred_element_type=F32)


def _norm_mod(x, nw, shift, scale):
    y = x * lax.rsqrt(jnp.mean(x * x, axis=-1, keepdims=True) + EPS)
    return (y * nw) * (1.0 + scale) + shift


def _mod_kernel(c_ref, w_ref, b_ref, o_ref):
    s_hi, s_lo = _split2(_silu(c_ref[...]))
    w_hi, w_lo = _split2(w_ref[...])
    o_ref[...] = _dot(s_hi, w_hi) + _dot(s_hi, w_lo) + _dot(s_lo, w_hi) + b_ref[...]


def _modulation(cvec, w_ada, b_ada):
    depth, d, w6 = w_ada.shape
    rows = cvec.shape[0]
    tn = 512
    return pl.pallas_call(
        _mod_kernel,
        grid=(depth, w6 // tn),
        in_specs=[
            pl.BlockSpec((rows, d), lambda l, j: (0, 0)),
            pl.BlockSpec((None, d, tn), lambda l, j: (l, 0, j)),
            pl.BlockSpec((None, 1, tn), lambda l, j: (l, 0, j)),
        ],
        out_specs=pl.BlockSpec((None, rows, tn), lambda l, j: (l, 0, j)),
        out_shape=jax.ShapeDtypeStruct((depth, rows, w6), F32),
        compiler_params=_params(2),
        name="adaln_mod",
    )(cvec, w_ada, b_ada.reshape(depth, 1, w6))


class _Geom:
    def __init__(self, batch, ctx_len, seq):
        self.batch, self.ctx_len, self.seq = batch, ctx_len, seq
        tm = 1024
        while (batch * ctx_len) % tm or seq % tm:
            tm //= 2
        assert tm >= CHUNK and ctx_len % CHUNK == 0 and seq % CHUNK == 0
        self.tm = tm
        self.ctx_rows = batch * ctx_len
        self.rows = self.ctx_rows + batch * seq
        self.n_ctx_tiles = self.ctx_rows // tm
        self.tiles_per_batch = seq // tm
        self.n_tiles = self.rows // tm
        self.step_rows = min(SCAN_ROWS, ctx_len)
        assert ctx_len % self.step_rows == 0 and seq % self.step_rows == 0 and self.step_rows % CHUNK == 0
        self.ns_ctx = ctx_len // self.step_rows
        self.ns_lat = seq // self.step_rows
        self.ns = self.ns_ctx + self.ns_lat

    def mod_row(self, i):
        return jnp.where(i < self.n_ctx_tiles, 0, 1 + (i - self.n_ctx_tiles) // self.tiles_per_batch)

    def rope_block(self, i):
        return jnp.where(i < self.n_ctx_tiles, 0, 1 + (i - self.n_ctx_tiles) % self.tiles_per_batch)

    def step_block(self, b, c):
        return jnp.where(c < self.ns_ctx, b * self.ns_ctx + c,
                         self.batch * self.ns_ctx + b * self.ns_lat + (c - self.ns_ctx))

    def backward_order(self, s):
        return jnp.where(s < self.ns_ctx, self.ns_ctx - 1 - s, self.ns - 1 - (s - self.ns_ctx))


def _rope_store(acc, cos, sin, o_ref, scale):
    lane = lax.broadcasted_iota(jnp.int32, (acc.shape[0], LANES), 1)
    first_half = (lane % 64) < 32
    for h in range(acc.shape[1] // LANES):
        xh = acc[:, h * LANES:(h + 1) * LANES]
        partner = jnp.where(first_half, pltpu.roll(xh, 96, 1), pltpu.roll(xh, 32, 1))
        o_ref[:, h * LANES:(h + 1) * LANES] = ((xh * cos + partner * sin) * scale).astype(o_ref.dtype)


def _inproj_kernel(x_ref, mod_ref, nw_ref, w_ref, wdt_ref, cos_ref, sin_ref, o_ref, dt_ref, h_ref):
    j = pl.program_id(1)

    @pl.when(j == 0)
    def _():
        h = _norm_mod(x_ref[...], nw_ref[...], mod_ref[0:1, :], mod_ref[1:2, :]).astype(BF16)
        h_ref[...] = h
        dt_ref[...] = _dot(h, wdt_ref[...])

    acc = _dot(h_ref[...], w_ref[...])

    jq, jk = COL_Q // acc.shape[1], COL_K // acc.shape[1]

    @pl.when(j == jq)
    def _():
        _rope_store(acc, cos_ref[...], sin_ref[...], o_ref, 1.0)

    @pl.when(j == jk)
    def _():
        _rope_store(acc, cos_ref[...], sin_ref[...], o_ref, RET_DK ** -0.5)

    @pl.when(jnp.logical_and(j != jq, j != jk))
    def _():
        o_ref[...] = acc.astype(o_ref.dtype)


def _inproj(geo, xs, mod_l, nw, w_main, w_dt, cos_t, sin_t):
    tm, tn = geo.tm, 1024
    return pl.pallas_call(
        _inproj_kernel,
        grid=(geo.n_tiles, PROJ_W // tn),
        in_specs=[
            pl.BlockSpec((tm, D_MODEL), lambda i, j: (i, 0)),
            pl.BlockSpec((None, 6, D_MODEL), lambda i, j: (geo.mod_row(i), 0, 0)),
            pl.BlockSpec((1, D_MODEL), lambda i, j: (0, 0)),
            pl.BlockSpec((D_MODEL, tn), lambda i, j: (0, j)),
            pl.BlockSpec((D_MODEL, LANES), lambda i, j: (0, 0)),
            pl.BlockSpec((tm, LANES), lambda i, j: (geo.rope_block(i), 0)),
            pl.BlockSpec((tm, LANES), lambda i, j: (geo.rope_block(i), 0)),
        ],
        out_specs=[
            pl.BlockSpec((tm, tn), lambda i, j: (i, j)),
            pl.BlockSpec((tm, LANES), lambda i, j: (i, 0)),
        ],
        out_shape=[
            jax.ShapeDtypeStruct((geo.rows, PROJ_W), BF16),
            jax.ShapeDtypeStruct((geo.rows, LANES), F32),
        ],
        scratch_shapes=[pltpu.VMEM((tm, D_MODEL), BF16)],
        compiler_params=_params(2),
        name="in_proj",
    )(xs, mod_l, nw, w_main, w_dt, cos_t, sin_t)


CONV_HALO = 16


def _conv_kernel(prev_ref, main_ref, next_ref, w_ref, b_ref, o_ref, ext_ref, *, geo, rows):
    i = pl.program_id(0)
    start = i * rows
    in_ctx = start < geo.ctx_rows
    seq_len = jnp.where(in_ctx, geo.ctx_len, geo.seq)
    off = jnp.where(in_ctx, start, start - geo.ctx_rows)
    first = (off % seq_len) == 0
    last = ((off + rows) % seq_len) == 0
    ext_ref[0:CONV_HALO, :] = jnp.where(first, 0.0, prev_ref[...].astype(F32))
    ext_ref[CONV_HALO:CONV_HALO + rows, :] = main_ref[...].astype(F32)
    ext_ref[CONV_HALO + rows:, :] = jnp.where(last, 0.0, next_ref[...].astype(F32))
    acc = jnp.broadcast_to(b_ref[...], (rows, b_ref.shape[1]))
    for k in range(CONV_K):
        acc = acc + ext_ref[pl.ds(CONV_HALO + k - CONV_K // 2, rows), :] * w_ref[k:k + 1, :]
    o_ref[...] = _silu(acc).astype(o_ref.dtype)


def _conv_silu(geo, proj, conv_w, conv_b):
    rows = min(256, geo.ctx_len)
    tc = 1024
    col0 = COL_XBC // tc
    n_halo = geo.rows // CONV_HALO
    per = rows // CONV_HALO
    return pl.pallas_call(
        functools.partial(_conv_kernel, geo=geo, rows=rows),
        grid=(geo.rows // rows, XBC_W // tc),
        in_specs=[
            pl.BlockSpec((CONV_HALO, tc), lambda i, j: (jnp.maximum(i * per - 1, 0), col0 + j)),
            pl.BlockSpec((rows, tc), lambda i, j: (i, col0 + j)),
            pl.BlockSpec((CONV_HALO, tc), lambda i, j: (jnp.minimum((i + 1) * per, n_halo - 1), col0 + j)),
            pl.BlockSpec((CONV_K, tc), lambda i, j: (0, j)),
            pl.BlockSpec((1, tc), lambda i, j: (0, j)),
        ],
        out_specs=pl.BlockSpec((rows, tc), lambda i, j: (i, j)),
        out_shape=jax.ShapeDtypeStruct((geo.rows, XBC_W), BF16),
        scratch_shapes=[pltpu.VMEM((rows + 2 * CONV_HALO, tc), F32)],
        compiler_params=_params(2),
        name="ssm_conv",
    )(proj, proj, proj, conv_w, conv_b)


def _decay_kernel(dt_ref, bias_ref, alog_ref, cc_ref, cp_ref, cpt_ref, dtt_ref):
    x = dt_ref[...] + bias_ref[...]
    dtv = jnp.maximum(x, 0.0) + jnp.log1p(jnp.exp(-jnp.abs(x)))
    la = dtv * (-jnp.exp(alog_ref[...]))
    ii = lax.broadcasted_iota(jnp.int32, (CHUNK, CHUNK), 0)
    jj = lax.broadcasted_iota(jnp.int32, (CHUNK, CHUNK), 1)
    lower = (jj <= ii).astype(BF16)
    upper = (jj >= ii).astype(BF16)
    hi, mid, lo = _split3(la)
    cum_f = _dot(lower, hi) + _dot(lower, mid) + _dot(lower, lo)
    cum_b = _dot(upper, hi) + _dot(upper, mid) + _dot(upper, lo)
    lane = lax.broadcasted_iota(jnp.int32, (CHUNK, LANES), 1)
    cc = jnp.where((lane % 16) < 8, cum_f, cum_b)
    cp = cc - jnp.log(dtv)
    cc_ref[...] = cc
    cp_ref[...] = cp
    cpt_ref[...] = cp.T
    dtt_ref[...] = dtv.T


def _decay_tables(geo, dt, bias, alog):
    n_chunks = geo.rows // CHUNK
    col = pl.BlockSpec((CHUNK, LANES), lambda i: (i, 0))
    row = pl.BlockSpec((None, LANES, CHUNK), lambda i: (i, 0, 0))
    vec = pl.BlockSpec((1, LANES), lambda i: (0, 0))
    col_shape = jax.ShapeDtypeStruct((geo.rows, LANES), F32)
    row_shape = jax.ShapeDtypeStruct((n_chunks, LANES, CHUNK), F32)
    return pl.pallas_call(
        _decay_kernel,
        grid=(n_chunks,),
        in_specs=[col, vec, vec],
        out_specs=[col, col, row, row],
        out_shape=[col_shape, col_shape, row_shape, row_shape],
        compiler_params=_params(1),
        name="ssm_decay",
    )(dt, bias, alog)


def _ssd_kernel(q_ref, k_ref, xs_ref, z_ref, cc_ref, cp_ref, cct_ref, cpt_ref, dtt_ref, dskip_ref,
                nw_ref, o_ref, sf_ref, sb_ref, hist_ref, *, geo):
    g, ph, s = pl.program_id(1), pl.program_id(2), pl.program_id(3)
    c = jnp.where(ph == 0, geo.backward_order(s), s)
    shift = (LANES - 16 * g) % LANES
    cc = pltpu.roll(cc_ref[...], shift, 1)
    cp = pltpu.roll(cp_ref[...], shift, 1)
    k = k_ref[...]
    xs = xs_ref[...]
    lane = lax.broadcasted_iota(jnp.int32, (CHUNK, LANES), 1)
    left = lane < 64
    left_row = lax.broadcasted_iota(jnp.int32, (1, LANES), 1) < 64
    n_pairs = SSM_HPG // 2

    def pair_cols(a, h1, h2):
        return jnp.where(left, a[:, h1:h1 + 1], a[:, h2:h2 + 1])

    @pl.when(jnp.logical_and(ph == 0, s == 0))
    def _():
        sb_ref[...] = jnp.zeros_like(sb_ref)

    @pl.when(jnp.logical_and(ph == 1, s == 0))
    def _():
        sf_ref[...] = jnp.zeros_like(sf_ref)

    @pl.when(ph == 0)
    def _():
        hist_ref[c] = sb_ref[...].astype(BF16)
        w_all = jnp.exp(cc[0:1, :] - cp)
        a_all = jnp.exp(cc[0:1, :])
        for p in range(n_pairs):
            h1, h2 = SSM_HPG + 2 * p, SSM_HPG + 2 * p + 1
            sl = slice(p * LANES, (p + 1) * LANES)
            wb = pair_cols(w_all, h1, h2)
            ab = jnp.where(left_row, a_all[:, h1:h1 + 1], a_all[:, h2:h2 + 1])
            vw = (xs[:, sl].astype(F32) * wb).astype(BF16)
            sb_ref[:, sl] = ab * sb_ref[:, sl] + _dot_tn(k, vw)

    @pl.when(ph == 1)
    def _():
        q = q_ref[...]
        cct = cct_ref[...]
        cpt = cpt_ref[...]
        dtt = dtt_ref[...]
        scores = _dot_nt(q, k)
        ii = lax.broadcasted_iota(jnp.int32, (CHUNK, CHUNK), 0)
        jj = lax.broadcasted_iota(jnp.int32, (CHUNK, CHUNK), 1)
        lower = jj <= ii
        diag = jj == ii
        yf = _dot(q, sf_ref[...].astype(BF16))
        yb = _dot(q, hist_ref[c])
        e_all = jnp.exp(cc)
        w_all = jnp.exp(cc[CHUNK - 1:CHUNK, :] - cp)
        a_all = jnp.exp(cc[CHUNK - 1:CHUNK, :])
        ys = []
        for p in range(n_pairs):
            sl = slice(p * LANES, (p + 1) * LANES)
            vp = xs[:, sl]
            probs = []
            for h in (2 * p, 2 * p + 1):
                hb = SSM_HPG + h
                arg = jnp.where(lower, cc[:, h:h + 1] - cpt[h:h + 1, :], cc[:, hb:hb + 1] - cpt[hb:hb + 1, :])
                e = jnp.exp(arg) + jnp.where(diag, dtt[hb:hb + 1, :], 0.0)
                probs.append((scores * e).astype(BF16))
            zero = jnp.zeros_like(vp)
            rhs = jnp.concatenate([jnp.where(left, vp, zero), jnp.where(left, zero, vp)], axis=0)
            y = _dot(jnp.concatenate(probs, axis=1), rhs)
            h1, h2 = 2 * p, 2 * p + 1
            y = y + pair_cols(e_all, h1, h2) * yf[:, sl] + pair_cols(e_all, SSM_HPG + h1, SSM_HPG + h2) * yb[:, sl]
            ys.append(y)
            wf = pair_cols(w_all, h1, h2)
            af = jnp.where(left_row, a_all[:, h1:h1 + 1], a_all[:, h2:h2 + 1])
            vw = (vp.astype(F32) * wf).astype(BF16)
            sf_ref[:, sl] = af * sf_ref[:, sl] + _dot_tn(k, vw)
        y = jnp.concatenate(ys, axis=1)
        y = y + dskip_ref[...] * xs.astype(F32)
        y = y * _silu(z_ref[...].astype(F32))
        y = y * lax.rsqrt(jnp.mean(y * y, axis=-1, keepdims=True) + EPS)
        o_ref[...] = (y * nw_ref[...]).astype(o_ref.dtype)


def _ssd_scan(geo, proj, xbc, cc, cp, cct, cpt, dtt, dskip, norm_w):
    gw = SSM_GROUP_W

    def blk(b, c):
        return geo.chunk_block(b, c)

    def cur(b, g, ph, s):
        return blk(b, jnp.where(ph == 0, geo.backward_order(s), s))

    def out_blk(b, g, ph, s):
        return blk(b, jnp.where(ph == 0, 0, s))

    row_spec = pl.BlockSpec((None, 16, CHUNK), lambda b, g, ph, s: (cur(b, g, ph, s), g, 0))
    col_spec = pl.BlockSpec((CHUNK, LANES), lambda b, g, ph, s: (cur(b, g, ph, s), 0))
    return pl.pallas_call(
        functools.partial(_ssd_kernel, geo=geo),
        grid=(geo.batch, SSM_GROUPS, 2, geo.nc),
        in_specs=[
            pl.BlockSpec((CHUNK, SSM_STATE), lambda b, g, ph, s: (cur(b, g, ph, s), (SSM_INNER + 1024) // SSM_STATE + g)),
            pl.BlockSpec((CHUNK, SSM_STATE), lambda b, g, ph, s: (cur(b, g, ph, s), SSM_INNER // SSM_STATE + g)),
            pl.BlockSpec((CHUNK, gw), lambda b, g, ph, s: (cur(b, g, ph, s), g)),
            pl.BlockSpec((CHUNK, gw), lambda b, g, ph, s: (cur(b, g, ph, s), COL_Z // gw + g)),
            col_spec, col_spec, row_spec, row_spec, row_spec,
            pl.BlockSpec((1, gw), lambda b, g, ph, s: (0, g)),
            pl.BlockSpec((1, gw), lambda b, g, ph, s: (0, g)),
        ],
        out_specs=pl.BlockSpec((CHUNK, gw), lambda b, g, ph, s: (out_blk(b, g, ph, s), g)),
        out_shape=jax.ShapeDtypeStruct((geo.rows, SSM_INNER), BF16),
        scratch_shapes=[
            pltpu.VMEM((SSM_STATE, gw), F32),
            pltpu.VMEM((SSM_STATE, gw), F32),
            pltpu.VMEM((geo.nc, SSM_STATE, gw), BF16),
        ],
        compiler_params=_params(4),
        name="ssd_scan",
    )(xbc, xbc, xbc, proj, cc, cp, cct, cpt, dtt, dskip, norm_w)


def _ret_kernel(dec_ref, q_ref, k_ref, v_ref, g_ref, gnw_ref, o_ref,
                sf_ref, sb_ref, hist_ref, m_ref, ey_ref, wk_ref, a_ref, *, geo):
    h, ph, s = pl.program_id(1), pl.program_id(2), pl.program_id(3)
    c = jnp.where(ph == 0, geo.backward_order(s), s)
    n = float(CHUNK)

    @pl.when(jnp.logical_and(ph == 0, s == 0))
    def _():
        lam_f = jnp.exp(jnp.full((CHUNK, 1), dec_ref[0, h], F32))
        lam_b = jnp.exp(jnp.full((CHUNK, 1), dec_ref[1, h], F32))
        ii = lax.broadcasted_iota(jnp.int32, (CHUNK, CHUNK), 0)
        jj = lax.broadcasted_iota(jnp.int32, (CHUNK, CHUNK), 1)
        dist = (ii - jj).astype(F32)
        m_ref[...] = (jnp.where(jj <= ii, jnp.exp(-lam_f * dist), 0.0)
                      + jnp.where(jj >= ii, jnp.exp(lam_b * dist), 0.0))
        row = lax.broadcasted_iota(jnp.int32, (CHUNK, 1), 0).astype(F32)
        ey_ref[0] = jnp.broadcast_to(jnp.exp(-lam_f * (row + 1.0)), (CHUNK, RET_DV))
        ey_ref[1] = jnp.broadcast_to(jnp.exp(-lam_b * (n - row)), (CHUNK, RET_DV))
        wk_ref[0] = jnp.broadcast_to(jnp.exp(-lam_f * (n - 1.0 - row)), (CHUNK, RET_DK))
        wk_ref[1] = jnp.broadcast_to(jnp.exp(-lam_b * row), (CHUNK, RET_DK))
        a_ref[0] = jnp.broadcast_to(jnp.exp(-lam_f[0:8, :] * n), (8, RET_DV))
        a_ref[1] = jnp.broadcast_to(jnp.exp(-lam_b[0:8, :] * n), (8, RET_DV))
        sb_ref[...] = jnp.zeros_like(sb_ref)

    @pl.when(jnp.logical_and(ph == 1, s == 0))
    def _():
        sf_ref[...] = jnp.zeros_like(sf_ref)

    k = k_ref[...]
    v = v_ref[...]

    @pl.when(ph == 0)
    def _():
        hist_ref[c] = sb_ref[...].astype(BF16)
        kw = (k.astype(F32) * wk_ref[1]).astype(BF16)
        sb_ref[...] = a_ref[1][0:1, :] * sb_ref[...] + _dot_tn(kw, v)

    @pl.when(ph == 1)
    def _():
        q = q_ref[...]
        probs = (_dot_nt(q, k) * m_ref[...]).astype(BF16)
        y = _dot(probs, v)
        y = y + ey_ref[0] * _dot(q, sf_ref[...].astype(BF16)) + ey_ref[1] * _dot(q, hist_ref[c])
        kw = (k.astype(F32) * wk_ref[0]).astype(BF16)
        sf_ref[...] = a_ref[0][0:1, :] * sf_ref[...] + _dot_tn(kw, v)
        mu = jnp.mean(y, axis=-1, keepdims=True)
        yc = y - mu
        var = jnp.mean(yc * yc, axis=-1, keepdims=True)
        yn = yc * lax.rsqrt(var + EPS)
        o_ref[...] = (yn * gnw_ref[...] * _silu(g_ref[...].astype(F32))).astype(o_ref.dtype)


def _ret_scan(geo, proj, decays, gn_w):
    def cur(b, h, ph, s):
        return geo.chunk_block(b, jnp.where(ph == 0, geo.backward_order(s), s))

    def out_blk(b, h, ph, s):
        return geo.chunk_block(b, jnp.where(ph == 0, 0, s))

    return pl.pallas_call(
        functools.partial(_ret_kernel, geo=geo),
        grid=(geo.batch, RET_HEADS, 2, geo.nc),
        in_specs=[
            pl.BlockSpec(memory_space=pltpu.SMEM),
            pl.BlockSpec((CHUNK, RET_DK), lambda b, h, ph, s: (cur(b, h, ph, s), COL_Q // RET_DK + h)),
            pl.BlockSpec((CHUNK, RET_DK), lambda b, h, ph, s: (cur(b, h, ph, s), COL_K // RET_DK + h)),
            pl.BlockSpec((CHUNK, RET_DV), lambda b, h, ph, s: (cur(b, h, ph, s), COL_V // RET_DV + h)),
            pl.BlockSpec((CHUNK, RET_DV), lambda b, h, ph, s: (cur(b, h, ph, s), COL_G // RET_DV + h)),
            pl.BlockSpec((1, RET_DV), lambda b, h, ph, s: (0, h)),
        ],
        out_specs=pl.BlockSpec((CHUNK, RET_DV), lambda b, h, ph, s: (out_blk(b, h, ph, s), h)),
        out_shape=jax.ShapeDtypeStruct((geo.rows, RET_HEADS * RET_DV), BF16),
        scratch_shapes=[
            pltpu.VMEM((RET_DK, RET_DV), F32),
            pltpu.VMEM((RET_DK, RET_DV), F32),
            pltpu.VMEM((geo.nc, RET_DK, RET_DV), BF16),
            pltpu.VMEM((CHUNK, CHUNK), F32),
            pltpu.VMEM((2, CHUNK, RET_DV), F32),
            pltpu.VMEM((2, CHUNK, RET_DK), F32),
            pltpu.VMEM((2, 8, RET_DV), F32),
        ],
        compiler_params=_params(4),
        name="ret_scan",
    )(decays, proj, proj, proj, proj, gn_w)


def _lanes2(a):
    return jnp.concatenate([a, a], axis=1)


def _ret_tables(dec_ref, direction, rows, wk_ref, a_ref, e_ref=None, m_ref=None):
    n = float(rows)
    row = lax.broadcasted_iota(jnp.int32, (rows, LANES), 0).astype(F32)
    for h in range(RET_HEADS):
        lam = jnp.exp(jnp.full((rows, LANES), dec_ref[direction, h], F32))
        if direction == 0:
            wk_ref[h] = jnp.exp(-lam * (n - 1.0 - row))
            if e_ref is not None:
                e_ref[0, h] = jnp.exp(-lam * (row + 1.0))
        else:
            wk_ref[h] = jnp.exp(-lam * row)
        a_ref[h] = _lanes2(jnp.exp(-lam[0:8, :] * n))


def _ret_state_kernel(dec_ref, k_ref, v_ref, hist_ref, sb_ref, wk_ref, a_ref, *, rows):
    @pl.when(pl.program_id(1) == 0)
    def _():
        sb_ref[...] = jnp.zeros_like(sb_ref)
        _ret_tables(dec_ref, 1, rows, wk_ref, a_ref)

    def body(h, carry):
        kh = k_ref[:, pl.ds(pl.multiple_of(h * RET_DK, RET_DK), RET_DK)]
        vh = v_ref[:, pl.ds(pl.multiple_of(h * RET_DV, RET_DV), RET_DV)]
        hist_ref[h] = sb_ref[h].astype(BF16)
        kw = (kh.astype(F32) * wk_ref[h]).astype(BF16)
        sb_ref[h] = a_ref[h][0:1, :] * sb_ref[h] + _dot_tn(kw, vh)
        return carry

    lax.fori_loop(0, RET_HEADS, body, 0)


def _ret_out_kernel(dec_ref, q_ref, k_ref, v_ref, g_ref, hist_ref, gnw_ref, o_ref,
                    sf_ref, wk_ref, a_ref, e_ref, m_ref, *, rows):
    @pl.when(pl.program_id(1) == 0)
    def _():
        sf_ref[...] = jnp.zeros_like(sf_ref)
        _ret_tables(dec_ref, 0, rows, wk_ref, a_ref, e_ref)
        n = float(rows)
        row = lax.broadcasted_iota(jnp.int32, (rows, LANES), 0).astype(F32)
        ii = lax.broadcasted_iota(jnp.int32, (rows, rows), 0)
        jj = lax.broadcasted_iota(jnp.int32, (rows, rows), 1)
        dist = (ii - jj).astype(F32)
        for h in range(RET_HEADS):
            lam_f = jnp.exp(jnp.full((rows, 1), dec_ref[0, h], F32))
            lam_b = jnp.exp(jnp.full((rows, 1), dec_ref[1, h], F32))
            m_ref[h] = (jnp.where(jj <= ii, jnp.exp(-lam_f * dist), 0.0)
                        + jnp.where(jj >= ii, jnp.exp(lam_b * dist), 0.0))
            e_ref[1, h] = jnp.exp(-jnp.exp(jnp.full((rows, LANES), dec_ref[1, h], F32)) * (n - row))

    def body(h, carry):
        ok = pl.multiple_of(h * RET_DK, RET_DK)
        ov = pl.multiple_of(h * RET_DV, RET_DV)
        qh = q_ref[:, pl.ds(ok, RET_DK)]
        kh = k_ref[:, pl.ds(ok, RET_DK)]
        vh = v_ref[:, pl.ds(ov, RET_DV)]
        probs = (_dot_nt(qh, kh) * m_ref[h]).astype(BF16)
        y = _dot(probs, vh)
        y = y + _lanes2(e_ref[0, h]) * _dot(qh, sf_ref[h].astype(BF16)) + _lanes2(e_ref[1, h]) * _dot(qh, hist_ref[h])
        kw = (kh.astype(F32) * wk_ref[h]).astype(BF16)
        sf_ref[h] = a_ref[h][0:1, :] * sf_ref[h] + _dot_tn(kw, vh)
        mu = jnp.mean(y, axis=-1, keepdims=True)
        yc = y - mu
        yn = yc * lax.rsqrt(jnp.mean(yc * yc, axis=-1, keepdims=True) + EPS)
        gate = _silu(g_ref[:, pl.ds(ov, RET_DV)].astype(F32))
        o_ref[:, pl.ds(ov, RET_DV)] = (yn * gnw_ref[:, pl.ds(ov, RET_DV)] * gate).astype(o_ref.dtype)
        return carry

    lax.fori_loop(0, RET_HEADS, body, 0)


def _ret_scan2(geo, proj, decays, gn_w):
    rows = geo.step_rows
    qk_w, v_w = RET_HEADS * RET_DK, RET_HEADS * RET_DV

    def bwd(b, s):
        return geo.step_block(b, geo.backward_order(s))

    def fwd(b, s):
        return geo.step_block(b, s)

    smem = pl.BlockSpec(memory_space=pltpu.SMEM)
    hist_shape = (geo.batch, geo.ns, RET_HEADS, RET_DK, RET_DV)
    hist = pl.pallas_call(
        functools.partial(_ret_state_kernel, rows=rows),
        grid=(geo.batch, geo.ns),
        in_specs=[
            smem,
            pl.BlockSpec((rows, qk_w), lambda b, s: (bwd(b, s), COL_K // qk_w)),
            pl.BlockSpec((rows, v_w), lambda b, s: (bwd(b, s), COL_V // v_w)),
        ],
        out_specs=pl.BlockSpec((None, None) + hist_shape[2:], lambda b, s: (b, geo.backward_order(s), 0, 0, 0)),
        out_shape=jax.ShapeDtypeStruct(hist_shape, BF16),
        scratch_shapes=[
            pltpu.VMEM((RET_HEADS, RET_DK, RET_DV), F32),
            pltpu.VMEM((RET_HEADS, rows, LANES), F32),
            pltpu.VMEM((RET_HEADS, 8, RET_DV), F32),
        ],
        compiler_params=_params(2),
        name="ret_state",
    )(decays, proj, proj)
    return pl.pallas_call(
        functools.partial(_ret_out_kernel, rows=rows),
        grid=(geo.batch, geo.ns),
        in_specs=[
            smem,
            pl.BlockSpec((rows, qk_w), lambda b, s: (fwd(b, s), COL_Q // qk_w)),
            pl.BlockSpec((rows, qk_w), lambda b, s: (fwd(b, s), COL_K // qk_w)),
            pl.BlockSpec((rows, v_w), lambda b, s: (fwd(b, s), COL_V // v_w)),
            pl.BlockSpec((rows, v_w), lambda b, s: (fwd(b, s), COL_G // v_w)),
            pl.BlockSpec((None, None) + hist_shape[2:], lambda b, s: (b, s, 0, 0, 0)),
            pl.BlockSpec((1, v_w), lambda b, s: (0, 0)),
        ],
        out_specs=pl.BlockSpec((rows, v_w), lambda b, s: (fwd(b, s), 0)),
        out_shape=jax.ShapeDtypeStruct((geo.rows, v_w), BF16),
        scratch_shapes=[
            pltpu.VMEM((RET_HEADS, RET_DK, RET_DV), F32),
            pltpu.VMEM((RET_HEADS, rows, LANES), F32),
            pltpu.VMEM((RET_HEADS, 8, RET_DV), F32),
            pltpu.VMEM((2, RET_HEADS, rows, LANES), F32),
            pltpu.VMEM((RET_HEADS, rows, rows), F32),
        ],
        compiler_params=_params(2),
        name="ret_out",
    )(decays, proj, proj, proj, proj, hist, gn_w)


def _group_cols(ref, rows, g):
    return pltpu.roll(ref[rows, :], (LANES - 16 * g) % LANES, 1)


def _pair_cols(left, a, h1, h2):
    return jnp.where(left, a[:, h1:h1 + 1], a[:, h2:h2 + 1])


def _ssd_state_kernel(k_ref, xs_ref, cc_ref, cp_ref, hist_ref, sb_ref, *, cps):
    @pl.when(pl.program_id(1) == 0)
    def _():
        sb_ref[...] = jnp.zeros_like(sb_ref)

    lane = lax.broadcasted_iota(jnp.int32, (CHUNK, LANES), 1)
    left = lane < 64
    left_row = lax.broadcasted_iota(jnp.int32, (1, LANES), 1) < 64

    def body(g, carry):
        ok = pl.multiple_of(g * SSM_STATE, SSM_STATE)
        ov = pl.multiple_of(g * SSM_GROUP_W, SSM_GROUP_W)
        for ci in reversed(range(cps)):
            rows = pl.ds(ci * CHUNK, CHUNK)
            cc = _group_cols(cc_ref, rows, g)
            cp = _group_cols(cp_ref, rows, g)
            k = k_ref[rows, pl.ds(ok, SSM_STATE)]
            hist_ref[ci, g] = sb_ref[g].astype(BF16)
            w_all = jnp.exp(cc[0:1, :] - cp)
            a_all = jnp.exp(cc[0:1, :])
            for p in range(SSM_HPG // 2):
                h1, h2 = SSM_HPG + 2 * p, SSM_HPG + 2 * p + 1
                sl = slice(p * LANES, (p + 1) * LANES)
                vp = xs_ref[rows, pl.ds(pl.multiple_of(ov + p * LANES, LANES), LANES)]
                vw = (vp.astype(F32) * _pair_cols(left, w_all, h1, h2)).astype(BF16)
                ab = jnp.where(left_row, a_all[:, h1:h1 + 1], a_all[:, h2:h2 + 1])
                sb_ref[g, :, sl] = ab * sb_ref[g, :, sl] + _dot_tn(k, vw)
        return carry

    lax.fori_loop(0, SSM_GROUPS, body, 0)


def _ssd_out_kernel(q_ref, k_ref, xs_ref, z_ref, cc_ref, cp_ref, cpt_ref, dtt_ref, hist_ref,
                    dskip_ref, nw_ref, o_ref, sf_ref, *, cps):
    @pl.when(pl.program_id(1) == 0)
    def _():
        sf_ref[...] = jnp.zeros_like(sf_ref)

    lane = lax.broadcasted_iota(jnp.int32, (CHUNK, LANES), 1)
    left = lane < 64
    left_row = lax.broadcasted_iota(jnp.int32, (1, LANES), 1) < 64
    ii = lax.broadcasted_iota(jnp.int32, (CHUNK, CHUNK), 0)
    jj = lax.broadcasted_iota(jnp.int32, (CHUNK, CHUNK), 1)
    lower = jj <= ii
    diag = jj == ii

    def body(g, carry):
        ok = pl.multiple_of(g * SSM_STATE, SSM_STATE)
        ov = pl.multiple_of(g * SSM_GROUP_W, SSM_GROUP_W)
        og = pl.multiple_of(g * 16, 16)
        for ci in range(cps):
            rows = pl.ds(ci * CHUNK, CHUNK)
            cc = _group_cols(cc_ref, rows, g)
            cp = _group_cols(cp_ref, rows, g)
            cpt = cpt_ref[ci, pl.ds(og, 16), :]
            dtt = dtt_ref[ci, pl.ds(og, 16), :]
            q = q_ref[rows, pl.ds(ok, SSM_STATE)]
            k = k_ref[rows, pl.ds(ok, SSM_STATE)]
            scores = _dot_nt(q, k)
            yf = _dot(q, sf_ref[g].astype(BF16))
            yb = _dot(q, hist_ref[ci, g])
            w_all = jnp.exp(cc[CHUNK - 1:CHUNK, :] - cp)
            a_all = jnp.exp(cc[CHUNK - 1:CHUNK, :])
            ys = []
            for p in range(SSM_HPG // 2):
                h1, h2 = 2 * p, 2 * p + 1
                sl = slice(p * LANES, (p + 1) * LANES)
                cols = pl.ds(pl.multiple_of(ov + p * LANES, LANES), LANES)
                vp = xs_ref[rows, cols]
                probs, ef, eb = [], [], []
                for h in (h1, h2):
                    hb = SSM_HPG + h
                    cf = jnp.broadcast_to(cc[:, h:h + 1], (CHUNK, CHUNK))
                    cb = jnp.broadcast_to(cc[:, hb:hb + 1], (CHUNK, CHUNK))
                    arg = jnp.where(lower, cf - cpt[h:h + 1, :], cb - cpt[hb:hb + 1, :])
                    e = jnp.exp(arg) + jnp.where(diag, dtt[hb:hb + 1, :], 0.0)
                    probs.append((scores * e).astype(BF16))
                    ef.append(jnp.exp(cf))
                    eb.append(jnp.exp(cb))
                zero = jnp.zeros_like(vp)
                rhs = jnp.concatenate([jnp.where(left, vp, zero), jnp.where(left, zero, vp)], axis=0)
                y = _dot(jnp.concatenate(probs, axis=1), rhs)
                y = (y + jnp.where(left, ef[0], ef[1]) * yf[:, sl] + jnp.where(left, eb[0], eb[1]) * yb[:, sl])
                vw = (vp.astype(F32) * _pair_cols(left, w_all, h1, h2)).astype(BF16)
                af = jnp.where(left_row, a_all[:, h1:h1 + 1], a_all[:, h2:h2 + 1])
                sf_ref[g, :, sl] = af * sf_ref[g, :, sl] + _dot_tn(k, vw)
                ys.append(y + dskip_ref[:, cols] * vp.astype(F32))
            gcols = pl.ds(ov, SSM_GROUP_W)
            y = jnp.concatenate(ys, axis=1) * _silu(z_ref[rows, gcols].astype(F32))
            y = y * lax.rsqrt(jnp.mean(y * y, axis=-1, keepdims=True) + EPS)
            o_ref[rows, gcols] = (y * nw_ref[:, gcols]).astype(o_ref.dtype)
        return carry

    lax.fori_loop(0, SSM_GROUPS, body, 0)


def _ssd_scan2(geo, proj, xbc, cc, cp, cpt, dtt, dskip, norm_w):
    rows = geo.step_rows
    cps = rows // CHUNK
    bc_w = SSM_GROUPS * SSM_STATE

    def bwd(b, s):
        return geo.step_block(b, geo.backward_order(s))

    def fwd(b, s):
        return geo.step_block(b, s)

    hist_shape = (geo.batch, geo.ns * cps, SSM_GROUPS, SSM_STATE, SSM_GROUP_W)
    hist_block = (None, cps) + hist_shape[2:]
    hist = pl.pallas_call(
        functools.partial(_ssd_state_kernel, cps=cps),
        grid=(geo.batch, geo.ns),
        in_specs=[
            pl.BlockSpec((rows, bc_w), lambda b, s: (bwd(b, s), SSM_INNER // bc_w)),
            pl.BlockSpec((rows, SSM_INNER), lambda b, s: (bwd(b, s), 0)),
            pl.BlockSpec((rows, LANES), lambda b, s: (bwd(b, s), 0)),
            pl.BlockSpec((rows, LANES), lambda b, s: (bwd(b, s), 0)),
        ],
        out_specs=pl.BlockSpec(hist_block, lambda b, s: (b, geo.backward_order(s), 0, 0, 0)),
        out_shape=jax.ShapeDtypeStruct(hist_shape, BF16),
        scratch_shapes=[pltpu.VMEM((SSM_GROUPS, SSM_STATE, SSM_GROUP_W), F32)],
        compiler_params=_params(2),
        name="ssd_state",
    )(xbc, xbc, cc, cp)
    col = pl.BlockSpec((rows, LANES), lambda b, s: (fwd(b, s), 0))
    row = pl.BlockSpec((cps, LANES, CHUNK), lambda b, s: (fwd(b, s), 0, 0))
    return pl.pallas_call(
        functools.partial(_ssd_out_kernel, cps=cps),
        grid=(geo.batch, geo.ns),
        in_specs=[
            pl.BlockSpec((rows, bc_w), lambda b, s: (fwd(b, s), SSM_INNER // bc_w + 1)),
            pl.BlockSpec((rows, bc_w), lambda b, s: (fwd(b, s), SSM_INNER // bc_w)),
            pl.BlockSpec((rows, SSM_INNER), lambda b, s: (fwd(b, s), 0)),
            pl.BlockSpec((rows, SSM_INNER), lambda b, s: (fwd(b, s), COL_Z // SSM_INNER)),
            col, col, row, row,
            pl.BlockSpec(hist_block, lambda b, s: (b, s, 0, 0, 0)),
            pl.BlockSpec((1, SSM_INNER), lambda b, s: (0, 0)),
            pl.BlockSpec((1, SSM_INNER), lambda b, s: (0, 0)),
        ],
        out_specs=pl.BlockSpec((rows, SSM_INNER), lambda b, s: (fwd(b, s), 0)),
        out_shape=jax.ShapeDtypeStruct((geo.rows, SSM_INNER), BF16),
        scratch_shapes=[pltpu.VMEM((SSM_GROUPS, SSM_STATE, SSM_GROUP_W), F32)],
        compiler_params=_params(2),
        name="ssd_out",
    )(xbc, xbc, xbc, proj, cc, cp, cpt, dtt, hist, dskip, norm_w)


def _merge_kernel(yr_ref, ys_ref, wr_ref, ws_ref, gr_ref, gs_ref, o_ref):
    r = _dot(yr_ref[...], wr_ref[...])
    s = _dot(ys_ref[...], ws_ref[...])
    m = jax.nn.sigmoid(gr_ref[...].astype(F32)) * r + jax.nn.sigmoid(gs_ref[...].astype(F32)) * s
    o_ref[...] = m.astype(o_ref.dtype)


def _merge(geo, yr, ys, w_ret, w_ssm, proj):
    tm, tn = geo.tm, 512
    gr0 = COL_GATES // tn
    gs0 = (COL_GATES + D_MODEL) // tn
    return pl.pallas_call(
        _merge_kernel,
        grid=(geo.n_tiles, D_MODEL // tn),
        in_specs=[
            pl.BlockSpec((tm, yr.shape[1]), lambda i, j: (i, 0)),
            pl.BlockSpec((tm, ys.shape[1]), lambda i, j: (i, 0)),
            pl.BlockSpec((yr.shape[1], tn), lambda i, j: (0, j)),
            pl.BlockSpec((ys.shape[1], tn), lambda i, j: (0, j)),
            pl.BlockSpec((tm, tn), lambda i, j: (i, gr0 + j)),
            pl.BlockSpec((tm, tn), lambda i, j: (i, gs0 + j)),
        ],
        out_specs=pl.BlockSpec((tm, tn), lambda i, j: (i, j)),
        out_shape=jax.ShapeDtypeStruct((geo.rows, D_MODEL), BF16),
        compiler_params=_params(2),
        name="branch_merge",
    )(yr, ys, w_ret, w_ssm, proj, proj)


def _residual_kernel(a_ref, w_ref, x_ref, mod_ref, o_ref, *, gate_row):
    o_ref[...] = x_ref[...] + mod_ref[gate_row:gate_row + 1, :] * _dot(a_ref[...], w_ref[...])


def _matmul_residual(geo, a, w, xs, mod_l, gate_row, tile0, n_tiles, tn):
    tm = geo.tm
    kdim = a.shape[1]
    return pl.pallas_call(
        functools.partial(_residual_kernel, gate_row=gate_row),
        grid=(n_tiles, D_MODEL // tn),
        in_specs=[
            pl.BlockSpec((tm, kdim), lambda i, j: (i, 0)),
            pl.BlockSpec((kdim, tn), lambda i, j: (0, j)),
            pl.BlockSpec((tm, tn), lambda i, j: (tile0 + i, j)),
            pl.BlockSpec((None, 6, tn), lambda i, j: (geo.mod_row(tile0 + i), 0, j)),
        ],
        out_specs=pl.BlockSpec((tm, tn), lambda i, j: (i, j)),
        out_shape=jax.ShapeDtypeStruct((n_tiles * tm, D_MODEL), F32),
        compiler_params=_params(2),
        name="proj_residual",
    )(a, w, xs, mod_l)


def _ffn_up_kernel(x_ref, mod_ref, nw_ref, wg_ref, wu_ref, o_ref, h_ref):
    @pl.when(pl.program_id(1) == 0)
    def _():
        h_ref[...] = _norm_mod(x_ref[...], nw_ref[...], mod_ref[3:4, :], mod_ref[4:5, :]).astype(BF16)

    h = h_ref[...]
    o_ref[...] = (_silu(_dot(h, wg_ref[...])) * _dot(h, wu_ref[...])).astype(o_ref.dtype)


def _ffn_up(geo, xs, mod_l, nw, w_gate, w_up):
    tm, tf = geo.tm, 512
    return pl.pallas_call(
        _ffn_up_kernel,
        grid=(geo.n_tiles, D_FF // tf),
        in_specs=[
            pl.BlockSpec((tm, D_MODEL), lambda i, j: (i, 0)),
            pl.BlockSpec((None, 6, D_MODEL), lambda i, j: (geo.mod_row(i), 0, 0)),
            pl.BlockSpec((1, D_MODEL), lambda i, j: (0, 0)),
            pl.BlockSpec((D_MODEL, tf), lambda i, j: (0, j)),
            pl.BlockSpec((D_MODEL, tf), lambda i, j: (0, j)),
        ],
        out_specs=pl.BlockSpec((tm, tf), lambda i, j: (i, j)),
        out_shape=jax.ShapeDtypeStruct((geo.rows, D_FF), BF16),
        scratch_shapes=[pltpu.VMEM((tm, D_MODEL), BF16)],
        compiler_params=_params(2),
        name="ffn_up",
    )(xs, mod_l, nw, w_gate, w_up)


def _moe_kernel(x_ref, mod_ref, nw_ref, router_ref, wg_ref, wu_ref, wd_ref, o_ref, h_ref, comb_ref):
    e, j = pl.program_id(1), pl.program_id(2)
    tm = x_ref.shape[0]
    lane = lax.broadcasted_iota(jnp.int32, (tm, LANES), 1)

    @pl.when(jnp.logical_and(e == 0, j == 0))
    def _():
        h = _norm_mod(x_ref[...], nw_ref[...], mod_ref[3:4, :], mod_ref[4:5, :])
        h_hi, h_lo = _split2(h)
        h_ref[...] = h_hi
        r_hi, r_lo = _split2(router_ref[...])
        logits = _dot(h_hi, r_hi) + _dot(h_hi, r_lo) + _dot(h_lo, r_hi)
        neg = -jnp.inf
        lg = jnp.where(lane < N_EXPERTS, logits, neg)
        m1 = jnp.max(lg, axis=-1, keepdims=True)
        i1 = jnp.min(jnp.where(lg == m1, lane, LANES), axis=-1, keepdims=True)
        lg2 = jnp.where(lane == i1, neg, lg)
        m2 = jnp.max(lg2, axis=-1, keepdims=True)
        i2 = jnp.min(jnp.where(lg2 == m2, lane, LANES), axis=-1, keepdims=True)
        e2 = jnp.exp(m2 - m1)
        w1 = 1.0 / (1.0 + e2)
        w2 = e2 / (1.0 + e2)
        comb_ref[...] = jnp.where(lane == i1, w1, 0.0) + jnp.where(lane == i2, w2, 0.0)
        o_ref[...] = jnp.zeros_like(o_ref)

    h = h_ref[...]
    ce = jnp.sum(jnp.where(lane == e, comb_ref[...], 0.0), axis=-1, keepdims=True)
    hid = (_silu(_dot(h, wg_ref[...])) * _dot(h, wu_ref[...]) * ce).astype(BF16)
    for n in range(0, D_MODEL, 512):
        o_ref[:, n:n + 512] += _dot(hid, wd_ref[:, n:n + 512])

    @pl.when(jnp.logical_and(e == pl.num_programs(1) - 1, j == pl.num_programs(2) - 1))
    def _():
        o_ref[...] = x_ref[...] + mod_ref[5:6, :] * o_ref[...]


def _moe(geo, xs, mod_l, nw, router, w_gate, w_up, w_down, tile0, n_tiles):
    sub = 2 if geo.tm >= 1024 else 1
    tm, tf = geo.tm // sub, 512
    tile0, n_tiles = tile0 * sub, n_tiles * sub
    return pl.pallas_call(
        _moe_kernel,
        grid=(n_tiles, N_EXPERTS, MOE_D_FF // tf),
        in_specs=[
            pl.BlockSpec((tm, D_MODEL), lambda i, e, j: (tile0 + i, 0)),
            pl.BlockSpec((None, 6, D_MODEL), lambda i, e, j: (geo.mod_row((tile0 + i) // sub), 0, 0)),
            pl.BlockSpec((1, D_MODEL), lambda i, e, j: (0, 0)),
            pl.BlockSpec((D_MODEL, LANES), lambda i, e, j: (0, 0)),
            pl.BlockSpec((None, D_MODEL, tf), lambda i, e, j: (e, 0, j)),
            pl.BlockSpec((None, D_MODEL, tf), lambda i, e, j: (e, 0, j)),
            pl.BlockSpec((None, tf, D_MODEL), lambda i, e, j: (e, j, 0)),
        ],
        out_specs=pl.BlockSpec((tm, D_MODEL), lambda i, e, j: (i, 0)),
        out_shape=jax.ShapeDtypeStruct((n_tiles * tm, D_MODEL), F32),
        scratch_shapes=[pltpu.VMEM((tm, D_MODEL), BF16), pltpu.VMEM((tm, LANES), F32)],
        compiler_params=_params(3),
        name="moe_dense",
    )(xs, mod_l, nw, router, w_gate, w_up, w_down)


MOE_ROW_TILE = 512
ROUTE_W1, ROUTE_W2, ROUTE_E1, ROUTE_E2 = 8, 9, 10, 11
DMA_ROWS = 256


def _route_kernel(x_ref, mod_ref, nw_ref, router_ref, h_ref, route_ref, cnt_ref, tri_ref, carry_ref):
    i = pl.program_id(0)
    tm = x_ref.shape[0]
    lane = lax.broadcasted_iota(jnp.int32, (tm, LANES), 1)

    @pl.when(i == 0)
    def _():
        ii = lax.broadcasted_iota(jnp.int32, (tm, tm), 0)
        jj = lax.broadcasted_iota(jnp.int32, (tm, tm), 1)
        tri_ref[...] = (jj < ii).astype(BF16)
        carry_ref[...] = jnp.zeros_like(carry_ref)

    h = _norm_mod(x_ref[...], nw_ref[...], mod_ref[3:4, :], mod_ref[4:5, :])
    h_hi, h_lo = _split2(h)
    h_ref[...] = h_hi
    r_hi, r_lo = _split2(router_ref[...])
    logits = _dot(h_hi, r_hi) + _dot(h_hi, r_lo) + _dot(h_lo, r_hi)
    neg = -jnp.inf
    lg = jnp.where(lane < N_EXPERTS, logits, neg)
    m1 = jnp.max(lg, axis=-1, keepdims=True)
    i1 = jnp.min(jnp.where(lg == m1, lane, LANES), axis=-1, keepdims=True)
    lg2 = jnp.where(lane == i1, neg, lg)
    m2 = jnp.max(lg2, axis=-1, keepdims=True)
    i2 = jnp.min(jnp.where(lg2 == m2, lane, LANES), axis=-1, keepdims=True)
    e2 = jnp.exp(m2 - m1)
    w1 = 1.0 / (1.0 + e2)
    w2 = e2 / (1.0 + e2)
    chosen = jnp.logical_or(lane == i1, lane == i2)
    mask = jnp.where(chosen, 1.0, 0.0)
    pos = _dot(tri_ref[...], mask.astype(BF16)) + carry_ref[0:1, :]
    carry_ref[0:1, :] = carry_ref[0:1, :] + jnp.sum(mask, axis=0, keepdims=True)
    rec = jnp.where(lane < N_EXPERTS, pos, 0.0)
    rec = jnp.where(lane == ROUTE_W1, w1, rec)
    rec = jnp.where(lane == ROUTE_W2, w2, rec)
    rec = jnp.where(lane == ROUTE_E1, i1.astype(F32), rec)
    rec = jnp.where(lane == ROUTE_E2, i2.astype(F32), rec)
    route_ref[...] = rec
    cnt_ref[...] = carry_ref[...]


def _slots_kernel(route_ref, start_ref, o_ref):
    rec = route_ref[...]
    lane = lax.broadcasted_iota(jnp.int32, rec.shape, 1)
    slot = rec + start_ref[...]
    e1 = rec[:, ROUTE_E1:ROUTE_E1 + 1].astype(jnp.int32)
    e2 = rec[:, ROUTE_E2:ROUTE_E2 + 1].astype(jnp.int32)
    d1 = jnp.sum(jnp.where(lane == e1, slot, 0.0), axis=-1, keepdims=True)
    d2 = jnp.sum(jnp.where(lane == e2, slot, 0.0), axis=-1, keepdims=True)
    o_ref[...] = jnp.where(lane == 0, d1, jnp.where(lane == 1, d2, 0.0)).astype(jnp.int32)


def _row_copy(src_ref, dst_ref, src_row, dst_row, sem):
    return pltpu.make_async_copy(src_ref.at[src_row], dst_ref.at[dst_row], sem)


def _dispatch_kernel(d1_ref, d2_ref, h_ref, zeros_ref, o_ref, sem):
    del zeros_ref
    base = pl.program_id(0) * DMA_ROWS

    def issue(r, carry):
        _row_copy(h_ref, o_ref, base + r, d1_ref[r], sem).start()
        _row_copy(h_ref, o_ref, base + r, d2_ref[r], sem).start()
        return carry

    def drain(r, carry):
        _row_copy(h_ref, o_ref, 0, 0, sem).wait()
        _row_copy(h_ref, o_ref, 0, 0, sem).wait()
        return carry

    lax.fori_loop(0, DMA_ROWS, issue, 0)
    lax.fori_loop(0, DMA_ROWS, drain, 0)


def _collect_kernel(d1_ref, d2_ref, y_ref, o_ref, sem):
    base = pl.program_id(0) * DMA_ROWS

    def issue(r, carry):
        _row_copy(y_ref, o_ref.at[0], d1_ref[r], base + r, sem).start()
        _row_copy(y_ref, o_ref.at[1], d2_ref[r], base + r, sem).start()
        return carry

    def drain(r, carry):
        _row_copy(y_ref, o_ref.at[0], 0, 0, sem).wait()
        _row_copy(y_ref, o_ref.at[0], 0, 0, sem).wait()
        return carry

    lax.fori_loop(0, DMA_ROWS, issue, 0)
    lax.fori_loop(0, DMA_ROWS, drain, 0)


def _expert_up_kernel(te_ref, nu_ref, x_ref, wg_ref, wu_ref, o_ref):
    del te_ref
    live = pl.program_id(1) < nu_ref[0]

    @pl.when(live)
    def _():
        x = x_ref[...]
        o_ref[...] = (_silu(_dot(x, wg_ref[...])) * _dot(x, wu_ref[...])).astype(o_ref.dtype)

    @pl.when(jnp.logical_not(live))
    def _():
        o_ref[...] = jnp.zeros_like(o_ref)


def _expert_down_kernel(te_ref, nu_ref, h_ref, wd_ref, o_ref):
    del te_ref
    live = pl.program_id(1) < nu_ref[0]

    @pl.when(live)
    def _():
        o_ref[...] = _dot(h_ref[...], wd_ref[...]).astype(o_ref.dtype)

    @pl.when(jnp.logical_not(live))
    def _():
        o_ref[...] = jnp.zeros_like(o_ref)


def _combine_kernel(x_ref, y1_ref, y2_ref, route_ref, mod_ref, o_ref):
    rec = route_ref[...]
    w1 = rec[:, ROUTE_W1:ROUTE_W1 + 1]
    w2 = rec[:, ROUTE_W2:ROUTE_W2 + 1]
    y = w1 * y1_ref[...].astype(F32) + w2 * y2_ref[...].astype(F32)
    o_ref[...] = x_ref[...] + mod_ref[5:6, :] * y


def _moe_sparse(geo, xs, mod_l, nw, router, w_gate, w_up, w_down, tile0, n_tiles):
    tm, tg = geo.tm, MOE_ROW_TILE
    rows = n_tiles * tm
    n_slots = 2 * rows + N_EXPERTS * tg
    n_gt = n_slots // tg
    row_tile = pl.BlockSpec((tm, D_MODEL), lambda i: (i, 0))
    rec_tile = pl.BlockSpec((tm, LANES), lambda i: (i, 0))

    h2, route, counts = pl.pallas_call(
        _route_kernel,
        grid=(n_tiles,),
        in_specs=[
            pl.BlockSpec((tm, D_MODEL), lambda i: (tile0 + i, 0)),
            pl.BlockSpec((None, 6, D_MODEL), lambda i: (geo.mod_row(tile0 + i), 0, 0)),
            pl.BlockSpec((1, D_MODEL), lambda i: (0, 0)),
            pl.BlockSpec((D_MODEL, LANES), lambda i: (0, 0)),
        ],
        out_specs=[row_tile, rec_tile, pl.BlockSpec((8, LANES), lambda i: (0, 0))],
        out_shape=[
            jax.ShapeDtypeStruct((rows, D_MODEL), F32),
            jax.ShapeDtypeStruct((rows, LANES), F32),
            jax.ShapeDtypeStruct((8, LANES), F32),
        ],
        scratch_shapes=[pltpu.VMEM((tm, tm), BF16), pltpu.VMEM((8, LANES), F32)],
        compiler_params=_params(1),
        name="moe_route",
    )(xs, mod_l, nw, router)

    cnt = counts[0, :N_EXPERTS].astype(jnp.int32)
    padded = ((cnt + tg - 1) // tg) * tg
    ends = jnp.cumsum(padded)
    starts = ends - padded
    n_used = (ends[-1] // tg).astype(jnp.int32).reshape(1)
    tile_expert = jnp.searchsorted(ends, jnp.minimum(jnp.arange(n_gt), n_used[0] - 1) * tg, side="right")
    tile_expert = jnp.minimum(tile_expert, N_EXPERTS - 1).astype(jnp.int32)
    start_row = jnp.zeros((1, LANES), F32).at[0, :N_EXPERTS].set(starts.astype(F32))

    dest = pl.pallas_call(
        _slots_kernel,
        grid=(n_tiles,),
        in_specs=[rec_tile, pl.BlockSpec((1, LANES), lambda i: (0, 0))],
        out_specs=rec_tile,
        out_shape=jax.ShapeDtypeStruct((rows, LANES), jnp.int32),
        compiler_params=_params(1),
        name="moe_slots",
    )(route, start_row)
    d1, d2 = dest[:, 0], dest[:, 1]

    idx_spec = pl.BlockSpec((DMA_ROWS,), lambda i: (i,), memory_space=pltpu.SMEM)
    any_spec = pl.BlockSpec(memory_space=pl.ANY)
    xg = pl.pallas_call(
        _dispatch_kernel,
        grid=(rows // DMA_ROWS,),
        in_specs=[idx_spec, idx_spec, any_spec, any_spec],
        out_specs=any_spec,
        out_shape=jax.ShapeDtypeStruct((n_slots, 1, D_MODEL), F32),
        scratch_shapes=[pltpu.SemaphoreType.DMA(())],
        input_output_aliases={3: 0},
        compiler_params=_params(1),
        name="moe_dispatch",
    )(d1, d2, h2.reshape(rows, 1, D_MODEL), jnp.zeros((n_slots, 1, D_MODEL), F32))
    xg = xg.reshape(n_slots, D_MODEL)

    tf = 1024
    hg = pl.pallas_call(
        _expert_up_kernel,
        grid_spec=pltpu.PrefetchScalarGridSpec(
            num_scalar_prefetch=2,
            grid=(MOE_D_FF // tf, n_gt),
            in_specs=[
                pl.BlockSpec((tg, D_MODEL), lambda j, r, te, nu: (r, 0)),
                pl.BlockSpec((None, D_MODEL, tf), lambda j, r, te, nu: (te[r], 0, j)),
                pl.BlockSpec((None, D_MODEL, tf), lambda j, r, te, nu: (te[r], 0, j)),
            ],
            out_specs=pl.BlockSpec((tg, tf), lambda j, r, te, nu: (r, j)),
        ),
        out_shape=jax.ShapeDtypeStruct((n_slots, MOE_D_FF), BF16),
        compiler_params=_params(2),
        name="moe_expert_up",
    )(tile_expert, n_used, xg, w_gate, w_up)

    tn = 1024
    yg = pl.pallas_call(
        _expert_down_kernel,
        grid_spec=pltpu.PrefetchScalarGridSpec(
            num_scalar_prefetch=2,
            grid=(D_MODEL // tn, n_gt),
            in_specs=[
                pl.BlockSpec((tg, MOE_D_FF), lambda j, r, te, nu: (r, 0)),
                pl.BlockSpec((None, MOE_D_FF, tn), lambda j, r, te, nu: (te[r], 0, j)),
            ],
            out_specs=pl.BlockSpec((tg, tn), lambda j, r, te, nu: (r, j)),
        ),
        out_shape=jax.ShapeDtypeStruct((n_slots, D_MODEL), F32),
        compiler_params=_params(2),
        name="moe_expert_down",
    )(tile_expert, n_used, hg, w_down)

    y12 = pl.pallas_call(
        _collect_kernel,
        grid=(rows // DMA_ROWS,),
        in_specs=[idx_spec, idx_spec, any_spec],
        out_specs=any_spec,
        out_shape=jax.ShapeDtypeStruct((2, rows, 1, D_MODEL), F32),
        scratch_shapes=[pltpu.SemaphoreType.DMA(())],
        compiler_params=_params(1),
        name="moe_collect",
    )(d1, d2, yg.reshape(n_slots, 1, D_MODEL))
    y12 = y12.reshape(2, rows, D_MODEL)

    sub = 2 if tm >= 1024 else 1
    tc = tm // sub
    return pl.pallas_call(
        _combine_kernel,
        grid=(n_tiles * sub,),
        in_specs=[
            pl.BlockSpec((tc, D_MODEL), lambda i: (tile0 * sub + i, 0)),
            pl.BlockSpec((None, tc, D_MODEL), lambda i: (0, i, 0)),
            pl.BlockSpec((None, tc, D_MODEL), lambda i: (1, i, 0)),
            pl.BlockSpec((tc, LANES), lambda i: (i, 0)),
            pl.BlockSpec((None, 6, D_MODEL), lambda i: (geo.mod_row(tile0 + i // sub), 0, 0)),
        ],
        out_specs=pl.BlockSpec((tc, D_MODEL), lambda i: (i, 0)),
        out_shape=jax.ShapeDtypeStruct((rows, D_MODEL), F32),
        compiler_params=_params(1),
        name="moe_combine",
    )(xs, y12, y12, route, mod_l)


MOE_TOKEN_BLOCK = 512


def _dispatch_mm_kernel(rp_ref, sbp_ref, fp_ref, np_ref, dt_ref, h_ref, o_ref):
    p = pl.program_id(0)
    tg, tb = o_ref.shape[0], h_ref.shape[0]

    @pl.when(p < np_ref[0])
    def _():
        slot = rp_ref[p] * tg + lax.broadcasted_iota(jnp.int32, (tg, tb), 0)
        d = dt_ref[...]
        hit = jnp.logical_or(d[0:1, :] == slot, d[1:2, :] == slot)
        sel = jnp.where(hit, 1.0, 0.0).astype(BF16)
        rows = _dot(sel, h_ref[...])

        @pl.when(fp_ref[p] == 1)
        def _():
            o_ref[...] = rows.astype(o_ref.dtype)

        @pl.when(fp_ref[p] == 0)
        def _():
            o_ref[...] = (o_ref[...].astype(F32) + rows).astype(o_ref.dtype)


def _collect_mm_kernel(sbp_ref, rp_ref, fp_ref, lp_ref, np_ref, x_ref, dest_ref, route_ref, y_ref, mod_ref, o_ref):
    p = pl.program_id(0)
    tb, tg = o_ref.shape[0], y_ref.shape[0]

    @pl.when(p < np_ref[0])
    def _():
        slot = rp_ref[p] * tg + lax.broadcasted_iota(jnp.int32, (tb, tg), 1)
        dest = dest_ref[...]
        rec = route_ref[...]
        sel = (jnp.where(dest[:, 0:1] == slot, rec[:, ROUTE_W1:ROUTE_W1 + 1], 0.0)
               + jnp.where(dest[:, 1:2] == slot, rec[:, ROUTE_W2:ROUTE_W2 + 1], 0.0))
        part = _dot(sel.astype(BF16), y_ref[...])

        @pl.when(fp_ref[p] == 1)
        def _():
            o_ref[...] = part

        @pl.when(fp_ref[p] == 0)
        def _():
            o_ref[...] += part

        @pl.when(lp_ref[p] == 1)
        def _():
            o_ref[...] = x_ref[...] + mod_ref[5:6, :] * o_ref[...]


def _moe_sparse2(geo, xs, mod_l, nw, router, w_gate, w_up, w_down, tile0, n_tiles):
    tm, tg = geo.tm, MOE_ROW_TILE
    tb = min(MOE_TOKEN_BLOCK, tm)
    rows = n_tiles * tm
    nb = rows // tb
    n_slots = 2 * rows + N_EXPERTS * tg
    n_gt = n_slots // tg
    n_pairs_max = n_gt + nb * N_EXPERTS
    rec_tile = pl.BlockSpec((tm, LANES), lambda i: (i, 0))

    h2, route, counts = pl.pallas_call(
        _route_kernel,
        grid=(n_tiles,),
        in_specs=[
            pl.BlockSpec((tm, D_MODEL), lambda i: (tile0 + i, 0)),
            pl.BlockSpec((None, 6, D_MODEL), lambda i: (geo.mod_row(tile0 + i), 0, 0)),
            pl.BlockSpec((1, D_MODEL), lambda i: (0, 0)),
            pl.BlockSpec((D_MODEL, LANES), lambda i: (0, 0)),
        ],
        out_specs=[pl.BlockSpec((tm, D_MODEL), lambda i: (i, 0)), rec_tile, pl.BlockSpec((8, LANES), lambda i: (0, 0))],
        out_shape=[
            jax.ShapeDtypeStruct((rows, D_MODEL), BF16),
            jax.ShapeDtypeStruct((rows, LANES), F32),
            jax.ShapeDtypeStruct((8, LANES), F32),
        ],
        scratch_shapes=[pltpu.VMEM((tm, tm), BF16), pltpu.VMEM((8, LANES), F32)],
        compiler_params=_params(1),
        name="moe_route",
    )(xs, mod_l, nw, router)

    cnt = counts[0, :N_EXPERTS].astype(jnp.int32)
    padded = ((cnt + tg - 1) // tg) * tg
    ends = jnp.cumsum(padded)
    starts = ends - padded
    n_used = (ends[-1] // tg).astype(jnp.int32).reshape(1)
    tile_expert = jnp.searchsorted(ends, jnp.minimum(jnp.arange(n_gt), n_used[0] - 1) * tg, side="right")
    tile_expert = jnp.minimum(tile_expert, N_EXPERTS - 1).astype(jnp.int32)
    start_row = jnp.zeros((1, LANES), F32).at[0, :N_EXPERTS].set(starts.astype(F32))

    dest = pl.pallas_call(
        _slots_kernel,
        grid=(n_tiles,),
        in_specs=[rec_tile, pl.BlockSpec((1, LANES), lambda i: (0, 0))],
        out_specs=rec_tile,
        out_shape=jax.ShapeDtypeStruct((rows, LANES), jnp.int32),
        compiler_params=_params(1),
        name="moe_slots",
    )(route, start_row)

    pos_lo = route.reshape(nb, tb, LANES)[:, 0, :N_EXPERTS].astype(jnp.int32)
    pos_hi = jnp.concatenate([pos_lo[1:], cnt[None, :]], axis=0)
    t_lo = (starts[None, :] + pos_lo) // tg
    t_hi = (starts[None, :] + pos_hi - 1) // tg
    tiles = jnp.arange(n_gt)[:, None, None]
    share = jnp.any((pos_hi > pos_lo)[None] & (t_lo[None] <= tiles) & (tiles <= t_hi[None]), axis=-1)
    n_pairs = jnp.sum(share).astype(jnp.int32).reshape(1)
    last_valid = jnp.minimum(jnp.arange(n_pairs_max), n_pairs[0] - 1)

    def pair_list(mat):
        flat = jnp.nonzero(mat.ravel(), size=n_pairs_max, fill_value=0)[0][last_valid]
        major, minor = (flat // mat.shape[1]).astype(jnp.int32), (flat % mat.shape[1]).astype(jnp.int32)
        first = jnp.concatenate([jnp.ones((1,), jnp.int32), (major[1:] != major[:-1]).astype(jnp.int32)])
        last = jnp.concatenate([(major[1:] != major[:-1]).astype(jnp.int32), jnp.ones((1,), jnp.int32)])
        last = jnp.where(jnp.arange(n_pairs_max) == n_pairs[0] - 1, 1, last)
        return major, minor, first, last

    d_r, d_sb, d_first, _ = pair_list(share)
    c_sb, c_r, c_first, c_last = pair_list(share.T)
    dest_t = dest[:, :2].T

    xg = pl.pallas_call(
        _dispatch_mm_kernel,
        grid_spec=pltpu.PrefetchScalarGridSpec(
            num_scalar_prefetch=4,
            grid=(n_pairs_max,),
            in_specs=[
                pl.BlockSpec((2, tb), lambda p, rp, sbp, fp, npr: (0, sbp[p])),
                pl.BlockSpec((tb, D_MODEL), lambda p, rp, sbp, fp, npr: (sbp[p], 0)),
            ],
            out_specs=pl.BlockSpec((tg, D_MODEL), lambda p, rp, sbp, fp, npr: (rp[p], 0)),
        ),
        out_shape=jax.ShapeDtypeStruct((n_slots, D_MODEL), BF16),
        compiler_params=_params(1),
        name="moe_dispatch",
    )(d_r, d_sb, d_first, n_pairs, dest_t, h2)

    tf = 1024
    hg = pl.pallas_call(
        _expert_up_kernel,
        grid_spec=pltpu.PrefetchScalarGridSpec(
            num_scalar_prefetch=2,
            grid=(MOE_D_FF // tf, n_gt),
            in_specs=[
                pl.BlockSpec((tg, D_MODEL), lambda j, r, te, nu: (r, 0)),
                pl.BlockSpec((None, D_MODEL, tf), lambda j, r, te, nu: (te[r], 0, j)),
                pl.BlockSpec((None, D_MODEL, tf), lambda j, r, te, nu: (te[r], 0, j)),
            ],
            out_specs=pl.BlockSpec((tg, tf), lambda j, r, te, nu: (r, j)),
        ),
        out_shape=jax.ShapeDtypeStruct((n_slots, MOE_D_FF), BF16),
        compiler_params=_params(2),
        name="moe_expert_up",
    )(tile_expert, n_used, xg, w_gate, w_up)

    tn = 1024
    yg = pl.pallas_call(
        _expert_down_kernel,
        grid_spec=pltpu.PrefetchScalarGridSpec(
            num_scalar_prefetch=2,
            grid=(D_MODEL // tn, n_gt),
            in_specs=[
                pl.BlockSpec((tg, MOE_D_FF), lambda j, r, te, nu: (r, 0)),
                pl.BlockSpec((None, MOE_D_FF, tn), lambda j, r, te, nu: (te[r], 0, j)),
            ],
            out_specs=pl.BlockSpec((tg, tn), lambda j, r, te, nu: (r, j)),
        ),
        out_shape=jax.ShapeDtypeStruct((n_slots, D_MODEL), BF16),
        compiler_params=_params(2),
        name="moe_expert_down",
    )(tile_expert, n_used, hg, w_down)

    per = tm // tb
    return pl.pallas_call(
        _collect_mm_kernel,
        grid_spec=pltpu.PrefetchScalarGridSpec(
            num_scalar_prefetch=5,
            grid=(n_pairs_max,),
            in_specs=[
                pl.BlockSpec((tb, D_MODEL), lambda p, sbp, rp, fp, lp, npr: (tile0 * per + sbp[p], 0)),
                pl.BlockSpec((tb, LANES), lambda p, sbp, rp, fp, lp, npr: (sbp[p], 0)),
                pl.BlockSpec((tb, LANES), lambda p, sbp, rp, fp, lp, npr: (sbp[p], 0)),
                pl.BlockSpec((tg, D_MODEL), lambda p, sbp, rp, fp, lp, npr: (rp[p], 0)),
                pl.BlockSpec((None, 6, D_MODEL), lambda p, sbp, rp, fp, lp, npr: (geo.mod_row(tile0 + sbp[p] // per), 0, 0)),
            ],
            out_specs=pl.BlockSpec((tb, D_MODEL), lambda p, sbp, rp, fp, lp, npr: (sbp[p], 0)),
        ),
        out_shape=jax.ShapeDtypeStruct((rows, D_MODEL), F32),
        compiler_params=_params(1),
        name="moe_collect",
    )(c_sb, c_r, c_first, c_last, n_pairs, xs, dest, route, yg, mod_l)


def _final_norm_kernel(x_ref, w_ref, o_ref):
    x = x_ref[...]
    o_ref[...] = x * lax.rsqrt(jnp.mean(x * x, axis=-1, keepdims=True) + EPS) * w_ref[...]


def _final_norm(x, w, tm):
    rows = x.shape[0]
    return pl.pallas_call(
        _final_norm_kernel,
        grid=(rows // tm,),
        in_specs=[pl.BlockSpec((tm, D_MODEL), lambda i: (i, 0)), pl.BlockSpec((1, D_MODEL), lambda i: (0, 0))],
        out_specs=pl.BlockSpec((tm, D_MODEL), lambda i: (i, 0)),
        out_shape=jax.ShapeDtypeStruct((rows, D_MODEL), F32),
        compiler_params=_params(1),
        name="final_norm",
    )(x, w)


def _rope_tables(geo):
    half = RET_DK // 4
    inv = ROPE_BASE ** (-jnp.arange(half, dtype=F32) / half)
    pos = jnp.arange(geo.seq)
    ang_r = (pos // GRID_W).astype(F32)[:, None] * inv[None, :]
    ang_c = (pos % GRID_W).astype(F32)[:, None] * inv[None, :]
    cos = jnp.concatenate([jnp.cos(ang_r), jnp.cos(ang_r), jnp.cos(ang_c), jnp.cos(ang_c)], axis=1)
    sin = jnp.concatenate([-jnp.sin(ang_r), jnp.sin(ang_r), -jnp.sin(ang_c), jnp.sin(ang_c)], axis=1)
    cos = jnp.concatenate([jnp.ones((geo.tm, LANES), F32), cos], axis=0)
    sin = jnp.concatenate([jnp.zeros((geo.tm, LANES), F32), sin], axis=0)
    return cos, sin


_DT_PERM = np.array([d * SSM_HEADS + g * SSM_HPG + h
                     for g in range(SSM_GROUPS) for d in range(2) for h in range(SSM_HPG)])


def kernel(x, c, ctx, c_ctx, w_ada, b_ada, norm1_w, norm2_w, w_in, conv_w, conv_b, ret_decay_f, ret_decay_b, ret_gn_w, ssm_a_log_f, ssm_a_log_b, ssm_dt_bias_f, ssm_dt_bias_b, ssm_d, ssm_norm_w, w_ret_proj, w_ssm_proj, w_out, ffn_w_gate, ffn_w_up, ffn_w_down, moe_router, moe_w_gate, moe_w_up, moe_w_down, final_norm_w):
    batch, seq, d = x.shape
    ctx_len = ctx.shape[1]
    depth = w_ada.shape[0]
    assert d == D_MODEL and seq % GRID_W == 0
    geo = _Geom(batch, ctx_len, seq)
    tm = geo.tm

    mod_rows = -(-(batch + 1) // 8) * 8
    cvec = jnp.zeros((mod_rows, d), F32).at[0].set(c_ctx).at[1:batch + 1].set(c)
    mod = _modulation(cvec, w_ada, b_ada).reshape(depth, mod_rows, 6, d)
    cos_t, sin_t = _rope_tables(geo)

    xs = jnp.concatenate([ctx.reshape(batch * ctx_len, d), x.reshape(batch * seq, d)], axis=0)
    lat_tile0 = geo.n_ctx_tiles
    n_lat_tiles = geo.n_tiles - geo.n_ctx_tiles

    for i in range(depth):
        mod_l = mod[i]
        w_main = jnp.concatenate([w_in[i][:, ORIG_Z_LO:ORIG_Z_HI], w_in[i][:, :ORIG_Z_LO],
                                  w_in[i][:, ORIG_Z_HI:ORIG_DT_LO], w_in[i][:, ORIG_DT_HI:]], axis=1).astype(BF16)
        w_dt = w_in[i][:, ORIG_DT_LO:ORIG_DT_HI][:, _DT_PERM].astype(BF16)
        proj, dt = _inproj(geo, xs, mod_l, norm1_w[i][None, :], w_main, w_dt, cos_t, sin_t)

        xbc = _conv_silu(geo, proj, conv_w[i], conv_b[i][None, :])
        bias = jnp.concatenate([ssm_dt_bias_f[i], ssm_dt_bias_b[i]])[_DT_PERM][None, :]
        alog = jnp.concatenate([ssm_a_log_f[i], ssm_a_log_b[i]])[_DT_PERM][None, :]
        cc, cp, cpt, dtt = _decay_tables(geo, dt, bias, alog)
        dskip = jnp.repeat(ssm_d[i], SSM_INNER // SSM_HEADS)[None, :]
        ys = _ssd_scan2(geo, proj, xbc, cc, cp, cpt, dtt, dskip, ssm_norm_w[i][None, :])
        yr = _ret_scan2(geo, proj, jnp.stack([ret_decay_f[i], ret_decay_b[i]]), ret_gn_w[i][None, :])

        merged = _merge(geo, yr, ys, w_ret_proj[i].astype(BF16), w_ssm_proj[i].astype(BF16), proj)
        xs = _matmul_residual(geo, merged, w_out[i].astype(BF16), xs, mod_l, 2, 0, geo.n_tiles, 1024)

        j = i // 2
        tile0, n_tiles = (0, geo.n_tiles) if i < depth - 1 else (lat_tile0, n_lat_tiles)
        if i % 2 == 0:
            hid = _ffn_up(geo, xs, mod_l, norm2_w[i][None, :], ffn_w_gate[j].astype(BF16), ffn_w_up[j].astype(BF16))
            hid = hid[tile0 * tm:]
            xs = _matmul_residual(geo, hid, ffn_w_down[j].astype(BF16), xs, mod_l, 5, tile0, n_tiles, 512)
        else:
            router = jnp.zeros((d, LANES), F32).at[:, :N_EXPERTS].set(moe_router[j])
            xs = _moe_sparse2(geo, xs, mod_l, norm2_w[i][None, :], router, moe_w_gate[j].astype(BF16),
                              moe_w_up[j].astype(BF16), moe_w_down[j].astype(BF16), tile0, n_tiles)

    lat = xs if xs.shape[0] == batch * seq else xs[batch * ctx_len:]
    return _final_norm(lat, final_norm_w[None, :], tm).reshape(batch, seq, d)
```

```python
import functools

import numpy as np
import jax
import jax.numpy as jnp
from jax import lax
from jax.experimental import pallas as pl
from jax.experimental.pallas import tpu as pltpu

F32 = jnp.float32
BF16 = jnp.bfloat16

D_MODEL = 2048
GRID_W = 64
CHUNK = 128
EPS = 1e-6
ROPE_BASE = 10000.0
RET_HEADS = 8
RET_DK = 128
RET_DV = 256
SSM_INNER = 4096
SSM_HEADS = 64
SSM_GROUPS = 8
SSM_HPG = 8
SSM_GROUP_W = SSM_INNER // SSM_GROUPS
SSM_STATE = 128
XBC_W = 6144
CONV_K = 5
D_FF = 5632
N_EXPERTS = 8
MOE_D_FF = 4096

COL_Z, COL_Q, COL_K, COL_V, COL_G, COL_XBC, COL_GATES = 0, 4096, 5120, 6144, 8192, 10240, 16384
PROJ_W = 20480
ORIG_Z_LO, ORIG_Z_HI, ORIG_DT_LO, ORIG_DT_HI = 6144, 10240, 16384, 16512
SCAN_ROWS = 256

LANES = 128
VMEM_LIMIT_BYTES = 56 * 1024 * 1024


def _params(n_axes, vmem=VMEM_LIMIT_BYTES):
    return pltpu.CompilerParams(dimension_semantics=("arbitrary",) * n_axes, vmem_limit_bytes=vmem)


def _silu(x):
    return x * jax.nn.sigmoid(x)


def _split2(x):
    hi = x.astype(BF16)
    lo = (x - hi.astype(F32)).astype(BF16)
    return hi, lo


def _split3(x):
    hi = x.astype(BF16)
    r = x - hi.astype(F32)
    mid = r.astype(BF16)
    lo = (r - mid.astype(F32)).astype(BF16)
    return hi, mid, lo


def _dot(a, b):
    return jnp.dot(a, b, preferred_element_type=F32)


def _dot_nt(a, b):
    return lax.dot_general(a, b, (((1,), (1,)), ((), ())), preferred_element_type=F32)


def _dot_tn(a, b):
    return lax.dot_general(a, b, (((0,), (0,)), ((), ())), preferred_element_type=F32)


def _norm_mod(x, nw, shift, scale):
    y = x * lax.rsqrt(jnp.mean(x * x, axis=-1, keepdims=True) + EPS)
    return (y * nw) * (1.0 + scale) + shift


def _mod_kernel(c_ref, w_ref, b_ref, o_ref):
    s_hi, s_lo = _split2(_silu(c_ref[...]))
    w_hi, w_lo = _split2(w_ref[...])
    o_ref[...] = _dot(s_hi, w_hi) + _dot(s_hi, w_lo) + _dot(s_lo, w_hi) + b_ref[...]


def _modulation(cvec, w_ada, b_ada):
    depth, d, w6 = w_ada.shape
    rows = cvec.shape[0]
    tn = 512
    return pl.pallas_call(
        _mod_kernel,
        grid=(depth, w6 // tn),
        in_specs=[
            pl.BlockSpec((rows, d), lambda l, j: (0, 0)),
            pl.BlockSpec((None, d, tn), lambda l, j: (l, 0, j)),
            pl.BlockSpec((None, 1, tn), lambda l, j: (l, 0, j)),
        ],
        out_specs=pl.BlockSpec((None, rows, tn), lambda l, j: (l, 0, j)),
        out_shape=jax.ShapeDtypeStruct((depth, rows, w6), F32),
        compiler_params=_params(2),
        name="adaln_mod",
    )(cvec, w_ada, b_ada.reshape(depth, 1, w6))


class _Geom:
    def __init__(self, batch, ctx_len, seq):
        self.batch, self.ctx_len, self.seq = batch, ctx_len, seq
        tm = 1024
        while (batch * ctx_len) % tm or seq % tm:
            tm //= 2
        assert tm >= CHUNK and ctx_len % CHUNK == 0 and seq % CHUNK == 0
        self.tm = tm
        self.ctx_rows = batch * ctx_len
        self.rows = self.ctx_rows + batch * seq
        self.n_ctx_tiles = self.ctx_rows // tm
        self.tiles_per_batch = seq // tm
        self.n_tiles = self.rows // tm
        self.step_rows = min(SCAN_ROWS, ctx_len)
        assert ctx_len % self.step_rows == 0 and seq % self.step_rows == 0 and self.step_rows % CHUNK == 0
        self.ns_ctx = ctx_len // self.step_rows
        self.ns_lat = seq // self.step_rows
        self.ns = self.ns_ctx + self.ns_lat

    def mod_row(self, i):
        return jnp.where(i < self.n_ctx_tiles, 0, 1 + (i - self.n_ctx_tiles) // self.tiles_per_batch)

    def rope_block(self, i):
        return jnp.where(i < self.n_ctx_tiles, 0, 1 + (i - self.n_ctx_tiles) % self.tiles_per_batch)

    def step_block(self, b, c):
        return jnp.where(c < self.ns_ctx, b * self.ns_ctx + c,
                         self.batch * self.ns_ctx + b * self.ns_lat + (c - self.ns_ctx))

    def backward_order(self, s):
        return jnp.where(s < self.ns_ctx, self.ns_ctx - 1 - s, self.ns - 1 - (s - self.ns_ctx))


def _rope_store(acc, cos, sin, o_ref, scale):
    lane = lax.broadcasted_iota(jnp.int32, (acc.shape[0], LANES), 1)
    first_half = (lane % 64) < 32
    for h in range(acc.shape[1] // LANES):
        xh = acc[:, h * LANES:(h + 1) * LANES]
        partner = jnp.where(first_half, pltpu.roll(xh, 96, 1), pltpu.roll(xh, 32, 1))
        o_ref[:, h * LANES:(h + 1) * LANES] = ((xh * cos + partner * sin) * scale).astype(o_ref.dtype)


def _inproj_kernel(x_ref, mod_ref, nw_ref, w_ref, wdt_ref, cos_ref, sin_ref, o_ref, dt_ref, h_ref):
    j = pl.program_id(1)

    @pl.when(j == 0)
    def _():
        h = _norm_mod(x_ref[...], nw_ref[...], mod_ref[0:1, :], mod_ref[1:2, :]).astype(BF16)
        h_ref[...] = h
        dt_ref[...] = _dot(h, wdt_ref[...])

    acc = _dot(h_ref[...], w_ref[...])

    jq, jk = COL_Q // acc.shape[1], COL_K // acc.shape[1]

    @pl.when(j == jq)
    def _():
        _rope_store(acc, cos_ref[...], sin_ref[...], o_ref, 1.0)

    @pl.when(j == jk)
    def _():
        _rope_store(acc, cos_ref[...], sin_ref[...], o_ref, RET_DK ** -0.5)

    @pl.when(jnp.logical_and(j != jq, j != jk))
    def _():
        o_ref[...] = acc.astype(o_ref.dtype)


def _inproj(geo, xs, mod_l, nw, w_main, w_dt, cos_t, sin_t):
    tm, tn = geo.tm, 1024
    return pl.pallas_call(
        _inproj_kernel,
        grid=(geo.n_tiles, PROJ_W // tn),
        in_specs=[
            pl.BlockSpec((tm, D_MODEL), lambda i, j: (i, 0)),
            pl.BlockSpec((None, 6, D_MODEL), lambda i, j: (geo.mod_row(i), 0, 0)),
            pl.BlockSpec((1, D_MODEL), lambda i, j: (0, 0)),
            pl.BlockSpec((D_MODEL, tn), lambda i, j: (0, j)),
            pl.BlockSpec((D_MODEL, LANES), lambda i, j: (0, 0)),
            pl.BlockSpec((tm, LANES), lambda i, j: (geo.rope_block(i), 0)),
            pl.BlockSpec((tm, LANES), lambda i, j: (geo.rope_block(i), 0)),
        ],
        out_specs=[
            pl.BlockSpec((tm, tn), lambda i, j: (i, j)),
            pl.BlockSpec((tm, LANES), lambda i, j: (i, 0)),
        ],
        out_shape=[
            jax.ShapeDtypeStruct((geo.rows, PROJ_W), BF16),
            jax.ShapeDtypeStruct((geo.rows, LANES), F32),
        ],
        scratch_shapes=[pltpu.VMEM((tm, D_MODEL), BF16)],
        compiler_params=_params(2),
        name="in_proj",
    )(xs, mod_l, nw, w_main, w_dt, cos_t, sin_t)


CONV_HALO = 16


def _conv_kernel(prev_ref, main_ref, next_ref, w_ref, b_ref, o_ref, ext_ref, *, geo, rows):
    i = pl.program_id(0)
    start = i * rows
    in_ctx = start < geo.ctx_rows
    seq_len = jnp.where(in_ctx, geo.ctx_len, geo.seq)
    off = jnp.where(in_ctx, start, start - geo.ctx_rows)
    first = (off % seq_len) == 0
    last = ((off + rows) % seq_len) == 0
    ext_ref[0:CONV_HALO, :] = jnp.where(first, 0.0, prev_ref[...].astype(F32))
    ext_ref[CONV_HALO:CONV_HALO + rows, :] = main_ref[...].astype(F32)
    ext_ref[CONV_HALO + rows:, :] = jnp.where(last, 0.0, next_ref[...].astype(F32))
    acc = jnp.broadcast_to(b_ref[...], (rows, b_ref.shape[1]))
    for k in range(CONV_K):
        acc = acc + ext_ref[pl.ds(CONV_HALO + k - CONV_K // 2, rows), :] * w_ref[k:k + 1, :]
    o_ref[...] = _silu(acc).astype(o_ref.dtype)


def _conv_silu(geo, proj, conv_w, conv_b):
    rows = min(256, geo.ctx_len)
    tc = 1024
    col0 = COL_XBC // tc
    n_halo = geo.rows // CONV_HALO
    per = rows // CONV_HALO
    return pl.pallas_call(
        functools.partial(_conv_kernel, geo=geo, rows=rows),
        grid=(geo.rows // rows, XBC_W // tc),
        in_specs=[
            pl.BlockSpec((CONV_HALO, tc), lambda i, j: (jnp.maximum(i * per - 1, 0), col0 + j)),
            pl.BlockSpec((rows, tc), lambda i, j: (i, col0 + j)),
            pl.BlockSpec((CONV_HALO, tc), lambda i, j: (jnp.minimum((i + 1) * per, n_halo - 1), col0 + j)),
            pl.BlockSpec((CONV_K, tc), lambda i, j: (0, j)),
            pl.BlockSpec((1, tc), lambda i, j: (0, j)),
        ],
        out_specs=pl.BlockSpec((rows, tc), lambda i, j: (i, j)),
        out_shape=jax.ShapeDtypeStruct((geo.rows, XBC_W), BF16),
        scratch_shapes=[pltpu.VMEM((rows + 2 * CONV_HALO, tc), F32)],
        compiler_params=_params(2),
        name="ssm_conv",
    )(proj, proj, proj, conv_w, conv_b)


def _decay_kernel(dt_ref, bias_ref, alog_ref, cc_ref, cct_ref, cpt_ref, dtt_ref):
    x = dt_ref[...] + bias_ref[...]
    dtv = jnp.maximum(x, 0.0) + jnp.log1p(jnp.exp(-jnp.abs(x)))
    la = dtv * (-jnp.exp(alog_ref[...]))
    ii = lax.broadcasted_iota(jnp.int32, (CHUNK, CHUNK), 0)
    jj = lax.broadcasted_iota(jnp.int32, (CHUNK, CHUNK), 1)
    lower = (jj <= ii).astype(BF16)
    upper = (jj >= ii).astype(BF16)
    hi, mid, lo = _split3(la)
    cum_f = _dot(lower, hi) + _dot(lower, mid) + _dot(lower, lo)
    cum_b = _dot(upper, hi) + _dot(upper, mid) + _dot(upper, lo)
    lane = lax.broadcasted_iota(jnp.int32, (CHUNK, LANES), 1)
    cc = jnp.where((lane % 16) < 8, cum_f, cum_b)
    cc_ref[...] = cc
    cct_ref[...] = cc.T
    cpt_ref[...] = (cc - jnp.log(dtv)).T
    dtt_ref[...] = dtv.T


def _decay_tables(geo, dt, bias, alog):
    n_chunks = geo.rows // CHUNK
    col = pl.BlockSpec((CHUNK, LANES), lambda i: (i, 0))
    row = pl.BlockSpec((None, LANES, CHUNK), lambda i: (i, 0, 0))
    vec = pl.BlockSpec((1, LANES), lambda i: (0, 0))
    col_shape = jax.ShapeDtypeStruct((geo.rows, LANES), F32)
    row_shape = jax.ShapeDtypeStruct((n_chunks, LANES, CHUNK), F32)
    return pl.pallas_call(
        _decay_kernel,
        grid=(n_chunks,),
        in_specs=[col, vec, vec],
        out_specs=[col, row, row, row],
        out_shape=[col_shape, row_shape, row_shape, row_shape],
        compiler_params=_params(1),
        name="ssm_decay",
    )(dt, bias, alog)


def _ssd_kernel(q_ref, k_ref, xs_ref, z_ref, cc_ref, cp_ref, cct_ref, cpt_ref, dtt_ref, dskip_ref,
                nw_ref, o_ref, sf_ref, sb_ref, hist_ref, *, geo):
    g, ph, s = pl.program_id(1), pl.program_id(2), pl.program_id(3)
    c = jnp.where(ph == 0, geo.backward_order(s), s)
    shift = (LANES - 16 * g) % LANES
    cc = pltpu.roll(cc_ref[...], shift, 1)
    cp = pltpu.roll(cp_ref[...], shift, 1)
    k = k_ref[...]
    xs = xs_ref[...]
    lane = lax.broadcasted_iota(jnp.int32, (CHUNK, LANES), 1)
    left = lane < 64
    left_row = lax.broadcasted_iota(jnp.int32, (1, LANES), 1) < 64
    n_pairs = SSM_HPG // 2

    def pair_cols(a, h1, h2):
        return jnp.where(left, a[:, h1:h1 + 1], a[:, h2:h2 + 1])

    @pl.when(jnp.logical_and(ph == 0, s == 0))
    def _():
        sb_ref[...] = jnp.zeros_like(sb_ref)

    @pl.when(jnp.logical_and(ph == 1, s == 0))
    def _():
        sf_ref[...] = jnp.zeros_like(sf_ref)

    @pl.when(ph == 0)
    def _():
        hist_ref[c] = sb_ref[...].astype(BF16)
        w_all = jnp.exp(cc[0:1, :] - cp)
        a_all = jnp.exp(cc[0:1, :])
        for p in range(n_pairs):
            h1, h2 = SSM_HPG + 2 * p, SSM_HPG + 2 * p + 1
            sl = slice(p * LANES, (p + 1) * LANES)
            wb = pair_cols(w_all, h1, h2)
            ab = jnp.where(left_row, a_all[:, h1:h1 + 1], a_all[:, h2:h2 + 1])
            vw = (xs[:, sl].astype(F32) * wb).astype(BF16)
            sb_ref[:, sl] = ab * sb_ref[:, sl] + _dot_tn(k, vw)

    @pl.when(ph == 1)
    def _():
        q = q_ref[...]
        cct = cct_ref[...]
        cpt = cpt_ref[...]
        dtt = dtt_ref[...]
        scores = _dot_nt(q, k)
        ii = lax.broadcasted_iota(jnp.int32, (CHUNK, CHUNK), 0)
        jj = lax.broadcasted_iota(jnp.int32, (CHUNK, CHUNK), 1)
        lower = jj <= ii
        diag = jj == ii
        yf = _dot(q, sf_ref[...].astype(BF16))
        yb = _dot(q, hist_ref[c])
        e_all = jnp.exp(cc)
        w_all = jnp.exp(cc[CHUNK - 1:CHUNK, :] - cp)
        a_all = jnp.exp(cc[CHUNK - 1:CHUNK, :])
        ys = []
        for p in range(n_pairs):
            sl = slice(p * LANES, (p + 1) * LANES)
            vp = xs[:, sl]
            probs = []
            for h in (2 * p, 2 * p + 1):
                hb = SSM_HPG + h
                arg = jnp.where(lower, cc[:, h:h + 1] - cpt[h:h + 1, :], cc[:, hb:hb + 1] - cpt[hb:hb + 1, :])
                e = jnp.exp(arg) + jnp.where(diag, dtt[hb:hb + 1, :], 0.0)
                probs.append((scores * e).astype(BF16))
            zero = jnp.zeros_like(vp)
            rhs = jnp.concatenate([jnp.where(left, vp, zero), jnp.where(left, zero, vp)], axis=0)
            y = _dot(jnp.concatenate(probs, axis=1), rhs)
            h1, h2 = 2 * p, 2 * p + 1
            y = y + pair_cols(e_all, h1, h2) * yf[:, sl] + pair_cols(e_all, SSM_HPG + h1, SSM_HPG + h2) * yb[:, sl]
            ys.append(y)
            wf = pair_cols(w_all, h1, h2)
            af = jnp.where(left_row, a_all[:, h1:h1 + 1], a_all[:, h2:h2 + 1])
            vw = (vp.astype(F32) * wf).astype(BF16)
            sf_ref[:, sl] = af * sf_ref[:, sl] + _dot_tn(k, vw)
        y = jnp.concatenate(ys, axis=1)
        y = y + dskip_ref[...] * xs.astype(F32)
        y = y * _silu(z_ref[...].astype(F32))
        y = y * lax.rsqrt(jnp.mean(y * y, axis=-1, keepdims=True) + EPS)
        o_ref[...] = (y * nw_ref[...]).astype(o_ref.dtype)


def _ssd_scan(geo, proj, xbc, cc, cp, cct, cpt, dtt, dskip, norm_w):
    gw = SSM_GROUP_W

    def blk(b, c):
        return geo.chunk_block(b, c)

    def cur(b, g, ph, s):
        return blk(b, jnp.where(ph == 0, geo.backward_order(s), s))

    def out_blk(b, g, ph, s):
        return blk(b, jnp.where(ph == 0, 0, s))

    row_spec = pl.BlockSpec((None, 16, CHUNK), lambda b, g, ph, s: (cur(b, g, ph, s), g, 0))
    col_spec = pl.BlockSpec((CHUNK, LANES), lambda b, g, ph, s: (cur(b, g, ph, s), 0))
    return pl.pallas_call(
        functools.partial(_ssd_kernel, geo=geo),
        grid=(geo.batch, SSM_GROUPS, 2, geo.nc),
        in_specs=[
            pl.BlockSpec((CHUNK, SSM_STATE), lambda b, g, ph, s: (cur(b, g, ph, s), (SSM_INNER + 1024) // SSM_STATE + g)),
            pl.BlockSpec((CHUNK, SSM_STATE), lambda b, g, ph, s: (cur(b, g, ph, s), SSM_INNER // SSM_STATE + g)),
            pl.BlockSpec((CHUNK, gw), lambda b, g, ph, s: (cur(b, g, ph, s), g)),
            pl.BlockSpec((CHUNK, gw), lambda b, g, ph, s: (cur(b, g, ph, s), COL_Z // gw + g)),
            col_spec, col_spec, row_spec, row_spec, row_spec,
            pl.BlockSpec((1, gw), lambda b, g, ph, s: (0, g)),
            pl.BlockSpec((1, gw), lambda b, g, ph, s: (0, g)),
        ],
        out_specs=pl.BlockSpec((CHUNK, gw), lambda b, g, ph, s: (out_blk(b, g, ph, s), g)),
        out_shape=jax.ShapeDtypeStruct((geo.rows, SSM_INNER), BF16),
        scratch_shapes=[
            pltpu.VMEM((SSM_STATE, gw), F32),
            pltpu.VMEM((SSM_STATE, gw), F32),
            pltpu.VMEM((geo.nc, SSM_STATE, gw), BF16),
        ],
        compiler_params=_params(4),
        name="ssd_scan",
    )(xbc, xbc, xbc, proj, cc, cp, cct, cpt, dtt, dskip, norm_w)


def _ret_kernel(dec_ref, q_ref, k_ref, v_ref, g_ref, gnw_ref, o_ref,
                sf_ref, sb_ref, hist_ref, m_ref, ey_ref, wk_ref, a_ref, *, geo):
    h, ph, s = pl.program_id(1), pl.program_id(2), pl.program_id(3)
    c = jnp.where(ph == 0, geo.backward_order(s), s)
    n = float(CHUNK)

    @pl.when(jnp.logical_and(ph == 0, s == 0))
    def _():
        lam_f = jnp.exp(jnp.full((CHUNK, 1), dec_ref[0, h], F32))
        lam_b = jnp.exp(jnp.full((CHUNK, 1), dec_ref[1, h], F32))
        ii = lax.broadcasted_iota(jnp.int32, (CHUNK, CHUNK), 0)
        jj = lax.broadcasted_iota(jnp.int32, (CHUNK, CHUNK), 1)
        dist = (ii - jj).astype(F32)
        m_ref[...] = (jnp.where(jj <= ii, jnp.exp(-lam_f * dist), 0.0)
                      + jnp.where(jj >= ii, jnp.exp(lam_b * dist), 0.0))
        row = lax.broadcasted_iota(jnp.int32, (CHUNK, 1), 0).astype(F32)
        ey_ref[0] = jnp.broadcast_to(jnp.exp(-lam_f * (row + 1.0)), (CHUNK, RET_DV))
        ey_ref[1] = jnp.broadcast_to(jnp.exp(-lam_b * (n - row)), (CHUNK, RET_DV))
        wk_ref[0] = jnp.broadcast_to(jnp.exp(-lam_f * (n - 1.0 - row)), (CHUNK, RET_DK))
        wk_ref[1] = jnp.broadcast_to(jnp.exp(-lam_b * row), (CHUNK, RET_DK))
        a_ref[0] = jnp.broadcast_to(jnp.exp(-lam_f[0:8, :] * n), (8, RET_DV))
        a_ref[1] = jnp.broadcast_to(jnp.exp(-lam_b[0:8, :] * n), (8, RET_DV))
        sb_ref[...] = jnp.zeros_like(sb_ref)

    @pl.when(jnp.logical_and(ph == 1, s == 0))
    def _():
        sf_ref[...] = jnp.zeros_like(sf_ref)

    k = k_ref[...]
    v = v_ref[...]

    @pl.when(ph == 0)
    def _():
        hist_ref[c] = sb_ref[...].astype(BF16)
        kw = (k.astype(F32) * wk_ref[1]).astype(BF16)
        sb_ref[...] = a_ref[1][0:1, :] * sb_ref[...] + _dot_tn(kw, v)

    @pl.when(ph == 1)
    def _():
        q = q_ref[...]
        probs = (_dot_nt(q, k) * m_ref[...]).astype(BF16)
        y = _dot(probs, v)
        y = y + ey_ref[0] * _dot(q, sf_ref[...].astype(BF16)) + ey_ref[1] * _dot(q, hist_ref[c])
        kw = (k.astype(F32) * wk_ref[0]).astype(BF16)
        sf_ref[...] = a_ref[0][0:1, :] * sf_ref[...] + _dot_tn(kw, v)
        mu = jnp.mean(y, axis=-1, keepdims=True)
        yc = y - mu
        var = jnp.mean(yc * yc, axis=-1, keepdims=True)
        yn = yc * lax.rsqrt(var + EPS)
        o_ref[...] = (yn * gnw_ref[...] * _silu(g_ref[...].astype(F32))).astype(o_ref.dtype)


def _ret_scan(geo, proj, decays, gn_w):
    def cur(b, h, ph, s):
        return geo.chunk_block(b, jnp.where(ph == 0, geo.backward_order(s), s))

    def out_blk(b, h, ph, s):
        return geo.chunk_block(b, jnp.where(ph == 0, 0, s))

    return pl.pallas_call(
        functools.partial(_ret_kernel, geo=geo),
        grid=(geo.batch, RET_HEADS, 2, geo.nc),
        in_specs=[
            pl.BlockSpec(memory_space=pltpu.SMEM),
            pl.BlockSpec((CHUNK, RET_DK), lambda b, h, ph, s: (cur(b, h, ph, s), COL_Q // RET_DK + h)),
            pl.BlockSpec((CHUNK, RET_DK), lambda b, h, ph, s: (cur(b, h, ph, s), COL_K // RET_DK + h)),
            pl.BlockSpec((CHUNK, RET_DV), lambda b, h, ph, s: (cur(b, h, ph, s), COL_V // RET_DV + h)),
            pl.BlockSpec((CHUNK, RET_DV), lambda b, h, ph, s: (cur(b, h, ph, s), COL_G // RET_DV + h)),
            pl.BlockSpec((1, RET_DV), lambda b, h, ph, s: (0, h)),
        ],
        out_specs=pl.BlockSpec((CHUNK, RET_DV), lambda b, h, ph, s: (out_blk(b, h, ph, s), h)),
        out_shape=jax.ShapeDtypeStruct((geo.rows, RET_HEADS * RET_DV), BF16),
        scratch_shapes=[
            pltpu.VMEM((RET_DK, RET_DV), F32),
            pltpu.VMEM((RET_DK, RET_DV), F32),
            pltpu.VMEM((geo.nc, RET_DK, RET_DV), BF16),
            pltpu.VMEM((CHUNK, CHUNK), F32),
            pltpu.VMEM((2, CHUNK, RET_DV), F32),
            pltpu.VMEM((2, CHUNK, RET_DK), F32),
            pltpu.VMEM((2, 8, RET_DV), F32),
        ],
        compiler_params=_params(4),
        name="ret_scan",
    )(decays, proj, proj, proj, proj, gn_w)


def _lanes2(a):
    return jnp.concatenate([a, a], axis=1)


def _ret_tables(dec_ref, direction, rows, wk_ref, a_ref, e_ref=None, m_ref=None):
    n = float(rows)
    row = lax.broadcasted_iota(jnp.int32, (rows, LANES), 0).astype(F32)
    for h in range(RET_HEADS):
        lam = jnp.exp(jnp.full((rows, LANES), dec_ref[direction, h], F32))
        if direction == 0:
            wk_ref[h] = jnp.exp(-lam * (n - 1.0 - row))
            if e_ref is not None:
                e_ref[0, h] = jnp.exp(-lam * (row + 1.0))
        else:
            wk_ref[h] = jnp.exp(-lam * row)
        a_ref[h] = _lanes2(jnp.exp(-lam[0:8, :] * n))


def _ret_state_kernel(dec_ref, k_ref, v_ref, hist_ref, sb_ref, wk_ref, a_ref, *, rows):
    @pl.when(pl.program_id(1) == 0)
    def _():
        sb_ref[...] = jnp.zeros_like(sb_ref)
        _ret_tables(dec_ref, 1, rows, wk_ref, a_ref)

    def body(h, carry):
        kh = k_ref[:, pl.ds(pl.multiple_of(h * RET_DK, RET_DK), RET_DK)]
        vh = v_ref[:, pl.ds(pl.multiple_of(h * RET_DV, RET_DV), RET_DV)]
        hist_ref[h] = sb_ref[h].astype(BF16)
        kw = (kh.astype(F32) * wk_ref[h]).astype(BF16)
        sb_ref[h] = a_ref[h][0:1, :] * sb_ref[h] + _dot_tn(kw, vh)
        return carry

    lax.fori_loop(0, RET_HEADS, body, 0)


def _ret_out_kernel(dec_ref, q_ref, k_ref, v_ref, g_ref, hist_ref, gnw_ref, o_ref,
                    sf_ref, wk_ref, a_ref, e_ref, m_ref, *, rows):
    @pl.when(pl.program_id(1) == 0)
    def _():
        sf_ref[...] = jnp.zeros_like(sf_ref)
        _ret_tables(dec_ref, 0, rows, wk_ref, a_ref, e_ref)
        n = float(rows)
        row = lax.broadcasted_iota(jnp.int32, (rows, LANES), 0).astype(F32)
        ii = lax.broadcasted_iota(jnp.int32, (rows, rows), 0)
        jj = lax.broadcasted_iota(jnp.int32, (rows, rows), 1)
        dist = (ii - jj).astype(F32)
        for h in range(RET_HEADS):
            lam_f = jnp.exp(jnp.full((rows, 1), dec_ref[0, h], F32))
            lam_b = jnp.exp(jnp.full((rows, 1), dec_ref[1, h], F32))
            m_ref[h] = (jnp.where(jj <= ii, jnp.exp(-lam_f * dist), 0.0)
                        + jnp.where(jj >= ii, jnp.exp(lam_b * dist), 0.0))
            e_ref[1, h] = jnp.exp(-jnp.exp(jnp.full((rows, LANES), dec_ref[1, h], F32)) * (n - row))

    def body(h, carry):
        ok = pl.multiple_of(h * RET_DK, RET_DK)
        ov = pl.multiple_of(h * RET_DV, RET_DV)
        qh = q_ref[:, pl.ds(ok, RET_DK)]
        kh = k_ref[:, pl.ds(ok, RET_DK)]
        vh = v_ref[:, pl.ds(ov, RET_DV)]
        probs = (_dot_nt(qh, kh) * m_ref[h]).astype(BF16)
        y = _dot(probs, vh)
        y = y + _lanes2(e_ref[0, h]) * _dot(qh, sf_ref[h].astype(BF16)) + _lanes2(e_ref[1, h]) * _dot(qh, hist_ref[h])
        kw = (kh.astype(F32) * wk_ref[h]).astype(BF16)
        sf_ref[h] = a_ref[h][0:1, :] * sf_ref[h] + _dot_tn(kw, vh)
        mu = jnp.mean(y, axis=-1, keepdims=True)
        yc = y - mu
        yn = yc * lax.rsqrt(jnp.mean(yc * yc, axis=-1, keepdims=True) + EPS)
        gate = _silu(g_ref[:, pl.ds(ov, RET_DV)].astype(F32))
        o_ref[:, pl.ds(ov, RET_DV)] = (yn * gnw_ref[:, pl.ds(ov, RET_DV)] * gate).astype(o_ref.dtype)
        return carry

    lax.fori_loop(0, RET_HEADS, body, 0)


def _ret_scan2(geo, proj, decays, gn_w):
    rows = geo.step_rows
    qk_w, v_w = RET_HEADS * RET_DK, RET_HEADS * RET_DV

    def bwd(b, s):
        return geo.step_block(b, geo.backward_order(s))

    def fwd(b, s):
        return geo.step_block(b, s)

    smem = pl.BlockSpec(memory_space=pltpu.SMEM)
    hist_shape = (geo.batch, geo.ns, RET_HEADS, RET_DK, RET_DV)
    hist = pl.pallas_call(
        functools.partial(_ret_state_kernel, rows=rows),
        grid=(geo.batch, geo.ns),
        in_specs=[
            smem,
            pl.BlockSpec((rows, qk_w), lambda b, s: (bwd(b, s), COL_K // qk_w)),
            pl.BlockSpec((rows, v_w), lambda b, s: (bwd(b, s), COL_V // v_w)),
        ],
        out_specs=pl.BlockSpec((None, None) + hist_shape[2:], lambda b, s: (b, geo.backward_order(s), 0, 0, 0)),
        out_shape=jax.ShapeDtypeStruct(hist_shape, BF16),
        scratch_shapes=[
            pltpu.VMEM((RET_HEADS, RET_DK, RET_DV), F32),
            pltpu.VMEM((RET_HEADS, rows, LANES), F32),
            pltpu.VMEM((RET_HEADS, 8, RET_DV), F32),
        ],
        compiler_params=_params(2),
        name="ret_state",
    )(decays, proj, proj)
    return pl.pallas_call(
        functools.partial(_ret_out_kernel, rows=rows),
        grid=(geo.batch, geo.ns),
        in_specs=[
            smem,
            pl.BlockSpec((rows, qk_w), lambda b, s: (fwd(b, s), COL_Q // qk_w)),
            pl.BlockSpec((rows, qk_w), lambda b, s: (fwd(b, s), COL_K // qk_w)),
            pl.BlockSpec((rows, v_w), lambda b, s: (fwd(b, s), COL_V // v_w)),
            pl.BlockSpec((rows, v_w), lambda b, s: (fwd(b, s), COL_G // v_w)),
            pl.BlockSpec((None, None) + hist_shape[2:], lambda b, s: (b, s, 0, 0, 0)),
            pl.BlockSpec((1, v_w), lambda b, s: (0, 0)),
        ],
        out_specs=pl.BlockSpec((rows, v_w), lambda b, s: (fwd(b, s), 0)),
        out_shape=jax.ShapeDtypeStruct((geo.rows, v_w), BF16),
        scratch_shapes=[
            pltpu.VMEM((RET_HEADS, RET_DK, RET_DV), F32),
            pltpu.VMEM((RET_HEADS, rows, LANES), F32),
            pltpu.VMEM((RET_HEADS, 8, RET_DV), F32),
            pltpu.VMEM((2, RET_HEADS, rows, LANES), F32),
            pltpu.VMEM((RET_HEADS, rows, rows), F32),
        ],
        compiler_params=_params(2),
        name="ret_out",
    )(decays, proj, proj, proj, proj, hist, gn_w)


def _group_cols(ref, rows, g):
    return pltpu.roll(ref[rows, :], (LANES - 16 * g) % LANES, 1)


def _pair_cols(left, a, h1, h2):
    return jnp.where(left, a[:, h1:h1 + 1], a[:, h2:h2 + 1])


def _pair_rhs(left, vp):
    zero = jnp.zeros_like(vp)
    return jnp.concatenate([jnp.where(left, vp, zero), jnp.where(left, zero, vp)], axis=0)


def _state_update(s_ref, g, sl, kt, w_rows, h1, h2, a_pair, rhs):
    lhs = jnp.concatenate([(kt * w_rows[h1:h1 + 1, :]).astype(BF16), (kt * w_rows[h2:h2 + 1, :]).astype(BF16)], axis=1)
    s_ref[g, :, sl] = a_pair * s_ref[g, :, sl] + _dot(lhs, rhs)


def _ssd_state_kernel(k_ref, xs_ref, cc_ref, cct_ref, cpt_ref, hist_ref, sb_ref, *, cps):
    @pl.when(pl.program_id(1) == 0)
    def _():
        sb_ref[...] = jnp.zeros_like(sb_ref)

    left = lax.broadcasted_iota(jnp.int32, (CHUNK, LANES), 1) < 64
    left_row = lax.broadcasted_iota(jnp.int32, (1, LANES), 1) < 64

    def body(g, carry):
        ok = pl.multiple_of(g * SSM_STATE, SSM_STATE)
        ov = pl.multiple_of(g * SSM_GROUP_W, SSM_GROUP_W)
        ob = pl.multiple_of(g * 16 + SSM_HPG, SSM_HPG)
        for ci in reversed(range(cps)):
            rows = pl.ds(ci * CHUNK, CHUNK)
            a_all = jnp.exp(_group_cols(cc_ref, rows, g)[0:1, :])
            kt = k_ref[rows, pl.ds(ok, SSM_STATE)].astype(F32).T
            hist_ref[ci, g] = sb_ref[g].astype(BF16)
            w_rows = jnp.exp(cct_ref[ci, pl.ds(ob, SSM_HPG), 0:1] - cpt_ref[ci, pl.ds(ob, SSM_HPG), :])
            for p in range(SSM_HPG // 2):
                h1, h2 = 2 * p, 2 * p + 1
                sl = slice(p * LANES, (p + 1) * LANES)
                vp = xs_ref[rows, pl.ds(pl.multiple_of(ov + p * LANES, LANES), LANES)]
                ab = jnp.where(left_row, a_all[:, SSM_HPG + h1:SSM_HPG + h1 + 1], a_all[:, SSM_HPG + h2:SSM_HPG + h2 + 1])
                _state_update(sb_ref, g, sl, kt, w_rows, h1, h2, ab, _pair_rhs(left, vp))
        return carry

    lax.fori_loop(0, SSM_GROUPS, body, 0)


def _ssd_out_kernel(q_ref, k_ref, xs_ref, z_ref, cc_ref, cct_ref, cpt_ref, dtt_ref, hist_ref,
                    dskip_ref, nw_ref, o_ref, sf_ref, *, cps):
    @pl.when(pl.program_id(1) == 0)
    def _():
        sf_ref[...] = jnp.zeros_like(sf_ref)

    lane = lax.broadcasted_iota(jnp.int32, (CHUNK, LANES), 1)
    left = lane < 64
    left_row = lax.broadcasted_iota(jnp.int32, (1, LANES), 1) < 64
    ii = lax.broadcasted_iota(jnp.int32, (CHUNK, CHUNK), 0)
    jj = lax.broadcasted_iota(jnp.int32, (CHUNK, CHUNK), 1)
    lower = jj <= ii
    diag = jj == ii

    def body(g, carry):
        ok = pl.multiple_of(g * SSM_STATE, SSM_STATE)
        ov = pl.multiple_of(g * SSM_GROUP_W, SSM_GROUP_W)
        og = pl.multiple_of(g * 16, 16)
        for ci in range(cps):
            rows = pl.ds(ci * CHUNK, CHUNK)
            cc = _group_cols(cc_ref, rows, g)
            cpt = cpt_ref[ci, pl.ds(og, 16), :]
            dtt = dtt_ref[ci, pl.ds(og, 16), :]
            q = q_ref[rows, pl.ds(ok, SSM_STATE)]
            kt = k_ref[rows, pl.ds(ok, SSM_STATE)].astype(F32).T
            scores = _dot(q, kt.astype(BF16))
            yf = _dot(q, sf_ref[g].astype(BF16))
            yb = _dot(q, hist_ref[ci, g])
            w_rows = jnp.exp(cct_ref[ci, pl.ds(og, SSM_HPG), CHUNK - 1:CHUNK] - cpt[0:SSM_HPG, :])
            a_all = jnp.exp(cc[CHUNK - 1:CHUNK, :])
            ys = []
            for p in range(SSM_HPG // 2):
                h1, h2 = 2 * p, 2 * p + 1
                sl = slice(p * LANES, (p + 1) * LANES)
                cols = pl.ds(pl.multiple_of(ov + p * LANES, LANES), LANES)
                vp = xs_ref[rows, cols]
                probs, ef, eb = [], [], []
                for h in (h1, h2):
                    hb = SSM_HPG + h
                    cf = jnp.broadcast_to(cc[:, h:h + 1], (CHUNK, CHUNK))
                    cb = jnp.broadcast_to(cc[:, hb:hb + 1], (CHUNK, CHUNK))
                    arg = jnp.where(lower, cf - cpt[h:h + 1, :], cb - cpt[hb:hb + 1, :])
                    e = jnp.exp(arg) + jnp.where(diag, dtt[hb:hb + 1, :], 0.0)
                    probs.append((scores * e).astype(BF16))
                    ef.append(jnp.exp(cf))
                    eb.append(jnp.exp(cb))
                rhs = _pair_rhs(left, vp)
                y = _dot(jnp.concatenate(probs, axis=1), rhs)
                y = (y + jnp.where(left, ef[0], ef[1]) * yf[:, sl] + jnp.where(left, eb[0], eb[1]) * yb[:, sl])
                af = jnp.where(left_row, a_all[:, h1:h1 + 1], a_all[:, h2:h2 + 1])
                _state_update(sf_ref, g, sl, kt, w_rows, h1, h2, af, rhs)
                ys.append(y + dskip_ref[:, cols] * vp.astype(F32))
            gcols = pl.ds(ov, SSM_GROUP_W)
            y = jnp.concatenate(ys, axis=1) * _silu(z_ref[rows, gcols].astype(F32))
            y = y * lax.rsqrt(jnp.mean(y * y, axis=-1, keepdims=True) + EPS)
            o_ref[rows, gcols] = (y * nw_ref[:, gcols]).astype(o_ref.dtype)
        return carry

    lax.fori_loop(0, SSM_GROUPS, body, 0)


def _ssd_scan2(geo, proj, xbc, cc, cct, cpt, dtt, dskip, norm_w):
    rows = geo.step_rows
    cps = rows // CHUNK
    bc_w = SSM_GROUPS * SSM_STATE

    def bwd(b, s):
        return geo.step_block(b, geo.backward_order(s))

    def fwd(b, s):
        return geo.step_block(b, s)

    hist_shape = (geo.batch, geo.ns * cps, SSM_GROUPS, SSM_STATE, SSM_GROUP_W)
    hist_block = (None, cps) + hist_shape[2:]
    row_bwd = pl.BlockSpec((cps, LANES, CHUNK), lambda b, s: (bwd(b, s), 0, 0))
    hist = pl.pallas_call(
        functools.partial(_ssd_state_kernel, cps=cps),
        grid=(geo.batch, geo.ns),
        in_specs=[
            pl.BlockSpec((rows, bc_w), lambda b, s: (bwd(b, s), SSM_INNER // bc_w)),
            pl.BlockSpec((rows, SSM_INNER), lambda b, s: (bwd(b, s), 0)),
            pl.BlockSpec((rows, LANES), lambda b, s: (bwd(b, s), 0)),
            row_bwd, row_bwd,
        ],
        out_specs=pl.BlockSpec(hist_block, lambda b, s: (b, geo.backward_order(s), 0, 0, 0)),
        out_shape=jax.ShapeDtypeStruct(hist_shape, BF16),
        scratch_shapes=[pltpu.VMEM((SSM_GROUPS, SSM_STATE, SSM_GROUP_W), F32)],
        compiler_params=_params(2),
        name="ssd_state",
    )(xbc, xbc, cc, cct, cpt)
    col = pl.BlockSpec((rows, LANES), lambda b, s: (fwd(b, s), 0))
    row = pl.BlockSpec((cps, LANES, CHUNK), lambda b, s: (fwd(b, s), 0, 0))
    return pl.pallas_call(
        functools.partial(_ssd_out_kernel, cps=cps),
        grid=(geo.batch, geo.ns),
        in_specs=[
            pl.BlockSpec((rows, bc_w), lambda b, s: (fwd(b, s), SSM_INNER // bc_w + 1)),
            pl.BlockSpec((rows, bc_w), lambda b, s: (fwd(b, s), SSM_INNER // bc_w)),
            pl.BlockSpec((rows, SSM_INNER), lambda b, s: (fwd(b, s), 0)),
            pl.BlockSpec((rows, SSM_INNER), lambda b, s: (fwd(b, s), COL_Z // SSM_INNER)),
            col, row, row, row,
            pl.BlockSpec(hist_block, lambda b, s: (b, s, 0, 0, 0)),
            pl.BlockSpec((1, SSM_INNER), lambda b, s: (0, 0)),
            pl.BlockSpec((1, SSM_INNER), lambda b, s: (0, 0)),
        ],
        out_specs=pl.BlockSpec((rows, SSM_INNER), lambda b, s: (fwd(b, s), 0)),
        out_shape=jax.ShapeDtypeStruct((geo.rows, SSM_INNER), BF16),
        scratch_shapes=[pltpu.VMEM((SSM_GROUPS, SSM_STATE, SSM_GROUP_W), F32)],
        compiler_params=_params(2),
        name="ssd_out",
    )(xbc, xbc, xbc, proj, cc, cct, cpt, dtt, hist, dskip, norm_w)


def _merge_kernel(yr_ref, ys_ref, wr_ref, ws_ref, gr_ref, gs_ref, o_ref):
    r = _dot(yr_ref[...], wr_ref[...])
    s = _dot(ys_ref[...], ws_ref[...])
    m = jax.nn.sigmoid(gr_ref[...].astype(F32)) * r + jax.nn.sigmoid(gs_ref[...].astype(F32)) * s
    o_ref[...] = m.astype(o_ref.dtype)


def _merge(geo, yr, ys, w_ret, w_ssm, proj):
    tm, tn = geo.tm, 512
    gr0 = COL_GATES // tn
    gs0 = (COL_GATES + D_MODEL) // tn
    return pl.pallas_call(
        _merge_kernel,
        grid=(geo.n_tiles, D_MODEL // tn),
        in_specs=[
            pl.BlockSpec((tm, yr.shape[1]), lambda i, j: (i, 0)),
            pl.BlockSpec((tm, ys.shape[1]), lambda i, j: (i, 0)),
            pl.BlockSpec((yr.shape[1], tn), lambda i, j: (0, j)),
            pl.BlockSpec((ys.shape[1], tn), lambda i, j: (0, j)),
            pl.BlockSpec((tm, tn), lambda i, j: (i, gr0 + j)),
            pl.BlockSpec((tm, tn), lambda i, j: (i, gs0 + j)),
        ],
        out_specs=pl.BlockSpec((tm, tn), lambda i, j: (i, j)),
        out_shape=jax.ShapeDtypeStruct((geo.rows, D_MODEL), BF16),
        compiler_params=_params(2),
        name="branch_merge",
    )(yr, ys, w_ret, w_ssm, proj, proj)


def _residual_kernel(a_ref, w_ref, x_ref, mod_ref, o_ref, *, gate_row):
    o_ref[...] = x_ref[...] + mod_ref[gate_row:gate_row + 1, :] * _dot(a_ref[...], w_ref[...])


def _matmul_residual(geo, a, w, xs, mod_l, gate_row, tile0, n_tiles, tn):
    tm = geo.tm
    kdim = a.shape[1]
    return pl.pallas_call(
        functools.partial(_residual_kernel, gate_row=gate_row),
        grid=(n_tiles, D_MODEL // tn),
        in_specs=[
            pl.BlockSpec((tm, kdim), lambda i, j: (i, 0)),
            pl.BlockSpec((kdim, tn), lambda i, j: (0, j)),
            pl.BlockSpec((tm, tn), lambda i, j: (tile0 + i, j)),
            pl.BlockSpec((None, 6, tn), lambda i, j: (geo.mod_row(tile0 + i), 0, j)),
        ],
        out_specs=pl.BlockSpec((tm, tn), lambda i, j: (i, j)),
        out_shape=jax.ShapeDtypeStruct((n_tiles * tm, D_MODEL), F32),
        compiler_params=_params(2),
        name="proj_residual",
    )(a, w, xs, mod_l)


def _ffn_up_kernel(x_ref, mod_ref, nw_ref, wg_ref, wu_ref, o_ref, h_ref):
    @pl.when(pl.program_id(1) == 0)
    def _():
        h_ref[...] = _norm_mod(x_ref[...], nw_ref[...], mod_ref[3:4, :], mod_ref[4:5, :]).astype(BF16)

    h = h_ref[...]
    o_ref[...] = (_silu(_dot(h, wg_ref[...])) * _dot(h, wu_ref[...])).astype(o_ref.dtype)


def _ffn_up(geo, xs, mod_l, nw, w_gate, w_up):
    tm, tf = geo.tm, 512
    return pl.pallas_call(
        _ffn_up_kernel,
        grid=(geo.n_tiles, D_FF // tf),
        in_specs=[
            pl.BlockSpec((tm, D_MODEL), lambda i, j: (i, 0)),
            pl.BlockSpec((None, 6, D_MODEL), lambda i, j: (geo.mod_row(i), 0, 0)),
            pl.BlockSpec((1, D_MODEL), lambda i, j: (0, 0)),
            pl.BlockSpec((D_MODEL, tf), lambda i, j: (0, j)),
            pl.BlockSpec((D_MODEL, tf), lambda i, j: (0, j)),
        ],
        out_specs=pl.BlockSpec((tm, tf), lambda i, j: (i, j)),
        out_shape=jax.ShapeDtypeStruct((geo.rows, D_FF), BF16),
        scratch_shapes=[pltpu.VMEM((tm, D_MODEL), BF16)],
        compiler_params=_params(2),
        name="ffn_up",
    )(xs, mod_l, nw, w_gate, w_up)


def _moe_kernel(x_ref, mod_ref, nw_ref, router_ref, wg_ref, wu_ref, wd_ref, o_ref, h_ref, comb_ref):
    e, j = pl.program_id(1), pl.program_id(2)
    tm = x_ref.shape[0]
    lane = lax.broadcasted_iota(jnp.int32, (tm, LANES), 1)

    @pl.when(jnp.logical_and(e == 0, j == 0))
    def _():
        h = _norm_mod(x_ref[...], nw_ref[...], mod_ref[3:4, :], mod_ref[4:5, :])
        h_hi, h_lo = _split2(h)
        h_ref[...] = h_hi
        r_hi, r_lo = _split2(router_ref[...])
        logits = _dot(h_hi, r_hi) + _dot(h_hi, r_lo) + _dot(h_lo, r_hi)
        neg = -jnp.inf
        lg = jnp.where(lane < N_EXPERTS, logits, neg)
        m1 = jnp.max(lg, axis=-1, keepdims=True)
        i1 = jnp.min(jnp.where(lg == m1, lane, LANES), axis=-1, keepdims=True)
        lg2 = jnp.where(lane == i1, neg, lg)
        m2 = jnp.max(lg2, axis=-1, keepdims=True)
        i2 = jnp.min(jnp.where(lg2 == m2, lane, LANES), axis=-1, keepdims=True)
        e2 = jnp.exp(m2 - m1)
        w1 = 1.0 / (1.0 + e2)
        w2 = e2 / (1.0 + e2)
        comb_ref[...] = jnp.where(lane == i1, w1, 0.0) + jnp.where(lane == i2, w2, 0.0)
        o_ref[...] = jnp.zeros_like(o_ref)

    h = h_ref[...]
    ce = jnp.sum(jnp.where(lane == e, comb_ref[...], 0.0), axis=-1, keepdims=True)
    hid = (_silu(_dot(h, wg_ref[...])) * _dot(h, wu_ref[...]) * ce).astype(BF16)
    for n in range(0, D_MODEL, 512):
        o_ref[:, n:n + 512] += _dot(hid, wd_ref[:, n:n + 512])

    @pl.when(jnp.logical_and(e == pl.num_programs(1) - 1, j == pl.num_programs(2) - 1))
    def _():
        o_ref[...] = x_ref[...] + mod_ref[5:6, :] * o_ref[...]


def _moe(geo, xs, mod_l, nw, router, w_gate, w_up, w_down, tile0, n_tiles):
    sub = 2 if geo.tm >= 1024 else 1
    tm, tf = geo.tm // sub, 512
    tile0, n_tiles = tile0 * sub, n_tiles * sub
    return pl.pallas_call(
        _moe_kernel,
        grid=(n_tiles, N_EXPERTS, MOE_D_FF // tf),
        in_specs=[
            pl.BlockSpec((tm, D_MODEL), lambda i, e, j: (tile0 + i, 0)),
            pl.BlockSpec((None, 6, D_MODEL), lambda i, e, j: (geo.mod_row((tile0 + i) // sub), 0, 0)),
            pl.BlockSpec((1, D_MODEL), lambda i, e, j: (0, 0)),
            pl.BlockSpec((D_MODEL, LANES), lambda i, e, j: (0, 0)),
            pl.BlockSpec((None, D_MODEL, tf), lambda i, e, j: (e, 0, j)),
            pl.BlockSpec((None, D_MODEL, tf), lambda i, e, j: (e, 0, j)),
            pl.BlockSpec((None, tf, D_MODEL), lambda i, e, j: (e, j, 0)),
        ],
        out_specs=pl.BlockSpec((tm, D_MODEL), lambda i, e, j: (i, 0)),
        out_shape=jax.ShapeDtypeStruct((n_tiles * tm, D_MODEL), F32),
        scratch_shapes=[pltpu.VMEM((tm, D_MODEL), BF16), pltpu.VMEM((tm, LANES), F32)],
        compiler_params=_params(3),
        name="moe_dense",
    )(xs, mod_l, nw, router, w_gate, w_up, w_down)


MOE_ROW_TILE = 512
ROUTE_W1, ROUTE_W2, ROUTE_E1, ROUTE_E2 = 8, 9, 10, 11
DMA_ROWS = 256


def _route_kernel(x_ref, mod_ref, nw_ref, router_ref, h_ref, route_ref, cnt_ref, tri_ref, carry_ref):
    i = pl.program_id(0)
    tm = x_ref.shape[0]
    lane = lax.broadcasted_iota(jnp.int32, (tm, LANES), 1)

    @pl.when(i == 0)
    def _():
        ii = lax.broadcasted_iota(jnp.int32, (tm, tm), 0)
        jj = lax.broadcasted_iota(jnp.int32, (tm, tm), 1)
        tri_ref[...] = (jj < ii).astype(BF16)
        carry_ref[...] = jnp.zeros_like(carry_ref)

    h = _norm_mod(x_ref[...], nw_ref[...], mod_ref[3:4, :], mod_ref[4:5, :])
    h_hi, h_lo = _split2(h)
    h_ref[...] = h_hi
    r_hi, r_lo = _split2(router_ref[...])
    logits = _dot(h_hi, r_hi) + _dot(h_hi, r_lo) + _dot(h_lo, r_hi)
    neg = -jnp.inf
    lg = jnp.where(lane < N_EXPERTS, logits, neg)
    m1 = jnp.max(lg, axis=-1, keepdims=True)
    i1 = jnp.min(jnp.where(lg == m1, lane, LANES), axis=-1, keepdims=True)
    lg2 = jnp.where(lane == i1, neg, lg)
    m2 = jnp.max(lg2, axis=-1, keepdims=True)
    i2 = jnp.min(jnp.where(lg2 == m2, lane, LANES), axis=-1, keepdims=True)
    e2 = jnp.exp(m2 - m1)
    w1 = 1.0 / (1.0 + e2)
    w2 = e2 / (1.0 + e2)
    chosen = jnp.logical_or(lane == i1, lane == i2)
    mask = jnp.where(chosen, 1.0, 0.0)
    pos = _dot(tri_ref[...], mask.astype(BF16)) + carry_ref[0:1, :]
    carry_ref[0:1, :] = carry_ref[0:1, :] + jnp.sum(mask, axis=0, keepdims=True)
    rec = jnp.where(lane < N_EXPERTS, pos, 0.0)
    rec = jnp.where(lane == ROUTE_W1, w1, rec)
    rec = jnp.where(lane == ROUTE_W2, w2, rec)
    rec = jnp.where(lane == ROUTE_E1, i1.astype(F32), rec)
    rec = jnp.where(lane == ROUTE_E2, i2.astype(F32), rec)
    route_ref[...] = rec
    cnt_ref[...] = carry_ref[...]


def _slots_kernel(route_ref, start_ref, o_ref):
    rec = route_ref[...]
    lane = lax.broadcasted_iota(jnp.int32, rec.shape, 1)
    slot = rec + start_ref[...]
    e1 = rec[:, ROUTE_E1:ROUTE_E1 + 1].astype(jnp.int32)
    e2 = rec[:, ROUTE_E2:ROUTE_E2 + 1].astype(jnp.int32)
    d1 = jnp.sum(jnp.where(lane == e1, slot, 0.0), axis=-1, keepdims=True)
    d2 = jnp.sum(jnp.where(lane == e2, slot, 0.0), axis=-1, keepdims=True)
    o_ref[...] = jnp.where(lane == 0, d1, jnp.where(lane == 1, d2, 0.0)).astype(jnp.int32)


def _row_copy(src_ref, dst_ref, src_row, dst_row, sem):
    return pltpu.make_async_copy(src_ref.at[src_row], dst_ref.at[dst_row], sem)


def _dispatch_kernel(d1_ref, d2_ref, h_ref, zeros_ref, o_ref, sem):
    del zeros_ref
    base = pl.program_id(0) * DMA_ROWS

    def issue(r, carry):
        _row_copy(h_ref, o_ref, base + r, d1_ref[r], sem).start()
        _row_copy(h_ref, o_ref, base + r, d2_ref[r], sem).start()
        return carry

    def drain(r, carry):
        _row_copy(h_ref, o_ref, 0, 0, sem).wait()
        _row_copy(h_ref, o_ref, 0, 0, sem).wait()
        return carry

    lax.fori_loop(0, DMA_ROWS, issue, 0)
    lax.fori_loop(0, DMA_ROWS, drain, 0)


def _collect_kernel(d1_ref, d2_ref, y_ref, o_ref, sem):
    base = pl.program_id(0) * DMA_ROWS

    def issue(r, carry):
        _row_copy(y_ref, o_ref.at[0], d1_ref[r], base + r, sem).start()
        _row_copy(y_ref, o_ref.at[1], d2_ref[r], base + r, sem).start()
        return carry

    def drain(r, carry):
        _row_copy(y_ref, o_ref.at[0], 0, 0, sem).wait()
        _row_copy(y_ref, o_ref.at[0], 0, 0, sem).wait()
        return carry

    lax.fori_loop(0, DMA_ROWS, issue, 0)
    lax.fori_loop(0, DMA_ROWS, drain, 0)


def _expert_up_kernel(te_ref, nu_ref, x_ref, wg_ref, wu_ref, o_ref):
    del te_ref
    live = pl.program_id(1) < nu_ref[0]

    @pl.when(live)
    def _():
        x = x_ref[...]
        o_ref[...] = (_silu(_dot(x, wg_ref[...])) * _dot(x, wu_ref[...])).astype(o_ref.dtype)

    @pl.when(jnp.logical_not(live))
    def _():
        o_ref[...] = jnp.zeros_like(o_ref)


def _expert_down_kernel(te_ref, nu_ref, h_ref, wd_ref, o_ref):
    del te_ref
    live = pl.program_id(1) < nu_ref[0]

    @pl.when(live)
    def _():
        o_ref[...] = _dot(h_ref[...], wd_ref[...]).astype(o_ref.dtype)

    @pl.when(jnp.logical_not(live))
    def _():
        o_ref[...] = jnp.zeros_like(o_ref)


def _combine_kernel(x_ref, y1_ref, y2_ref, route_ref, mod_ref, o_ref):
    rec = route_ref[...]
    w1 = rec[:, ROUTE_W1:ROUTE_W1 + 1]
    w2 = rec[:, ROUTE_W2:ROUTE_W2 + 1]
    y = w1 * y1_ref[...].astype(F32) + w2 * y2_ref[...].astype(F32)
    o_ref[...] = x_ref[...] + mod_ref[5:6, :] * y


def _moe_sparse(geo, xs, mod_l, nw, router, w_gate, w_up, w_down, tile0, n_tiles):
    tm, tg = geo.tm, MOE_ROW_TILE
    rows = n_tiles * tm
    n_slots = 2 * rows + N_EXPERTS * tg
    n_gt = n_slots // tg
    row_tile = pl.BlockSpec((tm, D_MODEL), lambda i: (i, 0))
    rec_tile = pl.BlockSpec((tm, LANES), lambda i: (i, 0))

    h2, route, counts = pl.pallas_call(
        _route_kernel,
        grid=(n_tiles,),
        in_specs=[
            pl.BlockSpec((tm, D_MODEL), lambda i: (tile0 + i, 0)),
            pl.BlockSpec((None, 6, D_MODEL), lambda i: (geo.mod_row(tile0 + i), 0, 0)),
            pl.BlockSpec((1, D_MODEL), lambda i: (0, 0)),
            pl.BlockSpec((D_MODEL, LANES), lambda i: (0, 0)),
        ],
        out_specs=[row_tile, rec_tile, pl.BlockSpec((8, LANES), lambda i: (0, 0))],
        out_shape=[
            jax.ShapeDtypeStruct((rows, D_MODEL), F32),
            jax.ShapeDtypeStruct((rows, LANES), F32),
            jax.ShapeDtypeStruct((8, LANES), F32),
        ],
        scratch_shapes=[pltpu.VMEM((tm, tm), BF16), pltpu.VMEM((8, LANES), F32)],
        compiler_params=_params(1),
        name="moe_route",
    )(xs, mod_l, nw, router)

    cnt = counts[0, :N_EXPERTS].astype(jnp.int32)
    padded = ((cnt + tg - 1) // tg) * tg
    ends = jnp.cumsum(padded)
    starts = ends - padded
    n_used = (ends[-1] // tg).astype(jnp.int32).reshape(1)
    tile_expert = jnp.searchsorted(ends, jnp.minimum(jnp.arange(n_gt), n_used[0] - 1) * tg, side="right")
    tile_expert = jnp.minimum(tile_expert, N_EXPERTS - 1).astype(jnp.int32)
    start_row = jnp.zeros((1, LANES), F32).at[0, :N_EXPERTS].set(starts.astype(F32))

    dest = pl.pallas_call(
        _slots_kernel,
        grid=(n_tiles,),
        in_specs=[rec_tile, pl.BlockSpec((1, LANES), lambda i: (0, 0))],
        out_specs=rec_tile,
        out_shape=jax.ShapeDtypeStruct((rows, LANES), jnp.int32),
        compiler_params=_params(1),
        name="moe_slots",
    )(route, start_row)
    d1, d2 = dest[:, 0], dest[:, 1]

    idx_spec = pl.BlockSpec((DMA_ROWS,), lambda i: (i,), memory_space=pltpu.SMEM)
    any_spec = pl.BlockSpec(memory_space=pl.ANY)
    xg = pl.pallas_call(
        _dispatch_kernel,
        grid=(rows // DMA_ROWS,),
        in_specs=[idx_spec, idx_spec, any_spec, any_spec],
        out_specs=any_spec,
        out_shape=jax.ShapeDtypeStruct((n_slots, 1, D_MODEL), F32),
        scratch_shapes=[pltpu.SemaphoreType.DMA(())],
        input_output_aliases={3: 0},
        compiler_params=_params(1),
        name="moe_dispatch",
    )(d1, d2, h2.reshape(rows, 1, D_MODEL), jnp.zeros((n_slots, 1, D_MODEL), F32))
    xg = xg.reshape(n_slots, D_MODEL)

    tf = 1024
    hg = pl.pallas_call(
        _expert_up_kernel,
        grid_spec=pltpu.PrefetchScalarGridSpec(
            num_scalar_prefetch=2,
            grid=(MOE_D_FF // tf, n_gt),
            in_specs=[
                pl.BlockSpec((tg, D_MODEL), lambda j, r, te, nu: (r, 0)),
                pl.BlockSpec((None, D_MODEL, tf), lambda j, r, te, nu: (te[r], 0, j)),
                pl.BlockSpec((None, D_MODEL, tf), lambda j, r, te, nu: (te[r], 0, j)),
            ],
            out_specs=pl.BlockSpec((tg, tf), lambda j, r, te, nu: (r, j)),
        ),
        out_shape=jax.ShapeDtypeStruct((n_slots, MOE_D_FF), BF16),
        compiler_params=_params(2),
        name="moe_expert_up",
    )(tile_expert, n_used, xg, w_gate, w_up)

    tn = 1024
    yg = pl.pallas_call(
        _expert_down_kernel,
        grid_spec=pltpu.PrefetchScalarGridSpec(
            num_scalar_prefetch=2,
            grid=(D_MODEL // tn, n_gt),
            in_specs=[
                pl.BlockSpec((tg, MOE_D_FF), lambda j, r, te, nu: (r, 0)),
                pl.BlockSpec((None, MOE_D_FF, tn), lambda j, r, te, nu: (te[r], 0, j)),
            ],
            out_specs=pl.BlockSpec((tg, tn), lambda j, r, te, nu: (r, j)),
        ),
        out_shape=jax.ShapeDtypeStruct((n_slots, D_MODEL), F32),
        compiler_params=_params(2),
        name="moe_expert_down",
    )(tile_expert, n_used, hg, w_down)

    y12 = pl.pallas_call(
        _collect_kernel,
        grid=(rows // DMA_ROWS,),
        in_specs=[idx_spec, idx_spec, any_spec],
        out_specs=any_spec,
        out_shape=jax.ShapeDtypeStruct((2, rows, 1, D_MODEL), F32),
        scratch_shapes=[pltpu.SemaphoreType.DMA(())],
        compiler_params=_params(1),
        name="moe_collect",
    )(d1, d2, yg.reshape(n_slots, 1, D_MODEL))
    y12 = y12.reshape(2, rows, D_MODEL)

    sub = 2 if tm >= 1024 else 1
    tc = tm // sub
    return pl.pallas_call(
        _combine_kernel,
        grid=(n_tiles * sub,),
        in_specs=[
            pl.BlockSpec((tc, D_MODEL), lambda i: (tile0 * sub + i, 0)),
            pl.BlockSpec((None, tc, D_MODEL), lambda i: (0, i, 0)),
            pl.BlockSpec((None, tc, D_MODEL), lambda i: (1, i, 0)),
            pl.BlockSpec((tc, LANES), lambda i: (i, 0)),
            pl.BlockSpec((None, 6, D_MODEL), lambda i: (geo.mod_row(tile0 + i // sub), 0, 0)),
        ],
        out_specs=pl.BlockSpec((tc, D_MODEL), lambda i: (i, 0)),
        out_shape=jax.ShapeDtypeStruct((rows, D_MODEL), F32),
        compiler_params=_params(1),
        name="moe_combine",
    )(xs, y12, y12, route, mod_l)


MOE_TOKEN_BLOCK = 512


def _dispatch_mm_kernel(rp_ref, sbp_ref, fp_ref, np_ref, dt_ref, h_ref, o_ref):
    p = pl.program_id(0)
    tg, tb = o_ref.shape[0], h_ref.shape[0]

    @pl.when(p < np_ref[0])
    def _():
        slot = rp_ref[p] * tg + lax.broadcasted_iota(jnp.int32, (tg, tb), 0)
        d = dt_ref[...]
        hit = jnp.logical_or(d[0:1, :] == slot, d[1:2, :] == slot)
        sel = jnp.where(hit, 1.0, 0.0).astype(BF16)
        rows = _dot(sel, h_ref[...])

        @pl.when(fp_ref[p] == 1)
        def _():
            o_ref[...] = rows.astype(o_ref.dtype)

        @pl.when(fp_ref[p] == 0)
        def _():
            o_ref[...] = (o_ref[...].astype(F32) + rows).astype(o_ref.dtype)


def _collect_mm_kernel(sbp_ref, rp_ref, fp_ref, lp_ref, np_ref, x_ref, dest_ref, route_ref, y_ref, mod_ref, o_ref):
    p = pl.program_id(0)
    tb, tg = o_ref.shape[0], y_ref.shape[0]

    @pl.when(p < np_ref[0])
    def _():
        slot = rp_ref[p] * tg + lax.broadcasted_iota(jnp.int32, (tb, tg), 1)
        dest = dest_ref[...]
        rec = route_ref[...]
        sel = (jnp.where(dest[:, 0:1] == slot, rec[:, ROUTE_W1:ROUTE_W1 + 1], 0.0)
               + jnp.where(dest[:, 1:2] == slot, rec[:, ROUTE_W2:ROUTE_W2 + 1], 0.0))
        part = _dot(sel.astype(BF16), y_ref[...])

        @pl.when(fp_ref[p] == 1)
        def _():
            o_ref[...] = part

        @pl.when(fp_ref[p] == 0)
        def _():
            o_ref[...] += part

        @pl.when(lp_ref[p] == 1)
        def _():
            o_ref[...] = x_ref[...] + mod_ref[5:6, :] * o_ref[...]


def _moe_sparse2(geo, xs, mod_l, nw, router, w_gate, w_up, w_down, tile0, n_tiles):
    tm, tg = geo.tm, MOE_ROW_TILE
    tb = min(MOE_TOKEN_BLOCK, tm)
    rows = n_tiles * tm
    nb = rows // tb
    n_slots = 2 * rows + N_EXPERTS * tg
    n_gt = n_slots // tg
    n_pairs_max = n_gt + nb * N_EXPERTS
    rec_tile = pl.BlockSpec((tm, LANES), lambda i: (i, 0))

    h2, route, counts = pl.pallas_call(
        _route_kernel,
        grid=(n_tiles,),
        in_specs=[
            pl.BlockSpec((tm, D_MODEL), lambda i: (tile0 + i, 0)),
            pl.BlockSpec((None, 6, D_MODEL), lambda i: (geo.mod_row(tile0 + i), 0, 0)),
            pl.BlockSpec((1, D_MODEL), lambda i: (0, 0)),
            pl.BlockSpec((D_MODEL, LANES), lambda i: (0, 0)),
        ],
        out_specs=[pl.BlockSpec((tm, D_MODEL), lambda i: (i, 0)), rec_tile, pl.BlockSpec((8, LANES), lambda i: (0, 0))],
        out_shape=[
            jax.ShapeDtypeStruct((rows, D_MODEL), BF16),
            jax.ShapeDtypeStruct((rows, LANES), F32),
            jax.ShapeDtypeStruct((8, LANES), F32),
        ],
        scratch_shapes=[pltpu.VMEM((tm, tm), BF16), pltpu.VMEM((8, LANES), F32)],
        compiler_params=_params(1),
        name="moe_route",
    )(xs, mod_l, nw, router)

    cnt = counts[0, :N_EXPERTS].astype(jnp.int32)
    padded = ((cnt + tg - 1) // tg) * tg
    ends = jnp.cumsum(padded)
    starts = ends - padded
    n_used = (ends[-1] // tg).astype(jnp.int32).reshape(1)
    tile_expert = jnp.searchsorted(ends, jnp.minimum(jnp.arange(n_gt), n_used[0] - 1) * tg, side="right")
    tile_expert = jnp.minimum(tile_expert, N_EXPERTS - 1).astype(jnp.int32)
    start_row = jnp.zeros((1, LANES), F32).at[0, :N_EXPERTS].set(starts.astype(F32))

    dest = pl.pallas_call(
        _slots_kernel,
        grid=(n_tiles,),
        in_specs=[rec_tile, pl.BlockSpec((1, LANES), lambda i: (0, 0))],
        out_specs=rec_tile,
        out_shape=jax.ShapeDtypeStruct((rows, LANES), jnp.int32),
        compiler_params=_params(1),
        name="moe_slots",
    )(route, start_row)

    pos_lo = route.reshape(nb, tb, LANES)[:, 0, :N_EXPERTS].astype(jnp.int32)
    pos_hi = jnp.concatenate([pos_lo[1:], cnt[None, :]], axis=0)
    t_lo = (starts[None, :] + pos_lo) // tg
    t_hi = (starts[None, :] + pos_hi - 1) // tg
    tiles = jnp.arange(n_gt)[:, None, None]
    share = jnp.any((pos_hi > pos_lo)[None] & (t_lo[None] <= tiles) & (tiles <= t_hi[None]), axis=-1)
    n_pairs = jnp.sum(share).astype(jnp.int32).reshape(1)
    last_valid = jnp.minimum(jnp.arange(n_pairs_max), n_pairs[0] - 1)

    def pair_list(mat):
        flat = jnp.nonzero(mat.ravel(), size=n_pairs_max, fill_value=0)[0][last_valid]
        major, minor = (flat // mat.shape[1]).astype(jnp.int32), (flat % mat.shape[1]).astype(jnp.int32)
        first = jnp.concatenate([jnp.ones((1,), jnp.int32), (major[1:] != major[:-1]).astype(jnp.int32)])
        last = jnp.concatenate([(major[1:] != major[:-1]).astype(jnp.int32), jnp.ones((1,), jnp.int32)])
        last = jnp.where(jnp.arange(n_pairs_max) == n_pairs[0] - 1, 1, last)
        return major, minor, first, last

    d_r, d_sb, d_first, _ = pair_list(share)
    c_sb, c_r, c_first, c_last = pair_list(share.T)
    dest_t = dest[:, :2].T

    xg = pl.pallas_call(
        _dispatch_mm_kernel,
        grid_spec=pltpu.PrefetchScalarGridSpec(
            num_scalar_prefetch=4,
            grid=(n_pairs_max,),
            in_specs=[
                pl.BlockSpec((2, tb), lambda p, rp, sbp, fp, npr: (0, sbp[p])),
                pl.BlockSpec((tb, D_MODEL), lambda p, rp, sbp, fp, npr: (sbp[p], 0)),
            ],
            out_specs=pl.BlockSpec((tg, D_MODEL), lambda p, rp, sbp, fp, npr: (rp[p], 0)),
        ),
        out_shape=jax.ShapeDtypeStruct((n_slots, D_MODEL), BF16),
        compiler_params=_params(1),
        name="moe_dispatch",
    )(d_r, d_sb, d_first, n_pairs, dest_t, h2)

    tf = 1024
    hg = pl.pallas_call(
        _expert_up_kernel,
        grid_spec=pltpu.PrefetchScalarGridSpec(
            num_scalar_prefetch=2,
            grid=(MOE_D_FF // tf, n_gt),
            in_specs=[
                pl.BlockSpec((tg, D_MODEL), lambda j, r, te, nu: (r, 0)),
                pl.BlockSpec((None, D_MODEL, tf), lambda j, r, te, nu: (te[r], 0, j)),
                pl.BlockSpec((None, D_MODEL, tf), lambda j, r, te, nu: (te[r], 0, j)),
            ],
            out_specs=pl.BlockSpec((tg, tf), lambda j, r, te, nu: (r, j)),
        ),
        out_shape=jax.ShapeDtypeStruct((n_slots, MOE_D_FF), BF16),
        compiler_params=_params(2),
        name="moe_expert_up",
    )(tile_expert, n_used, xg, w_gate, w_up)

    tn = 1024
    yg = pl.pallas_call(
        _expert_down_kernel,
        grid_spec=pltpu.PrefetchScalarGridSpec(
            num_scalar_prefetch=2,
            grid=(D_MODEL // tn, n_gt),
            in_specs=[
                pl.BlockSpec((tg, MOE_D_FF), lambda j, r, te, nu: (r, 0)),
                pl.BlockSpec((None, MOE_D_FF, tn), lambda j, r, te, nu: (te[r], 0, j)),
            ],
            out_specs=pl.BlockSpec((tg, tn), lambda j, r, te, nu: (r, j)),
        ),
        out_shape=jax.ShapeDtypeStruct((n_slots, D_MODEL), BF16),
        compiler_params=_params(2),
        name="moe_expert_down",
    )(tile_expert, n_used, hg, w_down)

    per = tm // tb
    return pl.pallas_call(
        _collect_mm_kernel,
        grid_spec=pltpu.PrefetchScalarGridSpec(
            num_scalar_prefetch=5,
            grid=(n_pairs_max,),
            in_specs=[
                pl.BlockSpec((tb, D_MODEL), lambda p, sbp, rp, fp, lp, npr: (tile0 * per + sbp[p], 0)),
                pl.BlockSpec((tb, LANES), lambda p, sbp, rp, fp, lp, npr: (sbp[p], 0)),
                pl.BlockSpec((tb, LANES), lambda p, sbp, rp, fp, lp, npr: (sbp[p], 0)),
                pl.BlockSpec((tg, D_MODEL), lambda p, sbp, rp, fp, lp, npr: (rp[p], 0)),
                pl.BlockSpec((None, 6, D_MODEL), lambda p, sbp, rp, fp, lp, npr: (geo.mod_row(tile0 + sbp[p] // per), 0, 0)),
            ],
            out_specs=pl.BlockSpec((tb, D_MODEL), lambda p, sbp, rp, fp, lp, npr: (sbp[p], 0)),
        ),
        out_shape=jax.ShapeDtypeStruct((rows, D_MODEL), F32),
        compiler_params=_params(1),
        name="moe_collect",
    )(c_sb, c_r, c_first, c_last, n_pairs, xs, dest, route, yg, mod_l)


def _final_norm_kernel(x_ref, w_ref, o_ref):
    x = x_ref[...]
    o_ref[...] = x * lax.rsqrt(jnp.mean(x * x, axis=-1, keepdims=True) + EPS) * w_ref[...]


def _final_norm(x, w, tm):
    rows = x.shape[0]
    return pl.pallas_call(
        _final_norm_kernel,
        grid=(rows // tm,),
        in_specs=[pl.BlockSpec((tm, D_MODEL), lambda i: (i, 0)), pl.BlockSpec((1, D_MODEL), lambda i: (0, 0))],
        out_specs=pl.BlockSpec((tm, D_MODEL), lambda i: (i, 0)),
        out_shape=jax.ShapeDtypeStruct((rows, D_MODEL), F32),
        compiler_params=_params(1),
        name="final_norm",
    )(x, w)


def _rope_tables(geo):
    half = RET_DK // 4
    inv = ROPE_BASE ** (-jnp.arange(half, dtype=F32) / half)
    pos = jnp.arange(geo.seq)
    ang_r = (pos // GRID_W).astype(F32)[:, None] * inv[None, :]
    ang_c = (pos % GRID_W).astype(F32)[:, None] * inv[None, :]
    cos = jnp.concatenate([jnp.cos(ang_r), jnp.cos(ang_r), jnp.cos(ang_c), jnp.cos(ang_c)], axis=1)
    sin = jnp.concatenate([-jnp.sin(ang_r), jnp.sin(ang_r), -jnp.sin(ang_c), jnp.sin(ang_c)], axis=1)
    cos = jnp.concatenate([jnp.ones((geo.tm, LANES), F32), cos], axis=0)
    sin = jnp.concatenate([jnp.zeros((geo.tm, LANES), F32), sin], axis=0)
    return cos, sin


_DT_PERM = np.array([d * SSM_HEADS + g * SSM_HPG + h
                     for g in range(SSM_GROUPS) for d in range(2) for h in range(SSM_HPG)])


def kernel(x, c, ctx, c_ctx, w_ada, b_ada, norm1_w, norm2_w, w_in, conv_w, conv_b, ret_decay_f, ret_decay_b, ret_gn_w, ssm_a_log_f, ssm_a_log_b, ssm_dt_bias_f, ssm_dt_bias_b, ssm_d, ssm_norm_w, w_ret_proj, w_ssm_proj, w_out, ffn_w_gate, ffn_w_up, ffn_w_down, moe_router, moe_w_gate, moe_w_up, moe_w_down, final_norm_w):
    batch, seq, d = x.shape
    ctx_len = ctx.shape[1]
    depth = w_ada.shape[0]
    assert d == D_MODEL and seq % GRID_W == 0
    geo = _Geom(batch, ctx_len, seq)
    tm = geo.tm

    mod_rows = -(-(batch + 1) // 8) * 8
    cvec = jnp.zeros((mod_rows, d), F32).at[0].set(c_ctx).at[1:batch + 1].set(c)
    mod = _modulation(cvec, w_ada, b_ada).reshape(depth, mod_rows, 6, d)
    cos_t, sin_t = _rope_tables(geo)

    xs = jnp.concatenate([ctx.reshape(batch * ctx_len, d), x.reshape(batch * seq, d)], axis=0)
    lat_tile0 = geo.n_ctx_tiles
    n_lat_tiles = geo.n_tiles - geo.n_ctx_tiles

    for i in range(depth):
        mod_l = mod[i]
        w_main = jnp.concatenate([w_in[i][:, ORIG_Z_LO:ORIG_Z_HI], w_in[i][:, :ORIG_Z_LO],
                                  w_in[i][:, ORIG_Z_HI:ORIG_DT_LO], w_in[i][:, ORIG_DT_HI:]], axis=1).astype(BF16)
        w_dt = w_in[i][:, ORIG_DT_LO:ORIG_DT_HI][:, _DT_PERM].astype(BF16)
        proj, dt = _inproj(geo, xs, mod_l, norm1_w[i][None, :], w_main, w_dt, cos_t, sin_t)

        xbc = _conv_silu(geo, proj, conv_w[i], conv_b[i][None, :])
        bias = jnp.concatenate([ssm_dt_bias_f[i], ssm_dt_bias_b[i]])[_DT_PERM][None, :]
        alog = jnp.concatenate([ssm_a_log_f[i], ssm_a_log_b[i]])[_DT_PERM][None, :]
        cc, cct, cpt, dtt = _decay_tables(geo, dt, bias, alog)
        dskip = jnp.repeat(ssm_d[i], SSM_INNER // SSM_HEADS)[None, :]
        ys = _ssd_scan2(geo, proj, xbc, cc, cct, cpt, dtt, dskip, ssm_norm_w[i][None, :])
        yr = _ret_scan2(geo, proj, jnp.stack([ret_decay_f[i], ret_decay_b[i]]), ret_gn_w[i][None, :])

        merged = _merge(geo, yr, ys, w_ret_proj[i].astype(BF16), w_ssm_proj[i].astype(BF16), proj)
        xs = _matmul_residual(geo, merged, w_out[i].astype(BF16), xs, mod_l, 2, 0, geo.n_tiles, 1024)

        j = i // 2
        tile0, n_tiles = (0, geo.n_tiles) if i < depth - 1 else (lat_tile0, n_lat_tiles)
        if i % 2 == 0:
            hid = _ffn_up(geo, xs, mod_l, norm2_w[i][None, :], ffn_w_gate[j].astype(BF16), ffn_w_up[j].astype(BF16))
            hid = hid[tile0 * tm:]
            xs = _matmul_residual(geo, hid, ffn_w_down[j].astype(BF16), xs, mod_l, 5, tile0, n_tiles, 512)
        else:
            router = jnp.zeros((d, LANES), F32).at[:, :N_EXPERTS].set(moe_router[j])
            xs = _moe_sparse2(geo, xs, mod_l, norm2_w[i][None, :], router, moe_w_gate[j].astype(BF16),
                              moe_w_up[j].astype(BF16), moe_w_down[j].astype(BF16), tile0, n_tiles)

    lat = xs if xs.shape[0] == batch * seq else xs[batch * ctx_len:]
    return _final_norm(lat, final_norm_w[None, :], tm).reshape(batch, seq, d)
```

```python
import functools

import numpy as np
import jax
import jax.numpy as jnp
from jax import lax
from jax.experimental import pallas as pl
from jax.experimental.pallas import tpu as pltpu

F32 = jnp.float32
BF16 = jnp.bfloat16

D_MODEL = 2048
GRID_W = 64
CHUNK = 128
EPS = 1e-6
ROPE_BASE = 10000.0
RET_HEADS = 8
RET_DK = 128
RET_DV = 256
SSM_INNER = 4096
SSM_HEADS = 64
SSM_GROUPS = 8
SSM_HPG = 8
SSM_GROUP_W = SSM_INNER // SSM_GROUPS
SSM_STATE = 128
XBC_W = 6144
CONV_K = 5
D_FF = 5632
N_EXPERTS = 8
MOE_D_FF = 4096

COL_Z, COL_Q, COL_K, COL_V, COL_G, COL_XBC, COL_GATES = 0, 4096, 5120, 6144, 8192, 10240, 16384
PROJ_W = 20480
ORIG_Z_LO, ORIG_Z_HI, ORIG_DT_LO, ORIG_DT_HI = 6144, 10240, 16384, 16512
SCAN_ROWS = 256

LANES = 128
VMEM_LIMIT_BYTES = 56 * 1024 * 1024


def _params(n_axes, vmem=VMEM_LIMIT_BYTES):
    return pltpu.CompilerParams(dimension_semantics=("arbitrary",) * n_axes, vmem_limit_bytes=vmem)


def _silu(x):
    return x * jax.nn.sigmoid(x)


def _split2(x):
    hi = x.astype(BF16)
    lo = (x - hi.astype(F32)).astype(BF16)
    return hi, lo


def _split3(x):
    hi = x.astype(BF16)
    r = x - hi.astype(F32)
    mid = r.astype(BF16)
    lo = (r - mid.astype(F32)).astype(BF16)
    return hi, mid, lo


def _dot(a, b):
    return jnp.dot(a, b, preferred_element_type=F32)


def _dot_nt(a, b):
    return lax.dot_general(a, b, (((1,), (1,)), ((), ())), preferred_element_type=F32)


def _dot_tn(a, b):
    return lax.dot_general(a, b, (((0,), (0,)), ((), ())), preferred_element_type=F32)


def _norm_mod(x, nw, shift, scale):
    y = x * lax.rsqrt(jnp.mean(x * x, axis=-1, keepdims=True) + EPS)
    return (y * nw) * (1.0 + scale) + shift


def _mod_kernel(c_ref, w_ref, b_ref, o_ref):
    s_hi, s_lo = _split2(_silu(c_ref[...]))
    w_hi, w_lo = _split2(w_ref[...])
    o_ref[...] = _dot(s_hi, w_hi) + _dot(s_hi, w_lo) + _dot(s_lo, w_hi) + b_ref[...]


def _modulation(cvec, w_ada, b_ada):
    depth, d, w6 = w_ada.shape
    rows = cvec.shape[0]
    tn = 512
    return pl.pallas_call(
        _mod_kernel,
        grid=(depth, w6 // tn),
        in_specs=[
            pl.BlockSpec((rows, d), lambda l, j: (0, 0)),
            pl.BlockSpec((None, d, tn), lambda l, j: (l, 0, j)),
            pl.BlockSpec((None, 1, tn), lambda l, j: (l, 0, j)),
        ],
        out_specs=pl.BlockSpec((None, rows, tn), lambda l, j: (l, 0, j)),
        out_shape=jax.ShapeDtypeStruct((depth, rows, w6), F32),
        compiler_params=_params(2),
        name="adaln_mod",
    )(cvec, w_ada, b_ada.reshape(depth, 1, w6))


class _Geom:
    def __init__(self, batch, ctx_len, seq):
        self.batch, self.ctx_len, self.seq = batch, ctx_len, seq
        tm = 1024
        while (batch * ctx_len) % tm or seq % tm:
            tm //= 2
        assert tm >= CHUNK and ctx_len % CHUNK == 0 and seq % CHUNK == 0
        self.tm = tm
        self.ctx_rows = batch * ctx_len
        self.rows = self.ctx_rows + batch * seq
        self.n_ctx_tiles = self.ctx_rows // tm
        self.tiles_per_batch = seq // tm
        self.n_tiles = self.rows // tm
        self.step_rows = min(SCAN_ROWS, ctx_len)
        assert ctx_len % self.step_rows == 0 and seq % self.step_rows == 0 and self.step_rows % CHUNK == 0
        self.ns_ctx = ctx_len // self.step_rows
        self.ns_lat = seq // self.step_rows
        self.ns = self.ns_ctx + self.ns_lat

    def mod_row(self, i):
        return jnp.where(i < self.n_ctx_tiles, 0, 1 + (i - self.n_ctx_tiles) // self.tiles_per_batch)

    def rope_block(self, i):
        return jnp.where(i < self.n_ctx_tiles, 0, 1 + (i - self.n_ctx_tiles) % self.tiles_per_batch)

    def step_block(self, b, c):
        return jnp.where(c < self.ns_ctx, b * self.ns_ctx + c,
                         self.batch * self.ns_ctx + b * self.ns_lat + (c - self.ns_ctx))

    def backward_order(self, s):
        return jnp.where(s < self.ns_ctx, self.ns_ctx - 1 - s, self.ns - 1 - (s - self.ns_ctx))


def _rope_store(acc, cos, sin, o_ref, scale):
    lane = lax.broadcasted_iota(jnp.int32, (acc.shape[0], LANES), 1)
    first_half = (lane % 64) < 32
    for h in range(acc.shape[1] // LANES):
        xh = acc[:, h * LANES:(h + 1) * LANES]
        partner = jnp.where(first_half, pltpu.roll(xh, 96, 1), pltpu.roll(xh, 32, 1))
        o_ref[:, h * LANES:(h + 1) * LANES] = ((xh * cos + partner * sin) * scale).astype(o_ref.dtype)


def _inproj_kernel(x_ref, mod_ref, nw_ref, w_ref, wdt_ref, cos_ref, sin_ref, o_ref, dt_ref, h_ref):
    j = pl.program_id(1)

    @pl.when(j == 0)
    def _():
        h = _norm_mod(x_ref[...], nw_ref[...], mod_ref[0:1, :], mod_ref[1:2, :]).astype(BF16)
        h_ref[...] = h
        dt_ref[...] = _dot(h, wdt_ref[...])

    acc = _dot(h_ref[...], w_ref[...])

    jq, jk = COL_Q // acc.shape[1], COL_K // acc.shape[1]

    @pl.when(j == jq)
    def _():
        _rope_store(acc, cos_ref[...], sin_ref[...], o_ref, 1.0)

    @pl.when(j == jk)
    def _():
        _rope_store(acc, cos_ref[...], sin_ref[...], o_ref, RET_DK ** -0.5)

    @pl.when(jnp.logical_and(j != jq, j != jk))
    def _():
        o_ref[...] = acc.astype(o_ref.dtype)


def _inproj(geo, xs, mod_l, nw, w_main, w_dt, cos_t, sin_t):
    tm, tn = geo.tm, 1024
    return pl.pallas_call(
        _inproj_kernel,
        grid=(geo.n_tiles, PROJ_W // tn),
        in_specs=[
            pl.BlockSpec((tm, D_MODEL), lambda i, j: (i, 0)),
            pl.BlockSpec((None, 6, D_MODEL), lambda i, j: (geo.mod_row(i), 0, 0)),
            pl.BlockSpec((1, D_MODEL), lambda i, j: (0, 0)),
            pl.BlockSpec((D_MODEL, tn), lambda i, j: (0, j)),
            pl.BlockSpec((D_MODEL, LANES), lambda i, j: (0, 0)),
            pl.BlockSpec((tm, LANES), lambda i, j: (geo.rope_block(i), 0)),
            pl.BlockSpec((tm, LANES), lambda i, j: (geo.rope_block(i), 0)),
        ],
        out_specs=[
            pl.BlockSpec((tm, tn), lambda i, j: (i, j)),
            pl.BlockSpec((tm, LANES), lambda i, j: (i, 0)),
        ],
        out_shape=[
            jax.ShapeDtypeStruct((geo.rows, PROJ_W), BF16),
            jax.ShapeDtypeStruct((geo.rows, LANES), F32),
        ],
        scratch_shapes=[pltpu.VMEM((tm, D_MODEL), BF16)],
        compiler_params=_params(2),
        name="in_proj",
    )(xs, mod_l, nw, w_main, w_dt, cos_t, sin_t)


CONV_HALO = 16


def _conv_kernel(prev_ref, main_ref, next_ref, w_ref, b_ref, o_ref, ext_ref, *, geo, rows):
    i = pl.program_id(0)
    start = i * rows
    in_ctx = start < geo.ctx_rows
    seq_len = jnp.where(in_ctx, geo.ctx_len, geo.seq)
    off = jnp.where(in_ctx, start, start - geo.ctx_rows)
    first = (off % seq_len) == 0
    last = ((off + rows) % seq_len) == 0
    ext_ref[0:CONV_HALO, :] = jnp.where(first, 0.0, prev_ref[...].astype(F32))
    ext_ref[CONV_HALO:CONV_HALO + rows, :] = main_ref[...].astype(F32)
    ext_ref[CONV_HALO + rows:, :] = jnp.where(last, 0.0, next_ref[...].astype(F32))
    acc = jnp.broadcast_to(b_ref[...], (rows, b_ref.shape[1]))
    for k in range(CONV_K):
        acc = acc + ext_ref[pl.ds(CONV_HALO + k - CONV_K // 2, rows), :] * w_ref[k:k + 1, :]
    o_ref[...] = _silu(acc).astype(o_ref.dtype)


def _conv_silu(geo, proj, conv_w, conv_b):
    rows = min(256, geo.ctx_len)
    tc = 1024
    col0 = COL_XBC // tc
    n_halo = geo.rows // CONV_HALO
    per = rows // CONV_HALO
    return pl.pallas_call(
        functools.partial(_conv_kernel, geo=geo, rows=rows),
        grid=(geo.rows // rows, XBC_W // tc),
        in_specs=[
            pl.BlockSpec((CONV_HALO, tc), lambda i, j: (jnp.maximum(i * per - 1, 0), col0 + j)),
            pl.BlockSpec((rows, tc), lambda i, j: (i, col0 + j)),
            pl.BlockSpec((CONV_HALO, tc), lambda i, j: (jnp.minimum((i + 1) * per, n_halo - 1), col0 + j)),
            pl.BlockSpec((CONV_K, tc), lambda i, j: (0, j)),
            pl.BlockSpec((1, tc), lambda i, j: (0, j)),
        ],
        out_specs=pl.BlockSpec((rows, tc), lambda i, j: (i, j)),
        out_shape=jax.ShapeDtypeStruct((geo.rows, XBC_W), BF16),
        scratch_shapes=[pltpu.VMEM((rows + 2 * CONV_HALO, tc), F32)],
        compiler_params=_params(2),
        name="ssm_conv",
    )(proj, proj, proj, conv_w, conv_b)


def _decay_kernel(dt_ref, bias_ref, alog_ref, cc_ref, cct_ref, cpt_ref, dtt_ref):
    x = dt_ref[...] + bias_ref[...]
    dtv = jnp.maximum(x, 0.0) + jnp.log1p(jnp.exp(-jnp.abs(x)))
    la = dtv * (-jnp.exp(alog_ref[...]))
    ii = lax.broadcasted_iota(jnp.int32, (CHUNK, CHUNK), 0)
    jj = lax.broadcasted_iota(jnp.int32, (CHUNK, CHUNK), 1)
    lower = (jj <= ii).astype(BF16)
    upper = (jj >= ii).astype(BF16)
    hi, mid, lo = _split3(la)
    cum_f = _dot(lower, hi) + _dot(lower, mid) + _dot(lower, lo)
    cum_b = _dot(upper, hi) + _dot(upper, mid) + _dot(upper, lo)
    lane = lax.broadcasted_iota(jnp.int32, (CHUNK, LANES), 1)
    cc = jnp.where((lane % 16) < 8, cum_f, cum_b)
    cc_ref[...] = cc
    cct_ref[...] = cc.T
    cpt_ref[...] = (cc - jnp.log(dtv)).T
    dtt_ref[...] = dtv.T


def _decay_tables(geo, dt, bias, alog):
    n_chunks = geo.rows // CHUNK
    col = pl.BlockSpec((CHUNK, LANES), lambda i: (i, 0))
    row = pl.BlockSpec((None, LANES, CHUNK), lambda i: (i, 0, 0))
    vec = pl.BlockSpec((1, LANES), lambda i: (0, 0))
    col_shape = jax.ShapeDtypeStruct((geo.rows, LANES), F32)
    row_shape = jax.ShapeDtypeStruct((n_chunks, LANES, CHUNK), F32)
    return pl.pallas_call(
        _decay_kernel,
        grid=(n_chunks,),
        in_specs=[col, vec, vec],
        out_specs=[col, row, row, row],
        out_shape=[col_shape, row_shape, row_shape, row_shape],
        compiler_params=_params(1),
        name="ssm_decay",
    )(dt, bias, alog)


def _lanes2(a):
    return jnp.concatenate([a, a], axis=1)


def _ret_tables(dec_ref, direction, rows, wk_ref, a_ref, e_ref=None):
    n = float(rows)
    row = lax.broadcasted_iota(jnp.int32, (rows, LANES), 0).astype(F32)
    for h in range(RET_HEADS):
        lam = jnp.exp(jnp.full((rows, LANES), dec_ref[direction, h], F32))
        if direction == 0:
            wk_ref[h] = jnp.exp(-lam * (n - 1.0 - row))
            if e_ref is not None:
                e_ref[0, h] = jnp.exp(-lam * (row + 1.0))
        else:
            wk_ref[h] = jnp.exp(-lam * row)
        a_ref[h] = _lanes2(jnp.exp(-lam[0:8, :] * n))


def _ret_state_kernel(dec_ref, k_ref, v_ref, hist_ref, sb_ref, wk_ref, a_ref, *, rows):
    @pl.when(pl.program_id(1) == 0)
    def _():
        sb_ref[...] = jnp.zeros_like(sb_ref)
        _ret_tables(dec_ref, 1, rows, wk_ref, a_ref)

    def body(h, carry):
        kh = k_ref[:, pl.ds(pl.multiple_of(h * RET_DK, RET_DK), RET_DK)]
        vh = v_ref[:, pl.ds(pl.multiple_of(h * RET_DV, RET_DV), RET_DV)]
        hist_ref[h] = sb_ref[h].astype(BF16)
        kw = (kh.astype(F32) * wk_ref[h]).astype(BF16)
        sb_ref[h] = a_ref[h][0:1, :] * sb_ref[h] + _dot_tn(kw, vh)
        return carry

    lax.fori_loop(0, RET_HEADS, body, 0, unroll=4)


def _ret_out_kernel(dec_ref, q_ref, k_ref, v_ref, g_ref, hist_ref, gnw_ref, o_ref,
                    sf_ref, wk_ref, a_ref, e_ref, m_ref, *, rows):
    @pl.when(pl.program_id(1) == 0)
    def _():
        sf_ref[...] = jnp.zeros_like(sf_ref)
        _ret_tables(dec_ref, 0, rows, wk_ref, a_ref, e_ref)
        n = float(rows)
        row = lax.broadcasted_iota(jnp.int32, (rows, LANES), 0).astype(F32)
        ii = lax.broadcasted_iota(jnp.int32, (rows, rows), 0)
        jj = lax.broadcasted_iota(jnp.int32, (rows, rows), 1)
        dist = (ii - jj).astype(F32)
        for h in range(RET_HEADS):
            lam_f = jnp.exp(jnp.full((rows, 1), dec_ref[0, h], F32))
            lam_b = jnp.exp(jnp.full((rows, 1), dec_ref[1, h], F32))
            m_ref[h] = (jnp.where(jj <= ii, jnp.exp(-lam_f * dist), 0.0)
                        + jnp.where(jj >= ii, jnp.exp(lam_b * dist), 0.0))
            e_ref[1, h] = jnp.exp(-jnp.exp(jnp.full((rows, LANES), dec_ref[1, h], F32)) * (n - row))

    def body(h, carry):
        ok = pl.multiple_of(h * RET_DK, RET_DK)
        ov = pl.multiple_of(h * RET_DV, RET_DV)
        qh = q_ref[:, pl.ds(ok, RET_DK)]
        kh = k_ref[:, pl.ds(ok, RET_DK)]
        vh = v_ref[:, pl.ds(ov, RET_DV)]
        probs = (_dot_nt(qh, kh) * m_ref[h]).astype(BF16)
        y = _dot(probs, vh)
        y = y + _lanes2(e_ref[0, h]) * _dot(qh, sf_ref[h].astype(BF16)) + _lanes2(e_ref[1, h]) * _dot(qh, hist_ref[h])
        kw = (kh.astype(F32) * wk_ref[h]).astype(BF16)
        sf_ref[h] = a_ref[h][0:1, :] * sf_ref[h] + _dot_tn(kw, vh)
        mu = jnp.mean(y, axis=-1, keepdims=True)
        yc = y - mu
        yn = yc * lax.rsqrt(jnp.mean(yc * yc, axis=-1, keepdims=True) + EPS)
        gate = _silu(g_ref[:, pl.ds(ov, RET_DV)].astype(F32))
        o_ref[:, pl.ds(ov, RET_DV)] = (yn * gnw_ref[:, pl.ds(ov, RET_DV)] * gate).astype(o_ref.dtype)
        return carry

    lax.fori_loop(0, RET_HEADS, body, 0, unroll=4)


def _ret_scan(geo, proj, decays, gn_w):
    rows = geo.step_rows
    qk_w, v_w = RET_HEADS * RET_DK, RET_HEADS * RET_DV

    def bwd(b, s):
        return geo.step_block(b, geo.backward_order(s))

    def fwd(b, s):
        return geo.step_block(b, s)

    smem = pl.BlockSpec(memory_space=pltpu.SMEM)
    hist_shape = (geo.batch, geo.ns, RET_HEADS, RET_DK, RET_DV)
    hist = pl.pallas_call(
        functools.partial(_ret_state_kernel, rows=rows),
        grid=(geo.batch, geo.ns),
        in_specs=[
            smem,
            pl.BlockSpec((rows, qk_w), lambda b, s: (bwd(b, s), COL_K // qk_w)),
            pl.BlockSpec((rows, v_w), lambda b, s: (bwd(b, s), COL_V // v_w)),
        ],
        out_specs=pl.BlockSpec((None, None) + hist_shape[2:], lambda b, s: (b, geo.backward_order(s), 0, 0, 0)),
        out_shape=jax.ShapeDtypeStruct(hist_shape, BF16),
        scratch_shapes=[
            pltpu.VMEM((RET_HEADS, RET_DK, RET_DV), F32),
            pltpu.VMEM((RET_HEADS, rows, LANES), F32),
            pltpu.VMEM((RET_HEADS, 8, RET_DV), F32),
        ],
        compiler_params=_params(2),
        name="ret_state",
    )(decays, proj, proj)
    return pl.pallas_call(
        functools.partial(_ret_out_kernel, rows=rows),
        grid=(geo.batch, geo.ns),
        in_specs=[
            smem,
            pl.BlockSpec((rows, qk_w), lambda b, s: (fwd(b, s), COL_Q // qk_w)),
            pl.BlockSpec((rows, qk_w), lambda b, s: (fwd(b, s), COL_K // qk_w)),
            pl.BlockSpec((rows, v_w), lambda b, s: (fwd(b, s), COL_V // v_w)),
            pl.BlockSpec((rows, v_w), lambda b, s: (fwd(b, s), COL_G // v_w)),
            pl.BlockSpec((None, None) + hist_shape[2:], lambda b, s: (b, s, 0, 0, 0)),
            pl.BlockSpec((1, v_w), lambda b, s: (0, 0)),
        ],
        out_specs=pl.BlockSpec((rows, v_w), lambda b, s: (fwd(b, s), 0)),
        out_shape=jax.ShapeDtypeStruct((geo.rows, v_w), BF16),
        scratch_shapes=[
            pltpu.VMEM((RET_HEADS, RET_DK, RET_DV), F32),
            pltpu.VMEM((RET_HEADS, rows, LANES), F32),
            pltpu.VMEM((RET_HEADS, 8, RET_DV), F32),
            pltpu.VMEM((2, RET_HEADS, rows, LANES), F32),
            pltpu.VMEM((RET_HEADS, rows, rows), F32),
        ],
        compiler_params=_params(2),
        name="ret_out",
    )(decays, proj, proj, proj, proj, hist, gn_w)


def _group_cols(ref, rows, g):
    return pltpu.roll(ref[rows, :], (LANES - 16 * g) % LANES, 1)


def _pair_rhs(left, vp):
    zero = jnp.zeros_like(vp)
    return jnp.concatenate([jnp.where(left, vp, zero), jnp.where(left, zero, vp)], axis=0)


def _state_update(s_ref, g, sl, kt, w_rows, h1, h2, a_pair, rhs):
    lhs = jnp.concatenate([(kt * w_rows[h1:h1 + 1, :]).astype(BF16), (kt * w_rows[h2:h2 + 1, :]).astype(BF16)], axis=1)
    s_ref[g, :, sl] = a_pair * s_ref[g, :, sl] + _dot(lhs, rhs)


def _ssd_state_kernel(k_ref, xs_ref, cc_ref, cct_ref, cpt_ref, hist_ref, sb_ref, *, cps):
    @pl.when(pl.program_id(1) == 0)
    def _():
        sb_ref[...] = jnp.zeros_like(sb_ref)

    left = lax.broadcasted_iota(jnp.int32, (CHUNK, LANES), 1) < 64
    left_row = lax.broadcasted_iota(jnp.int32, (1, LANES), 1) < 64

    def body(g, carry):
        ok = pl.multiple_of(g * SSM_STATE, SSM_STATE)
        ov = pl.multiple_of(g * SSM_GROUP_W, SSM_GROUP_W)
        ob = pl.multiple_of(g * 16 + SSM_HPG, SSM_HPG)
        for ci in reversed(range(cps)):
            rows = pl.ds(ci * CHUNK, CHUNK)
            a_all = jnp.exp(_group_cols(cc_ref, rows, g)[0:1, :])
            kt = k_ref[rows, pl.ds(ok, SSM_STATE)].astype(F32).T
            hist_ref[ci, g] = sb_ref[g].astype(BF16)
            w_rows = jnp.exp(cct_ref[ci, pl.ds(ob, SSM_HPG), 0:1] - cpt_ref[ci, pl.ds(ob, SSM_HPG), :])
            for p in range(SSM_HPG // 2):
                h1, h2 = 2 * p, 2 * p + 1
                sl = slice(p * LANES, (p + 1) * LANES)
                vp = xs_ref[rows, pl.ds(pl.multiple_of(ov + p * LANES, LANES), LANES)]
                ab = jnp.where(left_row, a_all[:, SSM_HPG + h1:SSM_HPG + h1 + 1], a_all[:, SSM_HPG + h2:SSM_HPG + h2 + 1])
                _state_update(sb_ref, g, sl, kt, w_rows, h1, h2, ab, _pair_rhs(left, vp))
        return carry

    lax.fori_loop(0, SSM_GROUPS, body, 0, unroll=2)


def _ssd_out_kernel(q_ref, k_ref, xs_ref, z_ref, cc_ref, cct_ref, cpt_ref, dtt_ref, hist_ref,
                    dskip_ref, nw_ref, o_ref, sf_ref, *, cps):
    @pl.when(pl.program_id(1) == 0)
    def _():
        sf_ref[...] = jnp.zeros_like(sf_ref)

    left = lax.broadcasted_iota(jnp.int32, (CHUNK, LANES), 1) < 64
    left_row = lax.broadcasted_iota(jnp.int32, (1, LANES), 1) < 64
    ii = lax.broadcasted_iota(jnp.int32, (CHUNK, CHUNK), 0)
    jj = lax.broadcasted_iota(jnp.int32, (CHUNK, CHUNK), 1)
    lower = jj <= ii
    diag = jj == ii

    def body(g, carry):
        ok = pl.multiple_of(g * SSM_STATE, SSM_STATE)
        ov = pl.multiple_of(g * SSM_GROUP_W, SSM_GROUP_W)
        og = pl.multiple_of(g * 16, 16)
        for ci in range(cps):
            rows = pl.ds(ci * CHUNK, CHUNK)
            cc = _group_cols(cc_ref, rows, g)
            cpt = cpt_ref[ci, pl.ds(og, 16), :]
            dtt = dtt_ref[ci, pl.ds(og, 16), :]
            q = q_ref[rows, pl.ds(ok, SSM_STATE)]
            kt = k_ref[rows, pl.ds(ok, SSM_STATE)].astype(F32).T
            scores = _dot(q, kt.astype(BF16))
            yf = _dot(q, sf_ref[g].astype(BF16))
            yb = _dot(q, hist_ref[ci, g])
            w_rows = jnp.exp(cct_ref[ci, pl.ds(og, SSM_HPG), CHUNK - 1:CHUNK] - cpt[0:SSM_HPG, :])
            a_all = jnp.exp(cc[CHUNK - 1:CHUNK, :])
            ys = []
            for p in range(SSM_HPG // 2):
                h1, h2 = 2 * p, 2 * p + 1
                sl = slice(p * LANES, (p + 1) * LANES)
                cols = pl.ds(pl.multiple_of(ov + p * LANES, LANES), LANES)
                vp = xs_ref[rows, cols]
                probs, ef, eb = [], [], []
                for h in (h1, h2):
                    hb = SSM_HPG + h
                    cf = jnp.broadcast_to(cc[:, h:h + 1], (CHUNK, CHUNK))
                    cb = jnp.broadcast_to(cc[:, hb:hb + 1], (CHUNK, CHUNK))
                    arg = jnp.where(lower, cf - cpt[h:h + 1, :], cb - cpt[hb:hb + 1, :])
                    e = jnp.exp(arg) + jnp.where(diag, dtt[hb:hb + 1, :], 0.0)
                    probs.append((scores * e).astype(BF16))
                    ef.append(jnp.exp(cf))
                    eb.append(jnp.exp(cb))
                rhs = _pair_rhs(left, vp)
                y = _dot(jnp.concatenate(probs, axis=1), rhs)
                y = (y + jnp.where(left, ef[0], ef[1]) * yf[:, sl] + jnp.where(left, eb[0], eb[1]) * yb[:, sl])
                af = jnp.where(left_row, a_all[:, h1:h1 + 1], a_all[:, h2:h2 + 1])
                _state_update(sf_ref, g, sl, kt, w_rows, h1, h2, af, rhs)
                ys.append(y + dskip_ref[:, cols] * vp.astype(F32))
            gcols = pl.ds(ov, SSM_GROUP_W)
            y = jnp.concatenate(ys, axis=1) * _silu(z_ref[rows, gcols].astype(F32))
            y = y * lax.rsqrt(jnp.mean(y * y, axis=-1, keepdims=True) + EPS)
            o_ref[rows, gcols] = (y * nw_ref[:, gcols]).astype(o_ref.dtype)
        return carry

    lax.fori_loop(0, SSM_GROUPS, body, 0)


def _ssd_scan(geo, proj, xbc, cc, cct, cpt, dtt, dskip, norm_w):
    rows = geo.step_rows
    cps = rows // CHUNK
    bc_w = SSM_GROUPS * SSM_STATE

    def bwd(b, s):
        return geo.step_block(b, geo.backward_order(s))

    def fwd(b, s):
        return geo.step_block(b, s)

    hist_shape = (geo.batch, geo.ns * cps, SSM_GROUPS, SSM_STATE, SSM_GROUP_W)
    hist_block = (None, cps) + hist_shape[2:]
    row_bwd = pl.BlockSpec((cps, LANES, CHUNK), lambda b, s: (bwd(b, s), 0, 0))
    hist = pl.pallas_call(
        functools.partial(_ssd_state_kernel, cps=cps),
        grid=(geo.batch, geo.ns),
        in_specs=[
            pl.BlockSpec((rows, bc_w), lambda b, s: (bwd(b, s), SSM_INNER // bc_w)),
            pl.BlockSpec((rows, SSM_INNER), lambda b, s: (bwd(b, s), 0)),
            pl.BlockSpec((rows, LANES), lambda b, s: (bwd(b, s), 0)),
            row_bwd, row_bwd,
        ],
        out_specs=pl.BlockSpec(hist_block, lambda b, s: (b, geo.backward_order(s), 0, 0, 0)),
        out_shape=jax.ShapeDtypeStruct(hist_shape, BF16),
        scratch_shapes=[pltpu.VMEM((SSM_GROUPS, SSM_STATE, SSM_GROUP_W), F32)],
        compiler_params=_params(2),
        name="ssd_state",
    )(xbc, xbc, cc, cct, cpt)
    col = pl.BlockSpec((rows, LANES), lambda b, s: (fwd(b, s), 0))
    row = pl.BlockSpec((cps, LANES, CHUNK), lambda b, s: (fwd(b, s), 0, 0))
    return pl.pallas_call(
        functools.partial(_ssd_out_kernel, cps=cps),
        grid=(geo.batch, geo.ns),
        in_specs=[
            pl.BlockSpec((rows, bc_w), lambda b, s: (fwd(b, s), SSM_INNER // bc_w + 1)),
            pl.BlockSpec((rows, bc_w), lambda b, s: (fwd(b, s), SSM_INNER // bc_w)),
            pl.BlockSpec((rows, SSM_INNER), lambda b, s: (fwd(b, s), 0)),
            pl.BlockSpec((rows, SSM_INNER), lambda b, s: (fwd(b, s), COL_Z // SSM_INNER)),
            col, row, row, row,
            pl.BlockSpec(hist_block, lambda b, s: (b, s, 0, 0, 0)),
            pl.BlockSpec((1, SSM_INNER), lambda b, s: (0, 0)),
            pl.BlockSpec((1, SSM_INNER), lambda b, s: (0, 0)),
        ],
        out_specs=pl.BlockSpec((rows, SSM_INNER), lambda b, s: (fwd(b, s), 0)),
        out_shape=jax.ShapeDtypeStruct((geo.rows, SSM_INNER), BF16),
        scratch_shapes=[pltpu.VMEM((SSM_GROUPS, SSM_STATE, SSM_GROUP_W), F32)],
        compiler_params=_params(2),
        name="ssd_out",
    )(xbc, xbc, xbc, proj, cc, cct, cpt, dtt, hist, dskip, norm_w)


def _merge_kernel(yr_ref, ys_ref, wr_ref, ws_ref, gr_ref, gs_ref, o_ref):
    r = _dot(yr_ref[...], wr_ref[...])
    s = _dot(ys_ref[...], ws_ref[...])
    m = jax.nn.sigmoid(gr_ref[...].astype(F32)) * r + jax.nn.sigmoid(gs_ref[...].astype(F32)) * s
    o_ref[...] = m.astype(o_ref.dtype)


def _merge(geo, yr, ys, w_ret, w_ssm, proj):
    tm, tn = geo.tm, 512
    gr0 = COL_GATES // tn
    gs0 = (COL_GATES + D_MODEL) // tn
    return pl.pallas_call(
        _merge_kernel,
        grid=(geo.n_tiles, D_MODEL // tn),
        in_specs=[
            pl.BlockSpec((tm, yr.shape[1]), lambda i, j: (i, 0)),
            pl.BlockSpec((tm, ys.shape[1]), lambda i, j: (i, 0)),
            pl.BlockSpec((yr.shape[1], tn), lambda i, j: (0, j)),
            pl.BlockSpec((ys.shape[1], tn), lambda i, j: (0, j)),
            pl.BlockSpec((tm, tn), lambda i, j: (i, gr0 + j)),
            pl.BlockSpec((tm, tn), lambda i, j: (i, gs0 + j)),
        ],
        out_specs=pl.BlockSpec((tm, tn), lambda i, j: (i, j)),
        out_shape=jax.ShapeDtypeStruct((geo.rows, D_MODEL), BF16),
        compiler_params=_params(2),
        name="branch_merge",
    )(yr, ys, w_ret, w_ssm, proj, proj)


def _residual_kernel(a_ref, w_ref, x_ref, mod_ref, o_ref, *, gate_row):
    o_ref[...] = x_ref[...] + mod_ref[gate_row:gate_row + 1, :] * _dot(a_ref[...], w_ref[...])


def _matmul_residual(geo, a, w, xs, mod_l, gate_row, tile0, n_tiles, tn):
    tm = geo.tm
    kdim = a.shape[1]
    return pl.pallas_call(
        functools.partial(_residual_kernel, gate_row=gate_row),
        grid=(n_tiles, D_MODEL // tn),
        in_specs=[
            pl.BlockSpec((tm, kdim), lambda i, j: (i, 0)),
            pl.BlockSpec((kdim, tn), lambda i, j: (0, j)),
            pl.BlockSpec((tm, tn), lambda i, j: (tile0 + i, j)),
            pl.BlockSpec((None, 6, tn), lambda i, j: (geo.mod_row(tile0 + i), 0, j)),
        ],
        out_specs=pl.BlockSpec((tm, tn), lambda i, j: (i, j)),
        out_shape=jax.ShapeDtypeStruct((n_tiles * tm, D_MODEL), F32),
        compiler_params=_params(2),
        name="proj_residual",
    )(a, w, xs, mod_l)


def _ffn_up_kernel(x_ref, mod_ref, nw_ref, wg_ref, wu_ref, o_ref, h_ref):
    @pl.when(pl.program_id(1) == 0)
    def _():
        h_ref[...] = _norm_mod(x_ref[...], nw_ref[...], mod_ref[3:4, :], mod_ref[4:5, :]).astype(BF16)

    h = h_ref[...]
    o_ref[...] = (_silu(_dot(h, wg_ref[...])) * _dot(h, wu_ref[...])).astype(o_ref.dtype)


def _ffn_up(geo, xs, mod_l, nw, w_gate, w_up):
    tm, tf = geo.tm, 512
    return pl.pallas_call(
        _ffn_up_kernel,
        grid=(geo.n_tiles, D_FF // tf),
        in_specs=[
            pl.BlockSpec((tm, D_MODEL), lambda i, j: (i, 0)),
            pl.BlockSpec((None, 6, D_MODEL), lambda i, j: (geo.mod_row(i), 0, 0)),
            pl.BlockSpec((1, D_MODEL), lambda i, j: (0, 0)),
            pl.BlockSpec((D_MODEL, tf), lambda i, j: (0, j)),
            pl.BlockSpec((D_MODEL, tf), lambda i, j: (0, j)),
        ],
        out_specs=pl.BlockSpec((tm, tf), lambda i, j: (i, j)),
        out_shape=jax.ShapeDtypeStruct((geo.rows, D_FF), BF16),
        scratch_shapes=[pltpu.VMEM((tm, D_MODEL), BF16)],
        compiler_params=_params(2),
        name="ffn_up",
    )(xs, mod_l, nw, w_gate, w_up)


MOE_ROW_TILE = 512
MOE_TOKEN_BLOCK = 512
ROUTE_W1, ROUTE_W2, ROUTE_E1, ROUTE_E2 = 8, 9, 10, 11


def _route_kernel(x_ref, mod_ref, nw_ref, router_ref, h_ref, route_ref, cnt_ref, tri_ref, carry_ref):
    i = pl.program_id(0)
    tm = x_ref.shape[0]
    lane = lax.broadcasted_iota(jnp.int32, (tm, LANES), 1)

    @pl.when(i == 0)
    def _():
        ii = lax.broadcasted_iota(jnp.int32, (tm, tm), 0)
        jj = lax.broadcasted_iota(jnp.int32, (tm, tm), 1)
        tri_ref[...] = (jj < ii).astype(BF16)
        carry_ref[...] = jnp.zeros_like(carry_ref)

    h = _norm_mod(x_ref[...], nw_ref[...], mod_ref[3:4, :], mod_ref[4:5, :])
    h_hi, h_lo = _split2(h)
    h_ref[...] = h_hi
    r_hi, r_lo = _split2(router_ref[...])
    logits = _dot(h_hi, r_hi) + _dot(h_hi, r_lo) + _dot(h_lo, r_hi)
    neg = -jnp.inf
    lg = jnp.where(lane < N_EXPERTS, logits, neg)
    m1 = jnp.max(lg, axis=-1, keepdims=True)
    i1 = jnp.min(jnp.where(lg == m1, lane, LANES), axis=-1, keepdims=True)
    lg2 = jnp.where(lane == i1, neg, lg)
    m2 = jnp.max(lg2, axis=-1, keepdims=True)
    i2 = jnp.min(jnp.where(lg2 == m2, lane, LANES), axis=-1, keepdims=True)
    e2 = jnp.exp(m2 - m1)
    w1 = 1.0 / (1.0 + e2)
    w2 = e2 / (1.0 + e2)
    chosen = jnp.logical_or(lane == i1, lane == i2)
    mask = jnp.where(chosen, 1.0, 0.0)
    pos = _dot(tri_ref[...], mask.astype(BF16)) + carry_ref[0:1, :]
    carry_ref[0:1, :] = carry_ref[0:1, :] + jnp.sum(mask, axis=0, keepdims=True)
    rec = jnp.where(lane < N_EXPERTS, pos, 0.0)
    rec = jnp.where(lane == ROUTE_W1, w1, rec)
    rec = jnp.where(lane == ROUTE_W2, w2, rec)
    rec = jnp.where(lane == ROUTE_E1, i1.astype(F32), rec)
    rec = jnp.where(lane == ROUTE_E2, i2.astype(F32), rec)
    route_ref[...] = rec
    cnt_ref[...] = carry_ref[...]


def _slots_kernel(route_ref, start_ref, o_ref):
    rec = route_ref[...]
    lane = lax.broadcasted_iota(jnp.int32, rec.shape, 1)
    slot = rec + start_ref[...]
    e1 = rec[:, ROUTE_E1:ROUTE_E1 + 1].astype(jnp.int32)
    e2 = rec[:, ROUTE_E2:ROUTE_E2 + 1].astype(jnp.int32)
    d1 = jnp.sum(jnp.where(lane == e1, slot, 0.0), axis=-1, keepdims=True)
    d2 = jnp.sum(jnp.where(lane == e2, slot, 0.0), axis=-1, keepdims=True)
    o_ref[...] = jnp.where(lane == 0, d1, jnp.where(lane == 1, d2, 0.0)).astype(jnp.int32)


def _dispatch_kernel(rp_ref, sbp_ref, fp_ref, np_ref, dt_ref, h_ref, o_ref):
    p = pl.program_id(0)
    tg, tb = o_ref.shape[0], h_ref.shape[0]

    @pl.when(p < np_ref[0])
    def _():
        slot = rp_ref[p] * tg + lax.broadcasted_iota(jnp.int32, (tg, tb), 0)
        d = dt_ref[...]
        hit = jnp.logical_or(d[0:1, :] == slot, d[1:2, :] == slot)
        sel = jnp.where(hit, 1.0, 0.0).astype(BF16)
        rows = _dot(sel, h_ref[...])

        @pl.when(fp_ref[p] == 1)
        def _():
            o_ref[...] = rows.astype(o_ref.dtype)

        @pl.when(fp_ref[p] == 0)
        def _():
            o_ref[...] = (o_ref[...].astype(F32) + rows).astype(o_ref.dtype)


def _expert_up_kernel(te_ref, nu_ref, x_ref, wg_ref, wu_ref, o_ref):
    del te_ref
    live = pl.program_id(1) < nu_ref[0]

    @pl.when(live)
    def _():
        x = x_ref[...]
        o_ref[...] = (_silu(_dot(x, wg_ref[...])) * _dot(x, wu_ref[...])).astype(o_ref.dtype)

    @pl.when(jnp.logical_not(live))
    def _():
        o_ref[...] = jnp.zeros_like(o_ref)


def _expert_down_kernel(te_ref, nu_ref, h_ref, wd_ref, o_ref):
    del te_ref
    live = pl.program_id(1) < nu_ref[0]

    @pl.when(live)
    def _():
        o_ref[...] = _dot(h_ref[...], wd_ref[...]).astype(o_ref.dtype)

    @pl.when(jnp.logical_not(live))
    def _():
        o_ref[...] = jnp.zeros_like(o_ref)


def _collect_kernel(sbp_ref, rp_ref, fp_ref, lp_ref, np_ref, x_ref, dest_ref, route_ref, y_ref, mod_ref, o_ref):
    p = pl.program_id(0)
    tb, tg = o_ref.shape[0], y_ref.shape[0]

    @pl.when(p < np_ref[0])
    def _():
        slot = rp_ref[p] * tg + lax.broadcasted_iota(jnp.int32, (tb, tg), 1)
        dest = dest_ref[...]
        rec = route_ref[...]
        sel = (jnp.where(dest[:, 0:1] == slot, rec[:, ROUTE_W1:ROUTE_W1 + 1], 0.0)
               + jnp.where(dest[:, 1:2] == slot, rec[:, ROUTE_W2:ROUTE_W2 + 1], 0.0))
        part = _dot(sel.astype(BF16), y_ref[...])

        @pl.when(fp_ref[p] == 1)
        def _():
            o_ref[...] = part

        @pl.when(fp_ref[p] == 0)
        def _():
            o_ref[...] += part

        @pl.when(lp_ref[p] == 1)
        def _():
            o_ref[...] = x_ref[...] + mod_ref[5:6, :] * o_ref[...]


def _moe(geo, xs, mod_l, nw, router, w_gate, w_up, w_down, tile0, n_tiles):
    tm, tg = geo.tm, MOE_ROW_TILE
    tb = min(MOE_TOKEN_BLOCK, tm)
    rows = n_tiles * tm
    nb = rows // tb
    n_slots = 2 * rows + N_EXPERTS * tg
    n_gt = n_slots // tg
    n_pairs_max = n_gt + nb * N_EXPERTS
    rec_tile = pl.BlockSpec((tm, LANES), lambda i: (i, 0))

    h2, route, counts = pl.pallas_call(
        _route_kernel,
        grid=(n_tiles,),
        in_specs=[
            pl.BlockSpec((tm, D_MODEL), lambda i: (tile0 + i, 0)),
            pl.BlockSpec((None, 6, D_MODEL), lambda i: (geo.mod_row(tile0 + i), 0, 0)),
            pl.BlockSpec((1, D_MODEL), lambda i: (0, 0)),
            pl.BlockSpec((D_MODEL, LANES), lambda i: (0, 0)),
        ],
        out_specs=[pl.BlockSpec((tm, D_MODEL), lambda i: (i, 0)), rec_tile, pl.BlockSpec((8, LANES), lambda i: (0, 0))],
        out_shape=[
            jax.ShapeDtypeStruct((rows, D_MODEL), BF16),
            jax.ShapeDtypeStruct((rows, LANES), F32),
            jax.ShapeDtypeStruct((8, LANES), F32),
        ],
        scratch_shapes=[pltpu.VMEM((tm, tm), BF16), pltpu.VMEM((8, LANES), F32)],
        compiler_params=_params(1),
        name="moe_route",
    )(xs, mod_l, nw, router)

    cnt = counts[0, :N_EXPERTS].astype(jnp.int32)
    padded = ((cnt + tg - 1) // tg) * tg
    ends = jnp.cumsum(padded)
    starts = ends - padded
    n_used = (ends[-1] // tg).astype(jnp.int32).reshape(1)
    tile_expert = jnp.searchsorted(ends, jnp.minimum(jnp.arange(n_gt), n_used[0] - 1) * tg, side="right")
    tile_expert = jnp.minimum(tile_expert, N_EXPERTS - 1).astype(jnp.int32)
    start_row = jnp.zeros((1, LANES), F32).at[0, :N_EXPERTS].set(starts.astype(F32))

    dest = pl.pallas_call(
        _slots_kernel,
        grid=(n_tiles,),
        in_specs=[rec_tile, pl.BlockSpec((1, LANES), lambda i: (0, 0))],
        out_specs=rec_tile,
        out_shape=jax.ShapeDtypeStruct((rows, LANES), jnp.int32),
        compiler_params=_params(1),
        name="moe_slots",
    )(route, start_row)

    pos_lo = route.reshape(nb, tb, LANES)[:, 0, :N_EXPERTS].astype(jnp.int32)
    pos_hi = jnp.concatenate([pos_lo[1:], cnt[None, :]], axis=0)
    t_lo = (starts[None, :] + pos_lo) // tg
    t_hi = (starts[None, :] + pos_hi - 1) // tg
    tiles = jnp.arange(n_gt)[:, None, None]
    share = jnp.any((pos_hi > pos_lo)[None] & (t_lo[None] <= tiles) & (tiles <= t_hi[None]), axis=-1)
    n_pairs = jnp.sum(share).astype(jnp.int32).reshape(1)
    last_valid = jnp.minimum(jnp.arange(n_pairs_max), n_pairs[0] - 1)

    def pair_list(mat):
        flat = jnp.nonzero(mat.ravel(), size=n_pairs_max, fill_value=0)[0][last_valid]
        major, minor = (flat // mat.shape[1]).astype(jnp.int32), (flat % mat.shape[1]).astype(jnp.int32)
        first = jnp.concatenate([jnp.ones((1,), jnp.int32), (major[1:] != major[:-1]).astype(jnp.int32)])
        last = jnp.concatenate([(major[1:] != major[:-1]).astype(jnp.int32), jnp.ones((1,), jnp.int32)])
        last = jnp.where(jnp.arange(n_pairs_max) == n_pairs[0] - 1, 1, last)
        return major, minor, first, last

    d_r, d_sb, d_first, _ = pair_list(share)
    c_sb, c_r, c_first, c_last = pair_list(share.T)
    dest_t = dest[:, :2].T

    xg = pl.pallas_call(
        _dispatch_kernel,
        grid_spec=pltpu.PrefetchScalarGridSpec(
            num_scalar_prefetch=4,
            grid=(n_pairs_max,),
            in_specs=[
                pl.BlockSpec((2, tb), lambda p, rp, sbp, fp, npr: (0, sbp[p])),
                pl.BlockSpec((tb, D_MODEL), lambda p, rp, sbp, fp, npr: (sbp[p], 0)),
            ],
            out_specs=pl.BlockSpec((tg, D_MODEL), lambda p, rp, sbp, fp, npr: (rp[p], 0)),
        ),
        out_shape=jax.ShapeDtypeStruct((n_slots, D_MODEL), BF16),
        compiler_params=_params(1),
        name="moe_dispatch",
    )(d_r, d_sb, d_first, n_pairs, dest_t, h2)

    tf = 1024
    hg = pl.pallas_call(
        _expert_up_kernel,
        grid_spec=pltpu.PrefetchScalarGridSpec(
            num_scalar_prefetch=2,
            grid=(MOE_D_FF // tf, n_gt),
            in_specs=[
                pl.BlockSpec((tg, D_MODEL), lambda j, r, te, nu: (r, 0)),
                pl.BlockSpec((None, D_MODEL, tf), lambda j, r, te, nu: (te[r], 0, j)),
                pl.BlockSpec((None, D_MODEL, tf), lambda j, r, te, nu: (te[r], 0, j)),
            ],
            out_specs=pl.BlockSpec((tg, tf), lambda j, r, te, nu: (r, j)),
        ),
        out_shape=jax.ShapeDtypeStruct((n_slots, MOE_D_FF), BF16),
        compiler_params=_params(2),
        name="moe_expert_up",
    )(tile_expert, n_used, xg, w_gate, w_up)

    tn = 1024
    yg = pl.pallas_call(
        _expert_down_kernel,
        grid_spec=pltpu.PrefetchScalarGridSpec(
            num_scalar_prefetch=2,
            grid=(D_MODEL // tn, n_gt),
            in_specs=[
                pl.BlockSpec((tg, MOE_D_FF), lambda j, r, te, nu: (r, 0)),
                pl.BlockSpec((None, MOE_D_FF, tn), lambda j, r, te, nu: (te[r], 0, j)),
            ],
            out_specs=pl.BlockSpec((tg, tn), lambda j, r, te, nu: (r, j)),
        ),
        out_shape=jax.ShapeDtypeStruct((n_slots, D_MODEL), BF16),
        compiler_params=_params(2),
        name="moe_expert_down",
    )(tile_expert, n_used, hg, w_down)

    per = tm // tb
    return pl.pallas_call(
        _collect_kernel,
        grid_spec=pltpu.PrefetchScalarGridSpec(
            num_scalar_prefetch=5,
            grid=(n_pairs_max,),
            in_specs=[
                pl.BlockSpec((tb, D_MODEL), lambda p, sbp, rp, fp, lp, npr: (tile0 * per + sbp[p], 0)),
                pl.BlockSpec((tb, LANES), lambda p, sbp, rp, fp, lp, npr: (sbp[p], 0)),
                pl.BlockSpec((tb, LANES), lambda p, sbp, rp, fp, lp, npr: (sbp[p], 0)),
                pl.BlockSpec((tg, D_MODEL), lambda p, sbp, rp, fp, lp, npr: (rp[p], 0)),
                pl.BlockSpec((None, 6, D_MODEL), lambda p, sbp, rp, fp, lp, npr: (geo.mod_row(tile0 + sbp[p] // per), 0, 0)),
            ],
            out_specs=pl.BlockSpec((tb, D_MODEL), lambda p, sbp, rp, fp, lp, npr: (sbp[p], 0)),
        ),
        out_shape=jax.ShapeDtypeStruct((rows, D_MODEL), F32),
        compiler_params=_params(1),
        name="moe_collect",
    )(c_sb, c_r, c_first, c_last, n_pairs, xs, dest, route, yg, mod_l)


def _final_norm_kernel(x_ref, w_ref, o_ref):
    x = x_ref[...]
    o_ref[...] = x * lax.rsqrt(jnp.mean(x * x, axis=-1, keepdims=True) + EPS) * w_ref[...]


def _final_norm(x, w, tm):
    rows = x.shape[0]
    return pl.pallas_call(
        _final_norm_kernel,
        grid=(rows // tm,),
        in_specs=[pl.BlockSpec((tm, D_MODEL), lambda i: (i, 0)), pl.BlockSpec((1, D_MODEL), lambda i: (0, 0))],
        out_specs=pl.BlockSpec((tm, D_MODEL), lambda i: (i, 0)),
        out_shape=jax.ShapeDtypeStruct((rows, D_MODEL), F32),
        compiler_params=_params(1),
        name="final_norm",
    )(x, w)


def _rope_tables(geo):
    half = RET_DK // 4
    inv = ROPE_BASE ** (-jnp.arange(half, dtype=F32) / half)
    pos = jnp.arange(geo.seq)
    ang_r = (pos // GRID_W).astype(F32)[:, None] * inv[None, :]
    ang_c = (pos % GRID_W).astype(F32)[:, None] * inv[None, :]
    cos = jnp.concatenate([jnp.cos(ang_r), jnp.cos(ang_r), jnp.cos(ang_c), jnp.cos(ang_c)], axis=1)
    sin = jnp.concatenate([-jnp.sin(ang_r), jnp.sin(ang_r), -jnp.sin(ang_c), jnp.sin(ang_c)], axis=1)
    cos = jnp.concatenate([jnp.ones((geo.tm, LANES), F32), cos], axis=0)
    sin = jnp.concatenate([jnp.zeros((geo.tm, LANES), F32), sin], axis=0)
    return cos, sin


_DT_PERM = np.array([d * SSM_HEADS + g * SSM_HPG + h
                     for g in range(SSM_GROUPS) for d in range(2) for h in range(SSM_HPG)])


def kernel(x, c, ctx, c_ctx, w_ada, b_ada, norm1_w, norm2_w, w_in, conv_w, conv_b, ret_decay_f, ret_decay_b, ret_gn_w, ssm_a_log_f, ssm_a_log_b, ssm_dt_bias_f, ssm_dt_bias_b, ssm_d, ssm_norm_w, w_ret_proj, w_ssm_proj, w_out, ffn_w_gate, ffn_w_up, ffn_w_down, moe_router, moe_w_gate, moe_w_up, moe_w_down, final_norm_w):
    batch, seq, d = x.shape
    ctx_len = ctx.shape[1]
    depth = w_ada.shape[0]
    assert d == D_MODEL and seq % GRID_W == 0
    geo = _Geom(batch, ctx_len, seq)
    tm = geo.tm

    mod_rows = -(-(batch + 1) // 8) * 8
    cvec = jnp.zeros((mod_rows, d), F32).at[0].set(c_ctx).at[1:batch + 1].set(c)
    mod = _modulation(cvec, w_ada, b_ada).reshape(depth, mod_rows, 6, d)
    cos_t, sin_t = _rope_tables(geo)

    xs = jnp.concatenate([ctx.reshape(batch * ctx_len, d), x.reshape(batch * seq, d)], axis=0)
    lat_tile0 = geo.n_ctx_tiles
    n_lat_tiles = geo.n_tiles - geo.n_ctx_tiles

    for i in range(depth):
        mod_l = mod[i]
        w_main = jnp.concatenate([w_in[i][:, ORIG_Z_LO:ORIG_Z_HI], w_in[i][:, :ORIG_Z_LO],
                                  w_in[i][:, ORIG_Z_HI:ORIG_DT_LO], w_in[i][:, ORIG_DT_HI:]], axis=1).astype(BF16)
        w_dt = w_in[i][:, ORIG_DT_LO:ORIG_DT_HI][:, _DT_PERM].astype(BF16)
        proj, dt = _inproj(geo, xs, mod_l, norm1_w[i][None, :], w_main, w_dt, cos_t, sin_t)

        xbc = _conv_silu(geo, proj, conv_w[i], conv_b[i][None, :])
        bias = jnp.concatenate([ssm_dt_bias_f[i], ssm_dt_bias_b[i]])[_DT_PERM][None, :]
        alog = jnp.concatenate([ssm_a_log_f[i], ssm_a_log_b[i]])[_DT_PERM][None, :]
        cc, cct, cpt, dtt = _decay_tables(geo, dt, bias, alog)
        dskip = jnp.repeat(ssm_d[i], SSM_INNER // SSM_HEADS)[None, :]
        ys = _ssd_scan(geo, proj, xbc, cc, cct, cpt, dtt, dskip, ssm_norm_w[i][None, :])
        yr = _ret_scan(geo, proj, jnp.stack([ret_decay_f[i], ret_decay_b[i]]), ret_gn_w[i][None, :])

        merged = _merge(geo, yr, ys, w_ret_proj[i].astype(BF16), w_ssm_proj[i].astype(BF16), proj)
        xs = _matmul_residual(geo, merged, w_out[i].astype(BF16), xs, mod_l, 2, 0, geo.n_tiles, 1024)

        j = i // 2
        tile0, n_tiles = (0, geo.n_tiles) if i < depth - 1 else (lat_tile0, n_lat_tiles)
        if i % 2 == 0:
            hid = _ffn_up(geo, xs, mod_l, norm2_w[i][None, :], ffn_w_gate[j].astype(BF16), ffn_w_up[j].astype(BF16))
            hid = hid[tile0 * tm:]
            xs = _matmul_residual(geo, hid, ffn_w_down[j].astype(BF16), xs, mod_l, 5, tile0, n_tiles, 512)
        else:
            router = jnp.zeros((d, LANES), F32).at[:, :N_EXPERTS].set(moe_router[j])
            xs = _moe(geo, xs, mod_l, norm2_w[i][None, :], router, moe_w_gate[j].astype(BF16),
                      moe_w_up[j].astype(BF16), moe_w_down[j].astype(BF16), tile0, n_tiles)

    lat = xs if xs.shape[0] == batch * seq else xs[batch * ctx_len:]
    return _final_norm(lat, final_norm_w[None, :], tm).reshape(batch, seq, d)
```

```python
import functools

import numpy as np
import jax
import jax.numpy as jnp
from jax import lax
from jax.experimental import pallas as pl
from jax.experimental.pallas import tpu as pltpu

F32 = jnp.float32
BF16 = jnp.bfloat16

D_MODEL = 2048
GRID_W = 64
CHUNK = 128
EPS = 1e-6
ROPE_BASE = 10000.0
RET_HEADS = 8
RET_DK = 128
RET_DV = 256
SSM_INNER = 4096
SSM_HEADS = 64
SSM_GROUPS = 8
SSM_HPG = 8
SSM_GROUP_W = SSM_INNER // SSM_GROUPS
SSM_STATE = 128
XBC_W = 6144
CONV_K = 5
D_FF = 5632
N_EXPERTS = 8
MOE_D_FF = 4096

COL_Z, COL_Q, COL_K, COL_V, COL_G, COL_XBC, COL_GATES = 0, 4096, 5120, 6144, 8192, 10240, 16384
PROJ_W = 20480
ORIG_Z_LO, ORIG_Z_HI, ORIG_DT_LO, ORIG_DT_HI = 6144, 10240, 16384, 16512
SCAN_ROWS = 256

LANES = 128
VMEM_LIMIT_BYTES = 56 * 1024 * 1024


def _params(n_axes, vmem=VMEM_LIMIT_BYTES):
    return pltpu.CompilerParams(dimension_semantics=("arbitrary",) * n_axes, vmem_limit_bytes=vmem)


def _silu(x):
    return x * jax.nn.sigmoid(x)


def _split2(x):
    hi = x.astype(BF16)
    lo = (x - hi.astype(F32)).astype(BF16)
    return hi, lo


def _split3(x):
    hi = x.astype(BF16)
    r = x - hi.astype(F32)
    mid = r.astype(BF16)
    lo = (r - mid.astype(F32)).astype(BF16)
    return hi, mid, lo


def _dot(a, b):
    return jnp.dot(a, b, preferred_element_type=F32)


def _dot_nt(a, b):
    return lax.dot_general(a, b, (((1,), (1,)), ((), ())), preferred_element_type=F32)


def _dot_tn(a, b):
    return lax.dot_general(a, b, (((0,), (0,)), ((), ())), preferred_element_type=F32)


def _norm_mod(x, nw, shift, scale):
    y = x * lax.rsqrt(jnp.mean(x * x, axis=-1, keepdims=True) + EPS)
    return (y * nw) * (1.0 + scale) + shift


def _mod_kernel(c_ref, w_ref, b_ref, o_ref):
    s_hi, s_lo = _split2(_silu(c_ref[...]))
    w_hi, w_lo = _split2(w_ref[...])
    o_ref[...] = _dot(s_hi, w_hi) + _dot(s_hi, w_lo) + _dot(s_lo, w_hi) + b_ref[...]


def _modulation(cvec, w_ada, b_ada):
    depth, d, w6 = w_ada.shape
    rows = cvec.shape[0]
    tn = 512
    return pl.pallas_call(
        _mod_kernel,
        grid=(depth, w6 // tn),
        in_specs=[
            pl.BlockSpec((rows, d), lambda l, j: (0, 0)),
            pl.BlockSpec((None, d, tn), lambda l, j: (l, 0, j)),
            pl.BlockSpec((None, 1, tn), lambda l, j: (l, 0, j)),
        ],
        out_specs=pl.BlockSpec((None, rows, tn), lambda l, j: (l, 0, j)),
        out_shape=jax.ShapeDtypeStruct((depth, rows, w6), F32),
        compiler_params=_params(2),
        name="adaln_mod",
    )(cvec, w_ada, b_ada.reshape(depth, 1, w6))


class _Geom:
    def __init__(self, batch, ctx_len, seq):
        self.batch, self.ctx_len, self.seq = batch, ctx_len, seq
        tm = 1024
        while (batch * ctx_len) % tm or seq % tm:
            tm //= 2
        assert tm >= CHUNK and ctx_len % CHUNK == 0 and seq % CHUNK == 0
        self.tm = tm
        self.ctx_rows = batch * ctx_len
        self.rows = self.ctx_rows + batch * seq
        self.n_ctx_tiles = self.ctx_rows // tm
        self.tiles_per_batch = seq // tm
        self.n_tiles = self.rows // tm
        self.step_rows = min(SCAN_ROWS, ctx_len)
        assert ctx_len % self.step_rows == 0 and seq % self.step_rows == 0 and self.step_rows % CHUNK == 0
        self.ns_ctx = ctx_len // self.step_rows
        self.ns_lat = seq // self.step_rows
        self.ns = self.ns_ctx + self.ns_lat

    def mod_row(self, i):
        return jnp.where(i < self.n_ctx_tiles, 0, 1 + (i - self.n_ctx_tiles) // self.tiles_per_batch)

    def rope_block(self, i):
        return jnp.where(i < self.n_ctx_tiles, 0, 1 + (i - self.n_ctx_tiles) % self.tiles_per_batch)

    def step_block(self, b, c):
        return jnp.where(c < self.ns_ctx, b * self.ns_ctx + c,
                         self.batch * self.ns_ctx + b * self.ns_lat + (c - self.ns_ctx))

    def backward_order(self, s):
        return jnp.where(s < self.ns_ctx, self.ns_ctx - 1 - s, self.ns - 1 - (s - self.ns_ctx))


def _rope_store(acc, cos, sin, o_ref, scale):
    lane = lax.broadcasted_iota(jnp.int32, (acc.shape[0], LANES), 1)
    first_half = (lane % 64) < 32
    for h in range(acc.shape[1] // LANES):
        xh = acc[:, h * LANES:(h + 1) * LANES]
        partner = jnp.where(first_half, pltpu.roll(xh, 96, 1), pltpu.roll(xh, 32, 1))
        o_ref[:, h * LANES:(h + 1) * LANES] = ((xh * cos + partner * sin) * scale).astype(o_ref.dtype)


def _inproj_kernel(x_ref, mod_ref, nw_ref, w_ref, wdt_ref, cos_ref, sin_ref, o_ref, dt_ref, h_ref):
    j = pl.program_id(1)

    @pl.when(j == 0)
    def _():
        h = _norm_mod(x_ref[...], nw_ref[...], mod_ref[0:1, :], mod_ref[1:2, :]).astype(BF16)
        h_ref[...] = h
        dt_ref[...] = _dot(h, wdt_ref[...])

    acc = _dot(h_ref[...], w_ref[...])

    jq, jk = COL_Q // acc.shape[1], COL_K // acc.shape[1]

    @pl.when(j == jq)
    def _():
        _rope_store(acc, cos_ref[...], sin_ref[...], o_ref, 1.0)

    @pl.when(j == jk)
    def _():
        _rope_store(acc, cos_ref[...], sin_ref[...], o_ref, RET_DK ** -0.5)

    @pl.when(jnp.logical_and(j != jq, j != jk))
    def _():
        o_ref[...] = acc.astype(o_ref.dtype)


def _inproj(geo, xs, mod_l, nw, w_main, w_dt, cos_t, sin_t):
    tm, tn = geo.tm, 1024
    return pl.pallas_call(
        _inproj_kernel,
        grid=(geo.n_tiles, PROJ_W // tn),
        in_specs=[
            pl.BlockSpec((tm, D_MODEL), lambda i, j: (i, 0)),
            pl.BlockSpec((None, 6, D_MODEL), lambda i, j: (geo.mod_row(i), 0, 0)),
            pl.BlockSpec((1, D_MODEL), lambda i, j: (0, 0)),
            pl.BlockSpec((D_MODEL, tn), lambda i, j: (0, j)),
            pl.BlockSpec((D_MODEL, LANES), lambda i, j: (0, 0)),
            pl.BlockSpec((tm, LANES), lambda i, j: (geo.rope_block(i), 0)),
            pl.BlockSpec((tm, LANES), lambda i, j: (geo.rope_block(i), 0)),
        ],
        out_specs=[
            pl.BlockSpec((tm, tn), lambda i, j: (i, j)),
            pl.BlockSpec((tm, LANES), lambda i, j: (i, 0)),
        ],
        out_shape=[
            jax.ShapeDtypeStruct((geo.rows, PROJ_W), BF16),
            jax.ShapeDtypeStruct((geo.rows, LANES), F32),
        ],
        scratch_shapes=[pltpu.VMEM((tm, D_MODEL), BF16)],
        compiler_params=_params(2),
        name="in_proj",
    )(xs, mod_l, nw, w_main, w_dt, cos_t, sin_t)


CONV_HALO = 16


def _conv_kernel(prev_ref, main_ref, next_ref, w_ref, b_ref, o_ref, ext_ref, *, geo, rows):
    i = pl.program_id(0)
    start = i * rows
    in_ctx = start < geo.ctx_rows
    seq_len = jnp.where(in_ctx, geo.ctx_len, geo.seq)
    off = jnp.where(in_ctx, start, start - geo.ctx_rows)
    first = (off % seq_len) == 0
    last = ((off + rows) % seq_len) == 0
    ext_ref[0:CONV_HALO, :] = jnp.where(first, 0.0, prev_ref[...].astype(F32))
    ext_ref[CONV_HALO:CONV_HALO + rows, :] = main_ref[...].astype(F32)
    ext_ref[CONV_HALO + rows:, :] = jnp.where(last, 0.0, next_ref[...].astype(F32))
    acc = jnp.broadcast_to(b_ref[...], (rows, b_ref.shape[1]))
    for k in range(CONV_K):
        acc = acc + ext_ref[pl.ds(CONV_HALO + k - CONV_K // 2, rows), :] * w_ref[k:k + 1, :]
    o_ref[...] = _silu(acc).astype(o_ref.dtype)


def _conv_silu(geo, proj, conv_w, conv_b):
    rows = min(256, geo.ctx_len)
    tc = 2048
    col0 = COL_XBC // tc
    n_halo = geo.rows // CONV_HALO
    per = rows // CONV_HALO
    return pl.pallas_call(
        functools.partial(_conv_kernel, geo=geo, rows=rows),
        grid=(geo.rows // rows, XBC_W // tc),
        in_specs=[
            pl.BlockSpec((CONV_HALO, tc), lambda i, j: (jnp.maximum(i * per - 1, 0), col0 + j)),
            pl.BlockSpec((rows, tc), lambda i, j: (i, col0 + j)),
            pl.BlockSpec((CONV_HALO, tc), lambda i, j: (jnp.minimum((i + 1) * per, n_halo - 1), col0 + j)),
            pl.BlockSpec((CONV_K, tc), lambda i, j: (0, j)),
            pl.BlockSpec((1, tc), lambda i, j: (0, j)),
        ],
        out_specs=pl.BlockSpec((rows, tc), lambda i, j: (i, j)),
        out_shape=jax.ShapeDtypeStruct((geo.rows, XBC_W), BF16),
        scratch_shapes=[pltpu.VMEM((rows + 2 * CONV_HALO, tc), F32)],
        compiler_params=_params(2),
        name="ssm_conv",
    )(proj, proj, proj, conv_w, conv_b)


def _decay_kernel(dt_ref, bias_ref, alog_ref, cc_ref, cct_ref, cpt_ref, dtt_ref):
    ii = lax.broadcasted_iota(jnp.int32, (CHUNK, CHUNK), 0)
    jj = lax.broadcasted_iota(jnp.int32, (CHUNK, CHUNK), 1)
    lower = (jj <= ii).astype(BF16)
    upper = (jj >= ii).astype(BF16)
    forward = (lax.broadcasted_iota(jnp.int32, (CHUNK, LANES), 1) % 16) < 8
    neg_a = -jnp.exp(alog_ref[...])
    for ci in range(cct_ref.shape[0]):
        rows = pl.ds(ci * CHUNK, CHUNK)
        x = dt_ref[rows, :] + bias_ref[...]
        dtv = jnp.maximum(x, 0.0) + jnp.log1p(jnp.exp(-jnp.abs(x)))
        hi, mid, lo = _split3(dtv * neg_a)
        cum_f = _dot(lower, hi) + _dot(lower, mid) + _dot(lower, lo)
        cum_b = _dot(upper, hi) + _dot(upper, mid) + _dot(upper, lo)
        cc = jnp.where(forward, cum_f, cum_b)
        cc_ref[rows, :] = cc
        cct_ref[ci] = cc.T
        cpt_ref[ci] = (cc - jnp.log(dtv)).T
        dtt_ref[ci] = dtv.T


def _decay_tables(geo, dt, bias, alog):
    n_chunks = geo.rows // CHUNK
    per = max(k for k in (4, 2, 1) if n_chunks % k == 0)
    col = pl.BlockSpec((per * CHUNK, LANES), lambda i: (i, 0))
    row = pl.BlockSpec((per, LANES, CHUNK), lambda i: (i, 0, 0))
    vec = pl.BlockSpec((1, LANES), lambda i: (0, 0))
    col_shape = jax.ShapeDtypeStruct((geo.rows, LANES), F32)
    row_shape = jax.ShapeDtypeStruct((n_chunks, LANES, CHUNK), F32)
    return pl.pallas_call(
        _decay_kernel,
        grid=(n_chunks // per,),
        in_specs=[col, vec, vec],
        out_specs=[col, row, row, row],
        out_shape=[col_shape, row_shape, row_shape, row_shape],
        compiler_params=_params(1),
        name="ssm_decay",
    )(dt, bias, alog)


def _lanes2(a):
    return jnp.concatenate([a, a], axis=1)


def _ret_tables(dec_ref, direction, rows, wk_ref, a_ref, e_ref=None):
    n = float(rows)
    row = lax.broadcasted_iota(jnp.int32, (rows, LANES), 0).astype(F32)
    for h in range(RET_HEADS):
        lam = jnp.exp(jnp.full((rows, LANES), dec_ref[direction, h], F32))
        if direction == 0:
            wk_ref[h] = jnp.exp(-lam * (n - 1.0 - row))
            if e_ref is not None:
                e_ref[0, h] = jnp.exp(-lam * (row + 1.0))
        else:
            wk_ref[h] = jnp.exp(-lam * row)
        a_ref[h] = _lanes2(jnp.exp(-lam[0:8, :] * n))


def _ret_state_kernel(dec_ref, k_ref, v_ref, hist_ref, sb_ref, wk_ref, a_ref, *, rows):
    @pl.when(pl.program_id(1) == 0)
    def _():
        sb_ref[...] = jnp.zeros_like(sb_ref)
        _ret_tables(dec_ref, 1, rows, wk_ref, a_ref)

    def body(h, carry):
        kh = k_ref[:, pl.ds(pl.multiple_of(h * RET_DK, RET_DK), RET_DK)]
        vh = v_ref[:, pl.ds(pl.multiple_of(h * RET_DV, RET_DV), RET_DV)]
        hist_ref[h] = sb_ref[h].astype(BF16)
        kw = (kh.astype(F32) * wk_ref[h]).astype(BF16)
        sb_ref[h] = a_ref[h][0:1, :] * sb_ref[h] + _dot_tn(kw, vh)
        return carry

    lax.fori_loop(0, RET_HEADS, body, 0, unroll=4)


def _ret_out_kernel(dec_ref, q_ref, k_ref, v_ref, g_ref, hist_ref, gnw_ref, o_ref,
                    sf_ref, wk_ref, a_ref, e_ref, m_ref, *, rows):
    @pl.when(pl.program_id(1) == 0)
    def _():
        sf_ref[...] = jnp.zeros_like(sf_ref)
        _ret_tables(dec_ref, 0, rows, wk_ref, a_ref, e_ref)
        n = float(rows)
        row = lax.broadcasted_iota(jnp.int32, (rows, LANES), 0).astype(F32)
        ii = lax.broadcasted_iota(jnp.int32, (rows, rows), 0)
        jj = lax.broadcasted_iota(jnp.int32, (rows, rows), 1)
        dist = (ii - jj).astype(F32)
        for h in range(RET_HEADS):
            lam_f = jnp.exp(jnp.full((rows, 1), dec_ref[0, h], F32))
            lam_b = jnp.exp(jnp.full((rows, 1), dec_ref[1, h], F32))
            m_ref[h] = (jnp.where(jj <= ii, jnp.exp(-lam_f * dist), 0.0)
                        + jnp.where(jj >= ii, jnp.exp(lam_b * dist), 0.0))
            e_ref[1, h] = jnp.exp(-jnp.exp(jnp.full((rows, LANES), dec_ref[1, h], F32)) * (n - row))

    def body(h, carry):
        ok = pl.multiple_of(h * RET_DK, RET_DK)
        ov = pl.multiple_of(h * RET_DV, RET_DV)
        qh = q_ref[:, pl.ds(ok, RET_DK)]
        kh = k_ref[:, pl.ds(ok, RET_DK)]
        vh = v_ref[:, pl.ds(ov, RET_DV)]
        probs = (_dot_nt(qh, kh) * m_ref[h]).astype(BF16)
        y = _dot(probs, vh)
        y = y + _lanes2(e_ref[0, h]) * _dot(qh, sf_ref[h].astype(BF16)) + _lanes2(e_ref[1, h]) * _dot(qh, hist_ref[h])
        kw = (kh.astype(F32) * wk_ref[h]).astype(BF16)
        sf_ref[h] = a_ref[h][0:1, :] * sf_ref[h] + _dot_tn(kw, vh)
        mu = jnp.mean(y, axis=-1, keepdims=True)
        yc = y - mu
        yn = yc * lax.rsqrt(jnp.mean(yc * yc, axis=-1, keepdims=True) + EPS)
        gate = _silu(g_ref[:, pl.ds(ov, RET_DV)].astype(F32))
        o_ref[:, pl.ds(ov, RET_DV)] = (yn * gnw_ref[:, pl.ds(ov, RET_DV)] * gate).astype(o_ref.dtype)
        return carry

    lax.fori_loop(0, RET_HEADS, body, 0, unroll=4)


def _ret_scan(geo, proj, decays, gn_w):
    rows = geo.step_rows
    qk_w, v_w = RET_HEADS * RET_DK, RET_HEADS * RET_DV

    def bwd(b, s):
        return geo.step_block(b, geo.backward_order(s))

    def fwd(b, s):
        return geo.step_block(b, s)

    smem = pl.BlockSpec(memory_space=pltpu.SMEM)
    hist_shape = (geo.batch, geo.ns, RET_HEADS, RET_DK, RET_DV)
    hist = pl.pallas_call(
        functools.partial(_ret_state_kernel, rows=rows),
        grid=(geo.batch, geo.ns),
        in_specs=[
            smem,
            pl.BlockSpec((rows, qk_w), lambda b, s: (bwd(b, s), COL_K // qk_w)),
            pl.BlockSpec((rows, v_w), lambda b, s: (bwd(b, s), COL_V // v_w)),
        ],
        out_specs=pl.BlockSpec((None, None) + hist_shape[2:], lambda b, s: (b, geo.backward_order(s), 0, 0, 0)),
        out_shape=jax.ShapeDtypeStruct(hist_shape, BF16),
        scratch_shapes=[
            pltpu.VMEM((RET_HEADS, RET_DK, RET_DV), F32),
            pltpu.VMEM((RET_HEADS, rows, LANES), F32),
            pltpu.VMEM((RET_HEADS, 8, RET_DV), F32),
        ],
        compiler_params=_params(2),
        name="ret_state",
    )(decays, proj, proj)
    return pl.pallas_call(
        functools.partial(_ret_out_kernel, rows=rows),
        grid=(geo.batch, geo.ns),
        in_specs=[
            smem,
            pl.BlockSpec((rows, qk_w), lambda b, s: (fwd(b, s), COL_Q // qk_w)),
            pl.BlockSpec((rows, qk_w), lambda b, s: (fwd(b, s), COL_K // qk_w)),
            pl.BlockSpec((rows, v_w), lambda b, s: (fwd(b, s), COL_V // v_w)),
            pl.BlockSpec((rows, v_w), lambda b, s: (fwd(b, s), COL_G // v_w)),
            pl.BlockSpec((None, None) + hist_shape[2:], lambda b, s: (b, s, 0, 0, 0)),
            pl.BlockSpec((1, v_w), lambda b, s: (0, 0)),
        ],
        out_specs=pl.BlockSpec((rows, v_w), lambda b, s: (fwd(b, s), 0)),
        out_shape=jax.ShapeDtypeStruct((geo.rows, v_w), BF16),
        scratch_shapes=[
            pltpu.VMEM((RET_HEADS, RET_DK, RET_DV), F32),
            pltpu.VMEM((RET_HEADS, rows, LANES), F32),
            pltpu.VMEM((RET_HEADS, 8, RET_DV), F32),
            pltpu.VMEM((2, RET_HEADS, rows, LANES), F32),
            pltpu.VMEM((RET_HEADS, rows, rows), F32),
        ],
        compiler_params=_params(2),
        name="ret_out",
    )(decays, proj, proj, proj, proj, hist, gn_w)


def _group_cols(ref, rows, g):
    return pltpu.roll(ref[rows, :], (LANES - 16 * g) % LANES, 1)


def _pair_rhs(left, vp):
    zero = jnp.zeros_like(vp)
    return jnp.concatenate([jnp.where(left, vp, zero), jnp.where(left, zero, vp)], axis=0)


def _state_update(s_ref, g, sl, kt, w_rows, h1, h2, a_pair, rhs):
    lhs = jnp.concatenate([(kt * w_rows[h1:h1 + 1, :]).astype(BF16), (kt * w_rows[h2:h2 + 1, :]).astype(BF16)], axis=1)
    s_ref[g, :, sl] = a_pair * s_ref[g, :, sl] + _dot(lhs, rhs)


def _ssd_state_kernel(k_ref, xs_ref, cc_ref, cct_ref, cpt_ref, hist_ref, sb_ref, *, cps):
    @pl.when(pl.program_id(1) == 0)
    def _():
        sb_ref[...] = jnp.zeros_like(sb_ref)

    left = lax.broadcasted_iota(jnp.int32, (CHUNK, LANES), 1) < 64
    left_row = lax.broadcasted_iota(jnp.int32, (1, LANES), 1) < 64

    def body(g, carry):
        ok = pl.multiple_of(g * SSM_STATE, SSM_STATE)
        ov = pl.multiple_of(g * SSM_GROUP_W, SSM_GROUP_W)
        ob = pl.multiple_of(g * 16 + SSM_HPG, SSM_HPG)
        for ci in reversed(range(cps)):
            rows = pl.ds(ci * CHUNK, CHUNK)
            a_all = jnp.exp(_group_cols(cc_ref, rows, g)[0:1, :])
            kt = k_ref[rows, pl.ds(ok, SSM_STATE)].astype(F32).T
            hist_ref[ci, g] = sb_ref[g].astype(BF16)
            w_rows = jnp.exp(cct_ref[ci, pl.ds(ob, SSM_HPG), 0:1] - cpt_ref[ci, pl.ds(ob, SSM_HPG), :])
            for p in range(SSM_HPG // 2):
                h1, h2 = 2 * p, 2 * p + 1
                sl = slice(p * LANES, (p + 1) * LANES)
                vp = xs_ref[rows, pl.ds(pl.multiple_of(ov + p * LANES, LANES), LANES)]
                ab = jnp.where(left_row, a_all[:, SSM_HPG + h1:SSM_HPG + h1 + 1], a_all[:, SSM_HPG + h2:SSM_HPG + h2 + 1])
                _state_update(sb_ref, g, sl, kt, w_rows, h1, h2, ab, _pair_rhs(left, vp))
        return carry

    lax.fori_loop(0, SSM_GROUPS, body, 0, unroll=2)


def _ssd_out_kernel(q_ref, k_ref, xs_ref, z_ref, cc_ref, cct_ref, cpt_ref, dtt_ref, hist_ref,
                    dskip_ref, nw_ref, o_ref, sf_ref, *, cps):
    @pl.when(pl.program_id(1) == 0)
    def _():
        sf_ref[...] = jnp.zeros_like(sf_ref)

    left = lax.broadcasted_iota(jnp.int32, (CHUNK, LANES), 1) < 64
    left_row = lax.broadcasted_iota(jnp.int32, (1, LANES), 1) < 64
    ii = lax.broadcasted_iota(jnp.int32, (CHUNK, CHUNK), 0)
    jj = lax.broadcasted_iota(jnp.int32, (CHUNK, CHUNK), 1)
    lower = jj <= ii
    diag = jj == ii

    def body(g, carry):
        ok = pl.multiple_of(g * SSM_STATE, SSM_STATE)
        ov = pl.multiple_of(g * SSM_GROUP_W, SSM_GROUP_W)
        og = pl.multiple_of(g * 16, 16)
        for ci in range(cps):
            rows = pl.ds(ci * CHUNK, CHUNK)
            cc = _group_cols(cc_ref, rows, g)
            cpt = cpt_ref[ci, pl.ds(og, 16), :]
            dtt = dtt_ref[ci, pl.ds(og, 16), :]
            q = q_ref[rows, pl.ds(ok, SSM_STATE)]
            kt = k_ref[rows, pl.ds(ok, SSM_STATE)].astype(F32).T
            scores = _dot(q, kt.astype(BF16))
            yf = _dot(q, sf_ref[g].astype(BF16))
            yb = _dot(q, hist_ref[ci, g])
            w_rows = jnp.exp(cct_ref[ci, pl.ds(og, SSM_HPG), CHUNK - 1:CHUNK] - cpt[0:SSM_HPG, :])
            a_all = jnp.exp(cc[CHUNK - 1:CHUNK, :])
            ys = []
            for p in range(SSM_HPG // 2):
                h1, h2 = 2 * p, 2 * p + 1
                sl = slice(p * LANES, (p + 1) * LANES)
                cols = pl.ds(pl.multiple_of(ov + p * LANES, LANES), LANES)
                vp = xs_ref[rows, cols]
                probs, ef, eb = [], [], []
                for h in (h1, h2):
                    hb = SSM_HPG + h
                    cf = jnp.broadcast_to(cc[:, h:h + 1], (CHUNK, CHUNK))
                    cb = jnp.broadcast_to(cc[:, hb:hb + 1], (CHUNK, CHUNK))
                    arg = jnp.where(lower, cf - cpt[h:h + 1, :], cb - cpt[hb:hb + 1, :])
                    e = jnp.exp(arg) + jnp.where(diag, dtt[hb:hb + 1, :], 0.0)
                    probs.append((scores * e).astype(BF16))
                    ef.append(jnp.exp(cf))
                    eb.append(jnp.exp(cb))
                rhs = _pair_rhs(left, vp)
                y = _dot(jnp.concatenate(probs, axis=1), rhs)
                y = (y + jnp.where(left, ef[0], ef[1]) * yf[:, sl] + jnp.where(left, eb[0], eb[1]) * yb[:, sl])
                af = jnp.where(left_row, a_all[:, h1:h1 + 1], a_all[:, h2:h2 + 1])
                _state_update(sf_ref, g, sl, kt, w_rows, h1, h2, af, rhs)
                ys.append(y + dskip_ref[:, cols] * vp.astype(F32))
            gcols = pl.ds(ov, SSM_GROUP_W)
            y = jnp.concatenate(ys, axis=1) * _silu(z_ref[rows, gcols].astype(F32))
            y = y * lax.rsqrt(jnp.mean(y * y, axis=-1, keepdims=True) + EPS)
            o_ref[rows, gcols] = (y * nw_ref[:, gcols]).astype(o_ref.dtype)
        return carry

    lax.fori_loop(0, SSM_GROUPS, body, 0)


def _ssd_scan(geo, proj, xbc, cc, cct, cpt, dtt, dskip, norm_w):
    rows = geo.step_rows
    cps = rows // CHUNK
    bc_w = SSM_GROUPS * SSM_STATE

    def bwd(b, s):
        return geo.step_block(b, geo.backward_order(s))

    def fwd(b, s):
        return geo.step_block(b, s)

    hist_shape = (geo.batch, geo.ns * cps, SSM_GROUPS, SSM_STATE, SSM_GROUP_W)
    hist_block = (None, cps) + hist_shape[2:]
    row_bwd = pl.BlockSpec((cps, LANES, CHUNK), lambda b, s: (bwd(b, s), 0, 0))
    hist = pl.pallas_call(
        functools.partial(_ssd_state_kernel, cps=cps),
        grid=(geo.batch, geo.ns),
        in_specs=[
            pl.BlockSpec((rows, bc_w), lambda b, s: (bwd(b, s), SSM_INNER // bc_w)),
            pl.BlockSpec((rows, SSM_INNER), lambda b, s: (bwd(b, s), 0)),
            pl.BlockSpec((rows, LANES), lambda b, s: (bwd(b, s), 0)),
            row_bwd, row_bwd,
        ],
        out_specs=pl.BlockSpec(hist_block, lambda b, s: (b, geo.backward_order(s), 0, 0, 0)),
        out_shape=jax.ShapeDtypeStruct(hist_shape, BF16),
        scratch_shapes=[pltpu.VMEM((SSM_GROUPS, SSM_STATE, SSM_GROUP_W), F32)],
        compiler_params=_params(2),
        name="ssd_state",
    )(xbc, xbc, cc, cct, cpt)
    col = pl.BlockSpec((rows, LANES), lambda b, s: (fwd(b, s), 0))
    row = pl.BlockSpec((cps, LANES, CHUNK), lambda b, s: (fwd(b, s), 0, 0))
    return pl.pallas_call(
        functools.partial(_ssd_out_kernel, cps=cps),
        grid=(geo.batch, geo.ns),
        in_specs=[
            pl.BlockSpec((rows, bc_w), lambda b, s: (fwd(b, s), SSM_INNER // bc_w + 1)),
            pl.BlockSpec((rows, bc_w), lambda b, s: (fwd(b, s), SSM_INNER // bc_w)),
            pl.BlockSpec((rows, SSM_INNER), lambda b, s: (fwd(b, s), 0)),
            pl.BlockSpec((rows, SSM_INNER), lambda b, s: (fwd(b, s), COL_Z // SSM_INNER)),
            col, row, row, row,
            pl.BlockSpec(hist_block, lambda b, s: (b, s, 0, 0, 0)),
            pl.BlockSpec((1, SSM_INNER), lambda b, s: (0, 0)),
            pl.BlockSpec((1, SSM_INNER), lambda b, s: (0, 0)),
        ],
        out_specs=pl.BlockSpec((rows, SSM_INNER), lambda b, s: (fwd(b, s), 0)),
        out_shape=jax.ShapeDtypeStruct((geo.rows, SSM_INNER), BF16),
        scratch_shapes=[pltpu.VMEM((SSM_GROUPS, SSM_STATE, SSM_GROUP_W), F32)],
        compiler_params=_params(2),
        name="ssd_out",
    )(xbc, xbc, xbc, proj, cc, cct, cpt, dtt, hist, dskip, norm_w)


def _merge_kernel(yr_ref, ys_ref, wr_ref, ws_ref, gr_ref, gs_ref, o_ref):
    r = _dot(yr_ref[...], wr_ref[...])
    s = _dot(ys_ref[...], ws_ref[...])
    m = jax.nn.sigmoid(gr_ref[...].astype(F32)) * r + jax.nn.sigmoid(gs_ref[...].astype(F32)) * s
    o_ref[...] = m.astype(o_ref.dtype)


def _merge(geo, yr, ys, w_ret, w_ssm, proj):
    tm, tn = geo.tm, 512
    gr0 = COL_GATES // tn
    gs0 = (COL_GATES + D_MODEL) // tn
    return pl.pallas_call(
        _merge_kernel,
        grid=(geo.n_tiles, D_MODEL // tn),
        in_specs=[
            pl.BlockSpec((tm, yr.shape[1]), lambda i, j: (i, 0)),
            pl.BlockSpec((tm, ys.shape[1]), lambda i, j: (i, 0)),
            pl.BlockSpec((yr.shape[1], tn), lambda i, j: (0, j)),
            pl.BlockSpec((ys.shape[1], tn), lambda i, j: (0, j)),
            pl.BlockSpec((tm, tn), lambda i, j: (i, gr0 + j)),
            pl.BlockSpec((tm, tn), lambda i, j: (i, gs0 + j)),
        ],
        out_specs=pl.BlockSpec((tm, tn), lambda i, j: (i, j)),
        out_shape=jax.ShapeDtypeStruct((geo.rows, D_MODEL), BF16),
        compiler_params=_params(2),
        name="branch_merge",
    )(yr, ys, w_ret, w_ssm, proj, proj)


def _residual_kernel(a_ref, w_ref, x_ref, mod_ref, o_ref, *, gate_row):
    o_ref[...] = x_ref[...] + mod_ref[gate_row:gate_row + 1, :] * _dot(a_ref[...], w_ref[...])


def _matmul_residual(geo, a, w, xs, mod_l, gate_row, tile0, n_tiles, tn):
    tm = geo.tm
    kdim = a.shape[1]
    return pl.pallas_call(
        functools.partial(_residual_kernel, gate_row=gate_row),
        grid=(n_tiles, D_MODEL // tn),
        in_specs=[
            pl.BlockSpec((tm, kdim), lambda i, j: (i, 0)),
            pl.BlockSpec((kdim, tn), lambda i, j: (0, j)),
            pl.BlockSpec((tm, tn), lambda i, j: (tile0 + i, j)),
            pl.BlockSpec((None, 6, tn), lambda i, j: (geo.mod_row(tile0 + i), 0, j)),
        ],
        out_specs=pl.BlockSpec((tm, tn), lambda i, j: (i, j)),
        out_shape=jax.ShapeDtypeStruct((n_tiles * tm, D_MODEL), F32),
        compiler_params=_params(2),
        name="proj_residual",
    )(a, w, xs, mod_l)


def _ffn_up_kernel(x_ref, mod_ref, nw_ref, wg_ref, wu_ref, o_ref, h_ref):
    @pl.when(pl.program_id(1) == 0)
    def _():
        h_ref[...] = _norm_mod(x_ref[...], nw_ref[...], mod_ref[3:4, :], mod_ref[4:5, :]).astype(BF16)

    h = h_ref[...]
    o_ref[...] = (_silu(_dot(h, wg_ref[...])) * _dot(h, wu_ref[...])).astype(o_ref.dtype)


def _ffn_up(geo, xs, mod_l, nw, w_gate, w_up):
    tm, tf = geo.tm, 512
    return pl.pallas_call(
        _ffn_up_kernel,
        grid=(geo.n_tiles, D_FF // tf),
        in_specs=[
            pl.BlockSpec((tm, D_MODEL), lambda i, j: (i, 0)),
            pl.BlockSpec((None, 6, D_MODEL), lambda i, j: (geo.mod_row(i), 0, 0)),
            pl.BlockSpec((1, D_MODEL), lambda i, j: (0, 0)),
            pl.BlockSpec((D_MODEL, tf), lambda i, j: (0, j)),
            pl.BlockSpec((D_MODEL, tf), lambda i, j: (0, j)),
        ],
        out_specs=pl.BlockSpec((tm, tf), lambda i, j: (i, j)),
        out_shape=jax.ShapeDtypeStruct((geo.rows, D_FF), BF16),
        scratch_shapes=[pltpu.VMEM((tm, D_MODEL), BF16)],
        compiler_params=_params(2),
        name="ffn_up",
    )(xs, mod_l, nw, w_gate, w_up)


MOE_ROW_TILE = 512
MOE_TOKEN_BLOCK = 512
ROUTE_W1, ROUTE_W2, ROUTE_E1, ROUTE_E2 = 8, 9, 10, 11


def _route_kernel(x_ref, mod_ref, nw_ref, router_ref, h_ref, route_ref, cnt_ref, tri_ref, carry_ref):
    i = pl.program_id(0)
    tm = x_ref.shape[0]
    lane = lax.broadcasted_iota(jnp.int32, (tm, LANES), 1)

    @pl.when(i == 0)
    def _():
        ii = lax.broadcasted_iota(jnp.int32, (tm, tm), 0)
        jj = lax.broadcasted_iota(jnp.int32, (tm, tm), 1)
        tri_ref[...] = (jj < ii).astype(BF16)
        carry_ref[...] = jnp.zeros_like(carry_ref)

    h = _norm_mod(x_ref[...], nw_ref[...], mod_ref[3:4, :], mod_ref[4:5, :])
    h_hi, h_lo = _split2(h)
    h_ref[...] = h_hi
    r_hi, r_lo = _split2(router_ref[...])
    logits = _dot(h_hi, r_hi) + _dot(h_hi, r_lo) + _dot(h_lo, r_hi)
    neg = -jnp.inf
    lg = jnp.where(lane < N_EXPERTS, logits, neg)
    m1 = jnp.max(lg, axis=-1, keepdims=True)
    i1 = jnp.min(jnp.where(lg == m1, lane, LANES), axis=-1, keepdims=True)
    lg2 = jnp.where(lane == i1, neg, lg)
    m2 = jnp.max(lg2, axis=-1, keepdims=True)
    i2 = jnp.min(jnp.where(lg2 == m2, lane, LANES), axis=-1, keepdims=True)
    e2 = jnp.exp(m2 - m1)
    w1 = 1.0 / (1.0 + e2)
    w2 = e2 / (1.0 + e2)
    chosen = jnp.logical_or(lane == i1, lane == i2)
    mask = jnp.where(chosen, 1.0, 0.0)
    pos = _dot(tri_ref[...], mask.astype(BF16)) + carry_ref[0:1, :]
    carry_ref[0:1, :] = carry_ref[0:1, :] + jnp.sum(mask, axis=0, keepdims=True)
    rec = jnp.where(lane < N_EXPERTS, pos, 0.0)
    rec = jnp.where(lane == ROUTE_W1, w1, rec)
    rec = jnp.where(lane == ROUTE_W2, w2, rec)
    rec = jnp.where(lane == ROUTE_E1, i1.astype(F32), rec)
    rec = jnp.where(lane == ROUTE_E2, i2.astype(F32), rec)
    route_ref[...] = rec
    cnt_ref[...] = carry_ref[...]


def _slots_kernel(route_ref, start_ref, o_ref):
    rec = route_ref[...]
    lane = lax.broadcasted_iota(jnp.int32, rec.shape, 1)
    slot = rec + start_ref[...]
    e1 = rec[:, ROUTE_E1:ROUTE_E1 + 1].astype(jnp.int32)
    e2 = rec[:, ROUTE_E2:ROUTE_E2 + 1].astype(jnp.int32)
    d1 = jnp.sum(jnp.where(lane == e1, slot, 0.0), axis=-1, keepdims=True)
    d2 = jnp.sum(jnp.where(lane == e2, slot, 0.0), axis=-1, keepdims=True)
    o_ref[...] = jnp.where(lane == 0, d1, jnp.where(lane == 1, d2, 0.0)).astype(jnp.int32)


def _dispatch_kernel(rp_ref, sbp_ref, fp_ref, np_ref, dt_ref, h_ref, o_ref):
    p = pl.program_id(0)
    tg, tb = o_ref.shape[0], h_ref.shape[0]

    @pl.when(p < np_ref[0])
    def _():
        slot = rp_ref[p] * tg + lax.broadcasted_iota(jnp.int32, (tg, tb), 0)
        d = dt_ref[...]
        hit = jnp.logical_or(d[0:1, :] == slot, d[1:2, :] == slot)
        sel = jnp.where(hit, 1.0, 0.0).astype(BF16)
        rows = _dot(sel, h_ref[...])

        @pl.when(fp_ref[p] == 1)
        def _():
            o_ref[...] = rows.astype(o_ref.dtype)

        @pl.when(fp_ref[p] == 0)
        def _():
            o_ref[...] = (o_ref[...].astype(F32) + rows).astype(o_ref.dtype)


def _expert_up_kernel(te_ref, nu_ref, x_ref, wg_ref, wu_ref, o_ref):
    del te_ref
    live = pl.program_id(1) < nu_ref[0]

    @pl.when(live)
    def _():
        x = x_ref[...]
        o_ref[...] = (_silu(_dot(x, wg_ref[...])) * _dot(x, wu_ref[...])).astype(o_ref.dtype)

    @pl.when(jnp.logical_not(live))
    def _():
        o_ref[...] = jnp.zeros_like(o_ref)


def _expert_down_kernel(te_ref, nu_ref, h_ref, wd_ref, o_ref):
    del te_ref
    live = pl.program_id(1) < nu_ref[0]

    @pl.when(live)
    def _():
        o_ref[...] = _dot(h_ref[...], wd_ref[...]).astype(o_ref.dtype)

    @pl.when(jnp.logical_not(live))
    def _():
        o_ref[...] = jnp.zeros_like(o_ref)


def _collect_kernel(sbp_ref, rp_ref, fp_ref, lp_ref, np_ref, x_ref, dest_ref, route_ref, y_ref, mod_ref, o_ref):
    p = pl.program_id(0)
    tb, tg = o_ref.shape[0], y_ref.shape[0]

    @pl.when(p < np_ref[0])
    def _():
        slot = rp_ref[p] * tg + lax.broadcasted_iota(jnp.int32, (tb, tg), 1)
        dest = dest_ref[...]
        rec = route_ref[...]
        sel = (jnp.where(dest[:, 0:1] == slot, rec[:, ROUTE_W1:ROUTE_W1 + 1], 0.0)
               + jnp.where(dest[:, 1:2] == slot, rec[:, ROUTE_W2:ROUTE_W2 + 1], 0.0))
        part = _dot(sel.astype(BF16), y_ref[...])

        @pl.when(fp_ref[p] == 1)
        def _():
            o_ref[...] = part

        @pl.when(fp_ref[p] == 0)
        def _():
            o_ref[...] += part

        @pl.when(lp_ref[p] == 1)
        def _():
            o_ref[...] = x_ref[...] + mod_ref[5:6, :] * o_ref[...]


def _moe(geo, xs, mod_l, nw, router, w_gate, w_up, w_down, tile0, n_tiles):
    tm, tg = geo.tm, MOE_ROW_TILE
    tb = min(MOE_TOKEN_BLOCK, tm)
    rows = n_tiles * tm
    nb = rows // tb
    n_slots = 2 * rows + N_EXPERTS * tg
    n_gt = n_slots // tg
    n_pairs_max = n_gt + nb * N_EXPERTS
    rec_tile = pl.BlockSpec((tm, LANES), lambda i: (i, 0))

    h2, route, counts = pl.pallas_call(
        _route_kernel,
        grid=(n_tiles,),
        in_specs=[
            pl.BlockSpec((tm, D_MODEL), lambda i: (tile0 + i, 0)),
            pl.BlockSpec((None, 6, D_MODEL), lambda i: (geo.mod_row(tile0 + i), 0, 0)),
            pl.BlockSpec((1, D_MODEL), lambda i: (0, 0)),
            pl.BlockSpec((D_MODEL, LANES), lambda i: (0, 0)),
        ],
        out_specs=[pl.BlockSpec((tm, D_MODEL), lambda i: (i, 0)), rec_tile, pl.BlockSpec((8, LANES), lambda i: (0, 0))],
        out_shape=[
            jax.ShapeDtypeStruct((rows, D_MODEL), BF16),
            jax.ShapeDtypeStruct((rows, LANES), F32),
            jax.ShapeDtypeStruct((8, LANES), F32),
        ],
        scratch_shapes=[pltpu.VMEM((tm, tm), BF16), pltpu.VMEM((8, LANES), F32)],
        compiler_params=_params(1),
        name="moe_route",
    )(xs, mod_l, nw, router)

    cnt = counts[0, :N_EXPERTS].astype(jnp.int32)
    padded = ((cnt + tg - 1) // tg) * tg
    ends = jnp.cumsum(padded)
    starts = ends - padded
    n_used = (ends[-1] // tg).astype(jnp.int32).reshape(1)
    tile_expert = jnp.searchsorted(ends, jnp.minimum(jnp.arange(n_gt), n_used[0] - 1) * tg, side="right")
    tile_expert = jnp.minimum(tile_expert, N_EXPERTS - 1).astype(jnp.int32)
    start_row = jnp.zeros((1, LANES), F32).at[0, :N_EXPERTS].set(starts.astype(F32))

    dest = pl.pallas_call(
        _slots_kernel,
        grid=(n_tiles,),
        in_specs=[rec_tile, pl.BlockSpec((1, LANES), lambda i: (0, 0))],
        out_specs=rec_tile,
        out_shape=jax.ShapeDtypeStruct((rows, LANES), jnp.int32),
        compiler_params=_params(1),
        name="moe_slots",
    )(route, start_row)

    pos_lo = route.reshape(nb, tb, LANES)[:, 0, :N_EXPERTS].astype(jnp.int32)
    pos_hi = jnp.concatenate([pos_lo[1:], cnt[None, :]], axis=0)
    t_lo = (starts[None, :] + pos_lo) // tg
    t_hi = (starts[None, :] + pos_hi - 1) // tg
    tiles = jnp.arange(n_gt)[:, None, None]
    share = jnp.any((pos_hi > pos_lo)[None] & (t_lo[None] <= tiles) & (tiles <= t_hi[None]), axis=-1)
    n_pairs = jnp.sum(share).astype(jnp.int32).reshape(1)
    last_valid = jnp.minimum(jnp.arange(n_pairs_max), n_pairs[0] - 1)

    def pair_list(mat):
        flat = jnp.nonzero(mat.ravel(), size=n_pairs_max, fill_value=0)[0][last_valid]
        major, minor = (flat // mat.shape[1]).astype(jnp.int32), (flat % mat.shape[1]).astype(jnp.int32)
        first = jnp.concatenate([jnp.ones((1,), jnp.int32), (major[1:] != major[:-1]).astype(jnp.int32)])
        last = jnp.concatenate([(major[1:] != major[:-1]).astype(jnp.int32), jnp.ones((1,), jnp.int32)])
        last = jnp.where(jnp.arange(n_pairs_max) == n_pairs[0] - 1, 1, last)
        return major, minor, first, last

    d_r, d_sb, d_first, _ = pair_list(share)
    c_sb, c_r, c_first, c_last = pair_list(share.T)
    dest_t = dest[:, :2].T

    xg = pl.pallas_call(
        _dispatch_kernel,
        grid_spec=pltpu.PrefetchScalarGridSpec(
            num_scalar_prefetch=4,
            grid=(n_pairs_max,),
            in_specs=[
                pl.BlockSpec((2, tb), lambda p, rp, sbp, fp, npr: (0, sbp[p])),
                pl.BlockSpec((tb, D_MODEL), lambda p, rp, sbp, fp, npr: (sbp[p], 0)),
            ],
            out_specs=pl.BlockSpec((tg, D_MODEL), lambda p, rp, sbp, fp, npr: (rp[p], 0)),
        ),
        out_shape=jax.ShapeDtypeStruct((n_slots, D_MODEL), BF16),
        compiler_params=_params(1),
        name="moe_dispatch",
    )(d_r, d_sb, d_first, n_pairs, dest_t, h2)

    tf = 1024
    hg = pl.pallas_call(
        _expert_up_kernel,
        grid_spec=pltpu.PrefetchScalarGridSpec(
            num_scalar_prefetch=2,
            grid=(MOE_D_FF // tf, n_gt),
            in_specs=[
                pl.BlockSpec((tg, D_MODEL), lambda j, r, te, nu: (r, 0)),
                pl.BlockSpec((None, D_MODEL, tf), lambda j, r, te, nu: (te[r], 0, j)),
                pl.BlockSpec((None, D_MODEL, tf), lambda j, r, te, nu: (te[r], 0, j)),
            ],
            out_specs=pl.BlockSpec((tg, tf), lambda j, r, te, nu: (r, j)),
        ),
        out_shape=jax.ShapeDtypeStruct((n_slots, MOE_D_FF), BF16),
        compiler_params=_params(2),
        name="moe_expert_up",
    )(tile_expert, n_used, xg, w_gate, w_up)

    tn = 1024
    yg = pl.pallas_call(
        _expert_down_kernel,
        grid_spec=pltpu.PrefetchScalarGridSpec(
            num_scalar_prefetch=2,
            grid=(D_MODEL // tn, n_gt),
            in_specs=[
                pl.BlockSpec((tg, MOE_D_FF), lambda j, r, te, nu: (r, 0)),
                pl.BlockSpec((None, MOE_D_FF, tn), lambda j, r, te, nu: (te[r], 0, j)),
            ],
            out_specs=pl.BlockSpec((tg, tn), lambda j, r, te, nu: (r, j)),
        ),
        out_shape=jax.ShapeDtypeStruct((n_slots, D_MODEL), BF16),
        compiler_params=_params(2),
        name="moe_expert_down",
    )(tile_expert, n_used, hg, w_down)

    per = tm // tb
    return pl.pallas_call(
        _collect_kernel,
        grid_spec=pltpu.PrefetchScalarGridSpec(
            num_scalar_prefetch=5,
            grid=(n_pairs_max,),
            in_specs=[
                pl.BlockSpec((tb, D_MODEL), lambda p, sbp, rp, fp, lp, npr: (tile0 * per + sbp[p], 0)),
                pl.BlockSpec((tb, LANES), lambda p, sbp, rp, fp, lp, npr: (sbp[p], 0)),
                pl.BlockSpec((tb, LANES), lambda p, sbp, rp, fp, lp, npr: (sbp[p], 0)),
                pl.BlockSpec((tg, D_MODEL), lambda p, sbp, rp, fp, lp, npr: (rp[p], 0)),
                pl.BlockSpec((None, 6, D_MODEL), lambda p, sbp, rp, fp, lp, npr: (geo.mod_row(tile0 + sbp[p] // per), 0, 0)),
            ],
            out_specs=pl.BlockSpec((tb, D_MODEL), lambda p, sbp, rp, fp, lp, npr: (sbp[p], 0)),
        ),
        out_shape=jax.ShapeDtypeStruct((rows, D_MODEL), F32),
        compiler_params=_params(1),
        name="moe_collect",
    )(c_sb, c_r, c_first, c_last, n_pairs, xs, dest, route, yg, mod_l)


def _final_norm_kernel(x_ref, w_ref, o_ref):
    x = x_ref[...]
    o_ref[...] = x * lax.rsqrt(jnp.mean(x * x, axis=-1, keepdims=True) + EPS) * w_ref[...]


def _final_norm(x, w, tm):
    rows = x.shape[0]
    return pl.pallas_call(
        _final_norm_kernel,
        grid=(rows // tm,),
        in_specs=[pl.BlockSpec((tm, D_MODEL), lambda i: (i, 0)), pl.BlockSpec((1, D_MODEL), lambda i: (0, 0))],
        out_specs=pl.BlockSpec((tm, D_MODEL), lambda i: (i, 0)),
        out_shape=jax.ShapeDtypeStruct((rows, D_MODEL), F32),
        compiler_params=_params(1),
        name="final_norm",
    )(x, w)


def _rope_tables(geo):
    half = RET_DK // 4
    inv = ROPE_BASE ** (-jnp.arange(half, dtype=F32) / half)
    pos = jnp.arange(geo.seq)
    ang_r = (pos // GRID_W).astype(F32)[:, None] * inv[None, :]
    ang_c = (pos % GRID_W).astype(F32)[:, None] * inv[None, :]
    cos = jnp.concatenate([jnp.cos(ang_r), jnp.cos(ang_r), jnp.cos(ang_c), jnp.cos(ang_c)], axis=1)
    sin = jnp.concatenate([-jnp.sin(ang_r), jnp.sin(ang_r), -jnp.sin(ang_c), jnp.sin(ang_c)], axis=1)
    cos = jnp.concatenate([jnp.ones((geo.tm, LANES), F32), cos], axis=0)
    sin = jnp.concatenate([jnp.zeros((geo.tm, LANES), F32), sin], axis=0)
    return cos, sin


_DT_PERM = np.array([d * SSM_HEADS + g * SSM_HPG + h
                     for g in range(SSM_GROUPS) for d in range(2) for h in range(SSM_HPG)])


def kernel(x, c, ctx, c_ctx, w_ada, b_ada, norm1_w, norm2_w, w_in, conv_w, conv_b, ret_decay_f, ret_decay_b, ret_gn_w, ssm_a_log_f, ssm_a_log_b, ssm_dt_bias_f, ssm_dt_bias_b, ssm_d, ssm_norm_w, w_ret_proj, w_ssm_proj, w_out, ffn_w_gate, ffn_w_up, ffn_w_down, moe_router, moe_w_gate, moe_w_up, moe_w_down, final_norm_w):
    batch, seq, d = x.shape
    ctx_len = ctx.shape[1]
    depth = w_ada.shape[0]
    assert d == D_MODEL and seq % GRID_W == 0
    geo = _Geom(batch, ctx_len, seq)
    tm = geo.tm

    mod_rows = -(-(batch + 1) // 8) * 8
    cvec = jnp.zeros((mod_rows, d), F32).at[0].set(c_ctx).at[1:batch + 1].set(c)
    mod = _modulation(cvec, w_ada, b_ada).reshape(depth, mod_rows, 6, d)
    cos_t, sin_t = _rope_tables(geo)

    xs = jnp.concatenate([ctx.reshape(batch * ctx_len, d), x.reshape(batch * seq, d)], axis=0)
    lat_tile0 = geo.n_ctx_tiles
    n_lat_tiles = geo.n_tiles - geo.n_ctx_tiles

    for i in range(depth):
        mod_l = mod[i]
        w_main = jnp.concatenate([w_in[i][:, ORIG_Z_LO:ORIG_Z_HI], w_in[i][:, :ORIG_Z_LO],
                                  w_in[i][:, ORIG_Z_HI:ORIG_DT_LO], w_in[i][:, ORIG_DT_HI:]], axis=1).astype(BF16)
        w_dt = w_in[i][:, ORIG_DT_LO:ORIG_DT_HI][:, _DT_PERM].astype(BF16)
        proj, dt = _inproj(geo, xs, mod_l, norm1_w[i][None, :], w_main, w_dt, cos_t, sin_t)

        xbc = _conv_silu(geo, proj, conv_w[i], conv_b[i][None, :])
        bias = jnp.concatenate([ssm_dt_bias_f[i], ssm_dt_bias_b[i]])[_DT_PERM][None, :]
        alog = jnp.concatenate([ssm_a_log_f[i], ssm_a_log_b[i]])[_DT_PERM][None, :]
        cc, cct, cpt, dtt = _decay_tables(geo, dt, bias, alog)
        dskip = jnp.repeat(ssm_d[i], SSM_INNER // SSM_HEADS)[None, :]
        ys = _ssd_scan(geo, proj, xbc, cc, cct, cpt, dtt, dskip, ssm_norm_w[i][None, :])
        yr = _ret_scan(geo, proj, jnp.stack([ret_decay_f[i], ret_decay_b[i]]), ret_gn_w[i][None, :])

        merged = _merge(geo, yr, ys, w_ret_proj[i].astype(BF16), w_ssm_proj[i].astype(BF16), proj)
        xs = _matmul_residual(geo, merged, w_out[i].astype(BF16), xs, mod_l, 2, 0, geo.n_tiles, 1024)

        j = i // 2
        tile0, n_tiles = (0, geo.n_tiles) if i < depth - 1 else (lat_tile0, n_lat_tiles)
        if i % 2 == 0:
            hid = _ffn_up(geo, xs, mod_l, norm2_w[i][None, :], ffn_w_gate[j].astype(BF16), ffn_w_up[j].astype(BF16))
            hid = hid[tile0 * tm:]
            xs = _matmul_residual(geo, hid, ffn_w_down[j].astype(BF16), xs, mod_l, 5, tile0, n_tiles, 512)
        else:
            router = jnp.zeros((d, LANES), F32).at[:, :N_EXPERTS].set(moe_router[j])
            xs = _moe(geo, xs, mod_l, norm2_w[i][None, :], router, moe_w_gate[j].astype(BF16),
                      moe_w_up[j].astype(BF16), moe_w_down[j].astype(BF16), tile0, n_tiles)

    lat = xs if xs.shape[0] == batch * seq else xs[batch * ctx_len:]
    return _final_norm(lat, final_norm_w[None, :], tm).reshape(batch, seq, d)
```

```python
import functools

import numpy as np
import jax
import jax.numpy as jnp
from jax import lax
from jax.experimental import pallas as pl
from jax.experimental.pallas import tpu as pltpu

F32 = jnp.float32
BF16 = jnp.bfloat16

D_MODEL = 2048
GRID_W = 64
CHUNK = 128
EPS = 1e-6
ROPE_BASE = 10000.0
RET_HEADS = 8
RET_DK = 128
RET_DV = 256
SSM_INNER = 4096
SSM_HEADS = 64
SSM_GROUPS = 8
SSM_HPG = 8
SSM_GROUP_W = SSM_INNER // SSM_GROUPS
SSM_STATE = 128
XBC_W = 6144
CONV_K = 5
D_FF = 5632
N_EXPERTS = 8
MOE_D_FF = 4096

COL_Z, COL_Q, COL_K, COL_V, COL_G, COL_XBC, COL_GATES = 0, 4096, 5120, 6144, 8192, 10240, 16384
PROJ_W = 20480
ORIG_Z_LO, ORIG_Z_HI, ORIG_DT_LO, ORIG_DT_HI = 6144, 10240, 16384, 16512
SCAN_ROWS = 256

LANES = 128
VMEM_LIMIT_BYTES = 56 * 1024 * 1024


def _params(n_axes, vmem=VMEM_LIMIT_BYTES):
    return pltpu.CompilerParams(dimension_semantics=("arbitrary",) * n_axes, vmem_limit_bytes=vmem)


def _silu(x):
    return x * jax.nn.sigmoid(x)


def _split2(x):
    hi = x.astype(BF16)
    lo = (x - hi.astype(F32)).astype(BF16)
    return hi, lo


def _split3(x):
    hi = x.astype(BF16)
    r = x - hi.astype(F32)
    mid = r.astype(BF16)
    lo = (r - mid.astype(F32)).astype(BF16)
    return hi, mid, lo


def _dot(a, b):
    return jnp.dot(a, b, preferred_element_type=F32)


def _dot_nt(a, b):
    return lax.dot_general(a, b, (((1,), (1,)), ((), ())), preferred_element_type=F32)


def _dot_tn(a, b):
    return lax.dot_general(a, b, (((0,), (0,)), ((), ())), preferred_element_type=F32)


def _norm_mod(x, nw, shift, scale):
    y = x * lax.rsqrt(jnp.mean(x * x, axis=-1, keepdims=True) + EPS)
    return (y * nw) * (1.0 + scale) + shift


def _mod_kernel(c_ref, w_ref, b_ref, o_ref):
    s_hi, s_lo = _split2(_silu(c_ref[...]))
    w_hi, w_lo = _split2(w_ref[...])
    o_ref[...] = _dot(s_hi, w_hi) + _dot(s_hi, w_lo) + _dot(s_lo, w_hi) + b_ref[...]


def _modulation(cvec, w_ada, b_ada):
    depth, d, w6 = w_ada.shape
    rows = cvec.shape[0]
    tn = 512
    return pl.pallas_call(
        _mod_kernel,
        grid=(depth, w6 // tn),
        in_specs=[
            pl.BlockSpec((rows, d), lambda l, j: (0, 0)),
            pl.BlockSpec((None, d, tn), lambda l, j: (l, 0, j)),
            pl.BlockSpec((None, 1, tn), lambda l, j: (l, 0, j)),
        ],
        out_specs=pl.BlockSpec((None, rows, tn), lambda l, j: (l, 0, j)),
        out_shape=jax.ShapeDtypeStruct((depth, rows, w6), F32),
        compiler_params=_params(2),
        name="adaln_mod",
    )(cvec, w_ada, b_ada.reshape(depth, 1, w6))


class _Geom:
    def __init__(self, batch, ctx_len, seq):
        self.batch, self.ctx_len, self.seq = batch, ctx_len, seq
        tm = 1024
        while (batch * ctx_len) % tm or seq % tm:
            tm //= 2
        assert tm >= CHUNK and ctx_len % CHUNK == 0 and seq % CHUNK == 0
        self.tm = tm
        self.ctx_rows = batch * ctx_len
        self.rows = self.ctx_rows + batch * seq
        self.n_ctx_tiles = self.ctx_rows // tm
        self.tiles_per_batch = seq // tm
        self.n_tiles = self.rows // tm
        self.step_rows = min(SCAN_ROWS, ctx_len)
        assert ctx_len % self.step_rows == 0 and seq % self.step_rows == 0 and self.step_rows % CHUNK == 0
        self.ns_ctx = ctx_len // self.step_rows
        self.ns_lat = seq // self.step_rows
        self.ns = self.ns_ctx + self.ns_lat

    def mod_row(self, i):
        return jnp.where(i < self.n_ctx_tiles, 0, 1 + (i - self.n_ctx_tiles) // self.tiles_per_batch)

    def rope_block(self, i):
        return jnp.where(i < self.n_ctx_tiles, 0, 1 + (i - self.n_ctx_tiles) % self.tiles_per_batch)

    def step_block(self, b, c):
        return jnp.where(c < self.ns_ctx, b * self.ns_ctx + c,
                         self.batch * self.ns_ctx + b * self.ns_lat + (c - self.ns_ctx))

    def backward_order(self, s):
        return jnp.where(s < self.ns_ctx, self.ns_ctx - 1 - s, self.ns - 1 - (s - self.ns_ctx))


def _rope_store(acc, cos, sin, o_ref, scale):
    lane = lax.broadcasted_iota(jnp.int32, (acc.shape[0], LANES), 1)
    first_half = (lane % 64) < 32
    for h in range(acc.shape[1] // LANES):
        xh = acc[:, h * LANES:(h + 1) * LANES]
        partner = jnp.where(first_half, pltpu.roll(xh, 96, 1), pltpu.roll(xh, 32, 1))
        o_ref[:, h * LANES:(h + 1) * LANES] = ((xh * cos + partner * sin) * scale).astype(o_ref.dtype)


def _inproj_kernel(x_ref, mod_ref, nw_ref, w_ref, wdt_ref, cos_ref, sin_ref, o_ref, dt_ref, h_ref):
    j = pl.program_id(1)

    @pl.when(j == 0)
    def _():
        h = _norm_mod(x_ref[...], nw_ref[...], mod_ref[0:1, :], mod_ref[1:2, :]).astype(BF16)
        h_ref[...] = h
        dt_ref[...] = _dot(h, wdt_ref[...])

    acc = _dot(h_ref[...], w_ref[...])

    jq, jk = COL_Q // acc.shape[1], COL_K // acc.shape[1]

    @pl.when(j == jq)
    def _():
        _rope_store(acc, cos_ref[...], sin_ref[...], o_ref, 1.0)

    @pl.when(j == jk)
    def _():
        _rope_store(acc, cos_ref[...], sin_ref[...], o_ref, RET_DK ** -0.5)

    @pl.when(jnp.logical_and(j != jq, j != jk))
    def _():
        o_ref[...] = acc.astype(o_ref.dtype)


def _inproj(geo, xs, mod_l, nw, w_main, w_dt, cos_t, sin_t):
    tm, tn = geo.tm, 1024
    return pl.pallas_call(
        _inproj_kernel,
        grid=(geo.n_tiles, PROJ_W // tn),
        in_specs=[
            pl.BlockSpec((tm, D_MODEL), lambda i, j: (i, 0)),
            pl.BlockSpec((None, 6, D_MODEL), lambda i, j: (geo.mod_row(i), 0, 0)),
            pl.BlockSpec((1, D_MODEL), lambda i, j: (0, 0)),
            pl.BlockSpec((D_MODEL, tn), lambda i, j: (0, j)),
            pl.BlockSpec((D_MODEL, LANES), lambda i, j: (0, 0)),
            pl.BlockSpec((tm, LANES), lambda i, j: (geo.rope_block(i), 0)),
            pl.BlockSpec((tm, LANES), lambda i, j: (geo.rope_block(i), 0)),
        ],
        out_specs=[
            pl.BlockSpec((tm, tn), lambda i, j: (i, j)),
            pl.BlockSpec((tm, LANES), lambda i, j: (i, 0)),
        ],
        out_shape=[
            jax.ShapeDtypeStruct((geo.rows, PROJ_W), BF16),
            jax.ShapeDtypeStruct((geo.rows, LANES), F32),
        ],
        scratch_shapes=[pltpu.VMEM((tm, D_MODEL), BF16)],
        compiler_params=_params(2),
        name="in_proj",
    )(xs, mod_l, nw, w_main, w_dt, cos_t, sin_t)


CONV_HALO = 16


def _conv_kernel(prev_ref, main_ref, next_ref, w_ref, b_ref, o_ref, ext_ref, *, geo, rows):
    i = pl.program_id(0)
    start = i * rows
    in_ctx = start < geo.ctx_rows
    seq_len = jnp.where(in_ctx, geo.ctx_len, geo.seq)
    off = jnp.where(in_ctx, start, start - geo.ctx_rows)
    first = (off % seq_len) == 0
    last = ((off + rows) % seq_len) == 0
    ext_ref[0:CONV_HALO, :] = jnp.where(first, 0.0, prev_ref[...].astype(F32))
    ext_ref[CONV_HALO:CONV_HALO + rows, :] = main_ref[...].astype(F32)
    ext_ref[CONV_HALO + rows:, :] = jnp.where(last, 0.0, next_ref[...].astype(F32))
    ext = ext_ref[...]
    n_ext = rows + 2 * CONV_HALO
    acc = b_ref[...] + ext[CONV_HALO:CONV_HALO + rows, :] * w_ref[CONV_K // 2:CONV_K // 2 + 1, :]
    for k in range(CONV_K):
        s = k - CONV_K // 2
        if s != 0:
            acc = acc + pltpu.roll(ext, (-s) % n_ext, 0)[CONV_HALO:CONV_HALO + rows, :] * w_ref[k:k + 1, :]
    o_ref[...] = _silu(acc).astype(o_ref.dtype)


def _conv_silu(geo, proj, conv_w, conv_b):
    rows = min(256, geo.ctx_len)
    tc = 2048
    col0 = COL_XBC // tc
    n_halo = geo.rows // CONV_HALO
    per = rows // CONV_HALO
    return pl.pallas_call(
        functools.partial(_conv_kernel, geo=geo, rows=rows),
        grid=(geo.rows // rows, XBC_W // tc),
        in_specs=[
            pl.BlockSpec((CONV_HALO, tc), lambda i, j: (jnp.maximum(i * per - 1, 0), col0 + j)),
            pl.BlockSpec((rows, tc), lambda i, j: (i, col0 + j)),
            pl.BlockSpec((CONV_HALO, tc), lambda i, j: (jnp.minimum((i + 1) * per, n_halo - 1), col0 + j)),
            pl.BlockSpec((CONV_K, tc), lambda i, j: (0, j)),
            pl.BlockSpec((1, tc), lambda i, j: (0, j)),
        ],
        out_specs=pl.BlockSpec((rows, tc), lambda i, j: (i, j)),
        out_shape=jax.ShapeDtypeStruct((geo.rows, XBC_W), BF16),
        scratch_shapes=[pltpu.VMEM((rows + 2 * CONV_HALO, tc), F32)],
        compiler_params=_params(2),
        name="ssm_conv",
    )(proj, proj, proj, conv_w, conv_b)


def _decay_kernel(dt_ref, bias_ref, alog_ref, cc_ref, cct_ref, cpt_ref, dtt_ref):
    ii = lax.broadcasted_iota(jnp.int32, (CHUNK, CHUNK), 0)
    jj = lax.broadcasted_iota(jnp.int32, (CHUNK, CHUNK), 1)
    lower = (jj <= ii).astype(BF16)
    upper = (jj >= ii).astype(BF16)
    forward = (lax.broadcasted_iota(jnp.int32, (CHUNK, LANES), 1) % 16) < 8
    neg_a = -jnp.exp(alog_ref[...])
    for ci in range(cct_ref.shape[0]):
        rows = pl.ds(ci * CHUNK, CHUNK)
        x = dt_ref[rows, :] + bias_ref[...]
        dtv = jnp.maximum(x, 0.0) + jnp.log1p(jnp.exp(-jnp.abs(x)))
        hi, mid, lo = _split3(dtv * neg_a)
        cum_f = _dot(lower, hi) + _dot(lower, mid) + _dot(lower, lo)
        cum_b = _dot(upper, hi) + _dot(upper, mid) + _dot(upper, lo)
        cc = jnp.where(forward, cum_f, cum_b)
        cc_ref[rows, :] = cc
        cct_ref[ci] = cc.T
        cpt_ref[ci] = (cc - jnp.log(dtv)).T
        dtt_ref[ci] = dtv.T


def _decay_tables(geo, dt, bias, alog):
    n_chunks = geo.rows // CHUNK
    per = max(k for k in (4, 2, 1) if n_chunks % k == 0)
    col = pl.BlockSpec((per * CHUNK, LANES), lambda i: (i, 0))
    row = pl.BlockSpec((per, LANES, CHUNK), lambda i: (i, 0, 0))
    vec = pl.BlockSpec((1, LANES), lambda i: (0, 0))
    col_shape = jax.ShapeDtypeStruct((geo.rows, LANES), F32)
    row_shape = jax.ShapeDtypeStruct((n_chunks, LANES, CHUNK), F32)
    return pl.pallas_call(
        _decay_kernel,
        grid=(n_chunks // per,),
        in_specs=[col, vec, vec],
        out_specs=[col, row, row, row],
        out_shape=[col_shape, row_shape, row_shape, row_shape],
        compiler_params=_params(1),
        name="ssm_decay",
    )(dt, bias, alog)


def _lanes2(a):
    return jnp.concatenate([a, a], axis=1)


def _ret_tables(dec_ref, direction, rows, wk_ref, a_ref, e_ref=None):
    n = float(rows)
    row = lax.broadcasted_iota(jnp.int32, (rows, LANES), 0).astype(F32)
    for h in range(RET_HEADS):
        lam = jnp.exp(jnp.full((rows, LANES), dec_ref[direction, h], F32))
        if direction == 0:
            wk_ref[h] = jnp.exp(-lam * (n - 1.0 - row))
            if e_ref is not None:
                e_ref[0, h] = jnp.exp(-lam * (row + 1.0))
        else:
            wk_ref[h] = jnp.exp(-lam * row)
        a_ref[h] = _lanes2(jnp.exp(-lam[0:8, :] * n))


def _ret_state_kernel(dec_ref, k_ref, v_ref, hist_ref, sb_ref, wk_ref, a_ref, *, rows):
    @pl.when(pl.program_id(1) == 0)
    def _():
        sb_ref[...] = jnp.zeros_like(sb_ref)
        _ret_tables(dec_ref, 1, rows, wk_ref, a_ref)

    def body(h, carry):
        kh = k_ref[:, pl.ds(pl.multiple_of(h * RET_DK, RET_DK), RET_DK)]
        vh = v_ref[:, pl.ds(pl.multiple_of(h * RET_DV, RET_DV), RET_DV)]
        hist_ref[h] = sb_ref[h].astype(BF16)
        kw = (kh.astype(F32) * wk_ref[h]).astype(BF16)
        sb_ref[h] = a_ref[h][0:1, :] * sb_ref[h] + _dot_tn(kw, vh)
        return carry

    lax.fori_loop(0, RET_HEADS, body, 0, unroll=4)


def _ret_out_kernel(dec_ref, q_ref, k_ref, v_ref, g_ref, hist_ref, gnw_ref, o_ref,
                    sf_ref, wk_ref, a_ref, e_ref, m_ref, *, rows):
    @pl.when(pl.program_id(1) == 0)
    def _():
        sf_ref[...] = jnp.zeros_like(sf_ref)
        _ret_tables(dec_ref, 0, rows, wk_ref, a_ref, e_ref)
        n = float(rows)
        row = lax.broadcasted_iota(jnp.int32, (rows, LANES), 0).astype(F32)
        ii = lax.broadcasted_iota(jnp.int32, (rows, rows), 0)
        jj = lax.broadcasted_iota(jnp.int32, (rows, rows), 1)
        dist = (ii - jj).astype(F32)
        for h in range(RET_HEADS):
            lam_f = jnp.exp(jnp.full((rows, 1), dec_ref[0, h], F32))
            lam_b = jnp.exp(jnp.full((rows, 1), dec_ref[1, h], F32))
            m_ref[h] = (jnp.where(jj <= ii, jnp.exp(-lam_f * dist), 0.0)
                        + jnp.where(jj >= ii, jnp.exp(lam_b * dist), 0.0))
            e_ref[1, h] = jnp.exp(-jnp.exp(jnp.full((rows, LANES), dec_ref[1, h], F32)) * (n - row))

    def body(h, carry):
        ok = pl.multiple_of(h * RET_DK, RET_DK)
        ov = pl.multiple_of(h * RET_DV, RET_DV)
        qh = q_ref[:, pl.ds(ok, RET_DK)]
        kh = k_ref[:, pl.ds(ok, RET_DK)]
        vh = v_ref[:, pl.ds(ov, RET_DV)]
        probs = (_dot_nt(qh, kh) * m_ref[h]).astype(BF16)
        y = _dot(probs, vh)
        y = y + _lanes2(e_ref[0, h]) * _dot(qh, sf_ref[h].astype(BF16)) + _lanes2(e_ref[1, h]) * _dot(qh, hist_ref[h])
        kw = (kh.astype(F32) * wk_ref[h]).astype(BF16)
        sf_ref[h] = a_ref[h][0:1, :] * sf_ref[h] + _dot_tn(kw, vh)
        mu = jnp.mean(y, axis=-1, keepdims=True)
        yc = y - mu
        yn = yc * lax.rsqrt(jnp.mean(yc * yc, axis=-1, keepdims=True) + EPS)
        gate = _silu(g_ref[:, pl.ds(ov, RET_DV)].astype(F32))
        o_ref[:, pl.ds(ov, RET_DV)] = (yn * gnw_ref[:, pl.ds(ov, RET_DV)] * gate).astype(o_ref.dtype)
        return carry

    lax.fori_loop(0, RET_HEADS, body, 0, unroll=4)


def _ret_scan(geo, proj, decays, gn_w):
    rows = geo.step_rows
    qk_w, v_w = RET_HEADS * RET_DK, RET_HEADS * RET_DV

    def bwd(b, s):
        return geo.step_block(b, geo.backward_order(s))

    def fwd(b, s):
        return geo.step_block(b, s)

    smem = pl.BlockSpec(memory_space=pltpu.SMEM)
    hist_shape = (geo.batch, geo.ns, RET_HEADS, RET_DK, RET_DV)
    hist = pl.pallas_call(
        functools.partial(_ret_state_kernel, rows=rows),
        grid=(geo.batch, geo.ns),
        in_specs=[
            smem,
            pl.BlockSpec((rows, qk_w), lambda b, s: (bwd(b, s), COL_K // qk_w)),
            pl.BlockSpec((rows, v_w), lambda b, s: (bwd(b, s), COL_V // v_w)),
        ],
        out_specs=pl.BlockSpec((None, None) + hist_shape[2:], lambda b, s: (b, geo.backward_order(s), 0, 0, 0)),
        out_shape=jax.ShapeDtypeStruct(hist_shape, BF16),
        scratch_shapes=[
            pltpu.VMEM((RET_HEADS, RET_DK, RET_DV), F32),
            pltpu.VMEM((RET_HEADS, rows, LANES), F32),
            pltpu.VMEM((RET_HEADS, 8, RET_DV), F32),
        ],
        compiler_params=_params(2),
        name="ret_state",
    )(decays, proj, proj)
    return pl.pallas_call(
        functools.partial(_ret_out_kernel, rows=rows),
        grid=(geo.batch, geo.ns),
        in_specs=[
            smem,
            pl.BlockSpec((rows, qk_w), lambda b, s: (fwd(b, s), COL_Q // qk_w)),
            pl.BlockSpec((rows, qk_w), lambda b, s: (fwd(b, s), COL_K // qk_w)),
            pl.BlockSpec((rows, v_w), lambda b, s: (fwd(b, s), COL_V // v_w)),
            pl.BlockSpec((rows, v_w), lambda b, s: (fwd(b, s), COL_G // v_w)),
            pl.BlockSpec((None, None) + hist_shape[2:], lambda b, s: (b, s, 0, 0, 0)),
            pl.BlockSpec((1, v_w), lambda b, s: (0, 0)),
        ],
        out_specs=pl.BlockSpec((rows, v_w), lambda b, s: (fwd(b, s), 0)),
        out_shape=jax.ShapeDtypeStruct((geo.rows, v_w), BF16),
        scratch_shapes=[
            pltpu.VMEM((RET_HEADS, RET_DK, RET_DV), F32),
            pltpu.VMEM((RET_HEADS, rows, LANES), F32),
            pltpu.VMEM((RET_HEADS, 8, RET_DV), F32),
            pltpu.VMEM((2, RET_HEADS, rows, LANES), F32),
            pltpu.VMEM((RET_HEADS, rows, rows), F32),
        ],
        compiler_params=_params(2),
        name="ret_out",
    )(decays, proj, proj, proj, proj, hist, gn_w)


def _group_cols(ref, rows, g):
    return pltpu.roll(ref[rows, :], (LANES - 16 * g) % LANES, 1)


def _pair_rhs(left, vp):
    zero = jnp.zeros_like(vp)
    return jnp.concatenate([jnp.where(left, vp, zero), jnp.where(left, zero, vp)], axis=0)


def _state_update(s_ref, g, sl, kt, w_rows, h1, h2, a_pair, rhs):
    lhs = jnp.concatenate([(kt * w_rows[h1:h1 + 1, :]).astype(BF16), (kt * w_rows[h2:h2 + 1, :]).astype(BF16)], axis=1)
    s_ref[g, :, sl] = a_pair * s_ref[g, :, sl] + _dot(lhs, rhs)


def _ssd_state_kernel(k_ref, xs_ref, cc_ref, cct_ref, cpt_ref, hist_ref, sb_ref, *, cps):
    @pl.when(pl.program_id(1) == 0)
    def _():
        sb_ref[...] = jnp.zeros_like(sb_ref)

    left = lax.broadcasted_iota(jnp.int32, (CHUNK, LANES), 1) < 64
    left_row = lax.broadcasted_iota(jnp.int32, (1, LANES), 1) < 64

    def body(g, carry):
        ok = pl.multiple_of(g * SSM_STATE, SSM_STATE)
        ov = pl.multiple_of(g * SSM_GROUP_W, SSM_GROUP_W)
        ob = pl.multiple_of(g * 16 + SSM_HPG, SSM_HPG)
        for ci in reversed(range(cps)):
            rows = pl.ds(ci * CHUNK, CHUNK)
            a_all = jnp.exp(_group_cols(cc_ref, rows, g)[0:1, :])
            kt = k_ref[rows, pl.ds(ok, SSM_STATE)].astype(F32).T
            hist_ref[ci, g] = sb_ref[g].astype(BF16)
            w_rows = jnp.exp(cct_ref[ci, pl.ds(ob, SSM_HPG), 0:1] - cpt_ref[ci, pl.ds(ob, SSM_HPG), :])
            for p in range(SSM_HPG // 2):
                h1, h2 = 2 * p, 2 * p + 1
                sl = slice(p * LANES, (p + 1) * LANES)
                vp = xs_ref[rows, pl.ds(pl.multiple_of(ov + p * LANES, LANES), LANES)]
                ab = jnp.where(left_row, a_all[:, SSM_HPG + h1:SSM_HPG + h1 + 1], a_all[:, SSM_HPG + h2:SSM_HPG + h2 + 1])
                _state_update(sb_ref, g, sl, kt, w_rows, h1, h2, ab, _pair_rhs(left, vp))
        return carry

    lax.fori_loop(0, SSM_GROUPS, body, 0, unroll=2)


def _ssd_out_kernel(q_ref, k_ref, xs_ref, z_ref, cc_ref, cct_ref, cpt_ref, dtt_ref, hist_ref,
                    dskip_ref, nw_ref, o_ref, sf_ref, *, cps):
    @pl.when(pl.program_id(1) == 0)
    def _():
        sf_ref[...] = jnp.zeros_like(sf_ref)

    left = lax.broadcasted_iota(jnp.int32, (CHUNK, LANES), 1) < 64
    left_row = lax.broadcasted_iota(jnp.int32, (1, LANES), 1) < 64
    ii = lax.broadcasted_iota(jnp.int32, (CHUNK, CHUNK), 0)
    jj = lax.broadcasted_iota(jnp.int32, (CHUNK, CHUNK), 1)
    lower = jj <= ii
    diag = jj == ii

    def body(g, carry):
        ok = pl.multiple_of(g * SSM_STATE, SSM_STATE)
        ov = pl.multiple_of(g * SSM_GROUP_W, SSM_GROUP_W)
        og = pl.multiple_of(g * 16, 16)
        for ci in range(cps):
            rows = pl.ds(ci * CHUNK, CHUNK)
            cc = _group_cols(cc_ref, rows, g)
            cpt = cpt_ref[ci, pl.ds(og, 16), :]
            dtt = dtt_ref[ci, pl.ds(og, 16), :]
            q = q_ref[rows, pl.ds(ok, SSM_STATE)]
            kt = k_ref[rows, pl.ds(ok, SSM_STATE)].astype(F32).T
            scores = _dot(q, kt.astype(BF16))
            yf = _dot(q, sf_ref[g].astype(BF16))
            yb = _dot(q, hist_ref[ci, g])
            w_rows = jnp.exp(cct_ref[ci, pl.ds(og, SSM_HPG), CHUNK - 1:CHUNK] - cpt[0:SSM_HPG, :])
            a_all = jnp.exp(cc[CHUNK - 1:CHUNK, :])
            ys = []
            for p in range(SSM_HPG // 2):
                h1, h2 = 2 * p, 2 * p + 1
                sl = slice(p * LANES, (p + 1) * LANES)
                cols = pl.ds(pl.multiple_of(ov + p * LANES, LANES), LANES)
                vp = xs_ref[rows, cols]
                probs, ef, eb = [], [], []
                for h in (h1, h2):
                    hb = SSM_HPG + h
                    cf = jnp.broadcast_to(cc[:, h:h + 1], (CHUNK, CHUNK))
                    cb = jnp.broadcast_to(cc[:, hb:hb + 1], (CHUNK, CHUNK))
                    arg = jnp.where(lower, cf - cpt[h:h + 1, :], cb - cpt[hb:hb + 1, :])
                    e = jnp.exp(arg) + jnp.where(diag, dtt[hb:hb + 1, :], 0.0)
                    probs.append((scores * e).astype(BF16))
                    ef.append(jnp.exp(cf))
                    eb.append(jnp.exp(cb))
                rhs = _pair_rhs(left, vp)
                y = _dot(jnp.concatenate(probs, axis=1), rhs)
                y = (y + jnp.where(left, ef[0], ef[1]) * yf[:, sl] + jnp.where(left, eb[0], eb[1]) * yb[:, sl])
                af = jnp.where(left_row, a_all[:, h1:h1 + 1], a_all[:, h2:h2 + 1])
                _state_update(sf_ref, g, sl, kt, w_rows, h1, h2, af, rhs)
                ys.append(y + dskip_ref[:, cols] * vp.astype(F32))
            gcols = pl.ds(ov, SSM_GROUP_W)
            y = jnp.concatenate(ys, axis=1) * _silu(z_ref[rows, gcols].astype(F32))
            y = y * lax.rsqrt(jnp.mean(y * y, axis=-1, keepdims=True) + EPS)
            o_ref[rows, gcols] = (y * nw_ref[:, gcols]).astype(o_ref.dtype)
        return carry

    lax.fori_loop(0, SSM_GROUPS, body, 0)


def _ssd_scan(geo, proj, xbc, cc, cct, cpt, dtt, dskip, norm_w):
    rows = geo.step_rows
    cps = rows // CHUNK
    bc_w = SSM_GROUPS * SSM_STATE

    def bwd(b, s):
        return geo.step_block(b, geo.backward_order(s))

    def fwd(b, s):
        return geo.step_block(b, s)

    hist_shape = (geo.batch, geo.ns * cps, SSM_GROUPS, SSM_STATE, SSM_GROUP_W)
    hist_block = (None, cps) + hist_shape[2:]
    row_bwd = pl.BlockSpec((cps, LANES, CHUNK), lambda b, s: (bwd(b, s), 0, 0))
    hist = pl.pallas_call(
        functools.partial(_ssd_state_kernel, cps=cps),
        grid=(geo.batch, geo.ns),
        in_specs=[
            pl.BlockSpec((rows, bc_w), lambda b, s: (bwd(b, s), SSM_INNER // bc_w)),
            pl.BlockSpec((rows, SSM_INNER), lambda b, s: (bwd(b, s), 0)),
            pl.BlockSpec((rows, LANES), lambda b, s: (bwd(b, s), 0)),
            row_bwd, row_bwd,
        ],
        out_specs=pl.BlockSpec(hist_block, lambda b, s: (b, geo.backward_order(s), 0, 0, 0)),
        out_shape=jax.ShapeDtypeStruct(hist_shape, BF16),
        scratch_shapes=[pltpu.VMEM((SSM_GROUPS, SSM_STATE, SSM_GROUP_W), F32)],
        compiler_params=_params(2),
        name="ssd_state",
    )(xbc, xbc, cc, cct, cpt)
    col = pl.BlockSpec((rows, LANES), lambda b, s: (fwd(b, s), 0))
    row = pl.BlockSpec((cps, LANES, CHUNK), lambda b, s: (fwd(b, s), 0, 0))
    return pl.pallas_call(
        functools.partial(_ssd_out_kernel, cps=cps),
        grid=(geo.batch, geo.ns),
        in_specs=[
            pl.BlockSpec((rows, bc_w), lambda b, s: (fwd(b, s), SSM_INNER // bc_w + 1)),
            pl.BlockSpec((rows, bc_w), lambda b, s: (fwd(b, s), SSM_INNER // bc_w)),
            pl.BlockSpec((rows, SSM_INNER), lambda b, s: (fwd(b, s), 0)),
            pl.BlockSpec((rows, SSM_INNER), lambda b, s: (fwd(b, s), COL_Z // SSM_INNER)),
            col, row, row, row,
            pl.BlockSpec(hist_block, lambda b, s: (b, s, 0, 0, 0)),
            pl.BlockSpec((1, SSM_INNER), lambda b, s: (0, 0)),
            pl.BlockSpec((1, SSM_INNER), lambda b, s: (0, 0)),
        ],
        out_specs=pl.BlockSpec((rows, SSM_INNER), lambda b, s: (fwd(b, s), 0)),
        out_shape=jax.ShapeDtypeStruct((geo.rows, SSM_INNER), BF16),
        scratch_shapes=[pltpu.VMEM((SSM_GROUPS, SSM_STATE, SSM_GROUP_W), F32)],
        compiler_params=_params(2),
        name="ssd_out",
    )(xbc, xbc, xbc, proj, cc, cct, cpt, dtt, hist, dskip, norm_w)


def _merge_kernel(yr_ref, ys_ref, wr_ref, ws_ref, gr_ref, gs_ref, o_ref):
    r = _dot(yr_ref[...], wr_ref[...])
    s = _dot(ys_ref[...], ws_ref[...])
    m = jax.nn.sigmoid(gr_ref[...].astype(F32)) * r + jax.nn.sigmoid(gs_ref[...].astype(F32)) * s
    o_ref[...] = m.astype(o_ref.dtype)


def _merge(geo, yr, ys, w_ret, w_ssm, proj):
    tm, tn = geo.tm, 512
    gr0 = COL_GATES // tn
    gs0 = (COL_GATES + D_MODEL) // tn
    return pl.pallas_call(
        _merge_kernel,
        grid=(geo.n_tiles, D_MODEL // tn),
        in_specs=[
            pl.BlockSpec((tm, yr.shape[1]), lambda i, j: (i, 0)),
            pl.BlockSpec((tm, ys.shape[1]), lambda i, j: (i, 0)),
            pl.BlockSpec((yr.shape[1], tn), lambda i, j: (0, j)),
            pl.BlockSpec((ys.shape[1], tn), lambda i, j: (0, j)),
            pl.BlockSpec((tm, tn), lambda i, j: (i, gr0 + j)),
            pl.BlockSpec((tm, tn), lambda i, j: (i, gs0 + j)),
        ],
        out_specs=pl.BlockSpec((tm, tn), lambda i, j: (i, j)),
        out_shape=jax.ShapeDtypeStruct((geo.rows, D_MODEL), BF16),
        compiler_params=_params(2),
        name="branch_merge",
    )(yr, ys, w_ret, w_ssm, proj, proj)


def _residual_kernel(a_ref, w_ref, x_ref, mod_ref, o_ref, *, gate_row):
    o_ref[...] = x_ref[...] + mod_ref[gate_row:gate_row + 1, :] * _dot(a_ref[...], w_ref[...])


def _matmul_residual(geo, a, w, xs, mod_l, gate_row, tile0, n_tiles, tn):
    tm = geo.tm
    kdim = a.shape[1]
    return pl.pallas_call(
        functools.partial(_residual_kernel, gate_row=gate_row),
        grid=(n_tiles, D_MODEL // tn),
        in_specs=[
            pl.BlockSpec((tm, kdim), lambda i, j: (i, 0)),
            pl.BlockSpec((kdim, tn), lambda i, j: (0, j)),
            pl.BlockSpec((tm, tn), lambda i, j: (tile0 + i, j)),
            pl.BlockSpec((None, 6, tn), lambda i, j: (geo.mod_row(tile0 + i), 0, j)),
        ],
        out_specs=pl.BlockSpec((tm, tn), lambda i, j: (i, j)),
        out_shape=jax.ShapeDtypeStruct((n_tiles * tm, D_MODEL), F32),
        compiler_params=_params(2),
        name="proj_residual",
    )(a, w, xs, mod_l)


def _ffn_up_kernel(x_ref, mod_ref, nw_ref, wg_ref, wu_ref, o_ref, h_ref):
    @pl.when(pl.program_id(1) == 0)
    def _():
        h_ref[...] = _norm_mod(x_ref[...], nw_ref[...], mod_ref[3:4, :], mod_ref[4:5, :]).astype(BF16)

    h = h_ref[...]
    o_ref[...] = (_silu(_dot(h, wg_ref[...])) * _dot(h, wu_ref[...])).astype(o_ref.dtype)


def _ffn_up(geo, xs, mod_l, nw, w_gate, w_up):
    tm, tf = geo.tm, 512
    return pl.pallas_call(
        _ffn_up_kernel,
        grid=(geo.n_tiles, D_FF // tf),
        in_specs=[
            pl.BlockSpec((tm, D_MODEL), lambda i, j: (i, 0)),
            pl.BlockSpec((None, 6, D_MODEL), lambda i, j: (geo.mod_row(i), 0, 0)),
            pl.BlockSpec((1, D_MODEL), lambda i, j: (0, 0)),
            pl.BlockSpec((D_MODEL, tf), lambda i, j: (0, j)),
            pl.BlockSpec((D_MODEL, tf), lambda i, j: (0, j)),
        ],
        out_specs=pl.BlockSpec((tm, tf), lambda i, j: (i, j)),
        out_shape=jax.ShapeDtypeStruct((geo.rows, D_FF), BF16),
        scratch_shapes=[pltpu.VMEM((tm, D_MODEL), BF16)],
        compiler_params=_params(2),
        name="ffn_up",
    )(xs, mod_l, nw, w_gate, w_up)


MOE_ROW_TILE = 512
MOE_TOKEN_BLOCK = 512
ROUTE_W1, ROUTE_W2, ROUTE_E1, ROUTE_E2 = 8, 9, 10, 11


def _route_kernel(x_ref, mod_ref, nw_ref, router_ref, h_ref, route_ref, cnt_ref, tri_ref, carry_ref):
    i = pl.program_id(0)
    tm = x_ref.shape[0]
    lane = lax.broadcasted_iota(jnp.int32, (tm, LANES), 1)

    @pl.when(i == 0)
    def _():
        ii = lax.broadcasted_iota(jnp.int32, (tm, tm), 0)
        jj = lax.broadcasted_iota(jnp.int32, (tm, tm), 1)
        tri_ref[...] = (jj < ii).astype(BF16)
        carry_ref[...] = jnp.zeros_like(carry_ref)

    h = _norm_mod(x_ref[...], nw_ref[...], mod_ref[3:4, :], mod_ref[4:5, :])
    h_hi, h_lo = _split2(h)
    h_ref[...] = h_hi
    r_hi, r_lo = _split2(router_ref[...])
    logits = _dot(h_hi, r_hi) + _dot(h_hi, r_lo) + _dot(h_lo, r_hi)
    neg = -jnp.inf
    lg = jnp.where(lane < N_EXPERTS, logits, neg)
    m1 = jnp.max(lg, axis=-1, keepdims=True)
    i1 = jnp.min(jnp.where(lg == m1, lane, LANES), axis=-1, keepdims=True)
    lg2 = jnp.where(lane == i1, neg, lg)
    m2 = jnp.max(lg2, axis=-1, keepdims=True)
    i2 = jnp.min(jnp.where(lg2 == m2, lane, LANES), axis=-1, keepdims=True)
    e2 = jnp.exp(m2 - m1)
    w1 = 1.0 / (1.0 + e2)
    w2 = e2 / (1.0 + e2)
    chosen = jnp.logical_or(lane == i1, lane == i2)
    mask = jnp.where(chosen, 1.0, 0.0)
    pos = _dot(tri_ref[...], mask.astype(BF16)) + carry_ref[0:1, :]
    carry_ref[0:1, :] = carry_ref[0:1, :] + jnp.sum(mask, axis=0, keepdims=True)
    rec = jnp.where(lane < N_EXPERTS, pos, 0.0)
    rec = jnp.where(lane == ROUTE_W1, w1, rec)
    rec = jnp.where(lane == ROUTE_W2, w2, rec)
    rec = jnp.where(lane == ROUTE_E1, i1.astype(F32), rec)
    rec = jnp.where(lane == ROUTE_E2, i2.astype(F32), rec)
    route_ref[...] = rec
    cnt_ref[...] = carry_ref[...]


def _slots_kernel(route_ref, start_ref, o_ref):
    rec = route_ref[...]
    lane = lax.broadcasted_iota(jnp.int32, rec.shape, 1)
    slot = rec + start_ref[...]
    e1 = rec[:, ROUTE_E1:ROUTE_E1 + 1].astype(jnp.int32)
    e2 = rec[:, ROUTE_E2:ROUTE_E2 + 1].astype(jnp.int32)
    d1 = jnp.sum(jnp.where(lane == e1, slot, 0.0), axis=-1, keepdims=True)
    d2 = jnp.sum(jnp.where(lane == e2, slot, 0.0), axis=-1, keepdims=True)
    o_ref[...] = jnp.where(lane == 0, d1, jnp.where(lane == 1, d2, 0.0)).astype(jnp.int32)


def _dispatch_kernel(rp_ref, sbp_ref, fp_ref, np_ref, dt_ref, h_ref, o_ref):
    p = pl.program_id(0)
    tg, tb = o_ref.shape[0], h_ref.shape[0]

    @pl.when(p < np_ref[0])
    def _():
        slot = rp_ref[p] * tg + lax.broadcasted_iota(jnp.int32, (tg, tb), 0)
        d = dt_ref[...]
        hit = jnp.logical_or(d[0:1, :] == slot, d[1:2, :] == slot)
        sel = jnp.where(hit, 1.0, 0.0).astype(BF16)
        rows = _dot(sel, h_ref[...])

        @pl.when(fp_ref[p] == 1)
        def _():
            o_ref[...] = rows.astype(o_ref.dtype)

        @pl.when(fp_ref[p] == 0)
        def _():
            o_ref[...] = (o_ref[...].astype(F32) + rows).astype(o_ref.dtype)


def _expert_up_kernel(te_ref, nu_ref, x_ref, wg_ref, wu_ref, o_ref):
    del te_ref
    live = pl.program_id(1) < nu_ref[0]

    @pl.when(live)
    def _():
        x = x_ref[...]
        o_ref[...] = (_silu(_dot(x, wg_ref[...])) * _dot(x, wu_ref[...])).astype(o_ref.dtype)

    @pl.when(jnp.logical_not(live))
    def _():
        o_ref[...] = jnp.zeros_like(o_ref)


def _expert_down_kernel(te_ref, nu_ref, h_ref, wd_ref, o_ref):
    del te_ref
    live = pl.program_id(1) < nu_ref[0]

    @pl.when(live)
    def _():
        o_ref[...] = _dot(h_ref[...], wd_ref[...]).astype(o_ref.dtype)

    @pl.when(jnp.logical_not(live))
    def _():
        o_ref[...] = jnp.zeros_like(o_ref)


def _collect_kernel(sbp_ref, rp_ref, fp_ref, lp_ref, np_ref, x_ref, dest_ref, route_ref, y_ref, mod_ref, o_ref):
    p = pl.program_id(0)
    tb, tg = o_ref.shape[0], y_ref.shape[0]

    @pl.when(p < np_ref[0])
    def _():
        slot = rp_ref[p] * tg + lax.broadcasted_iota(jnp.int32, (tb, tg), 1)
        dest = dest_ref[...]
        rec = route_ref[...]
        sel = (jnp.where(dest[:, 0:1] == slot, rec[:, ROUTE_W1:ROUTE_W1 + 1], 0.0)
               + jnp.where(dest[:, 1:2] == slot, rec[:, ROUTE_W2:ROUTE_W2 + 1], 0.0))
        part = _dot(sel.astype(BF16), y_ref[...])

        @pl.when(fp_ref[p] == 1)
        def _():
            o_ref[...] = part

        @pl.when(fp_ref[p] == 0)
        def _():
            o_ref[...] += part

        @pl.when(lp_ref[p] == 1)
        def _():
            o_ref[...] = x_ref[...] + mod_ref[5:6, :] * o_ref[...]


def _moe(geo, xs, mod_l, nw, router, w_gate, w_up, w_down, tile0, n_tiles):
    tm, tg = geo.tm, MOE_ROW_TILE
    tb = min(MOE_TOKEN_BLOCK, tm)
    rows = n_tiles * tm
    nb = rows // tb
    n_slots = 2 * rows + N_EXPERTS * tg
    n_gt = n_slots // tg
    n_pairs_max = n_gt + nb * N_EXPERTS
    rec_tile = pl.BlockSpec((tm, LANES), lambda i: (i, 0))

    h2, route, counts = pl.pallas_call(
        _route_kernel,
        grid=(n_tiles,),
        in_specs=[
            pl.BlockSpec((tm, D_MODEL), lambda i: (tile0 + i, 0)),
            pl.BlockSpec((None, 6, D_MODEL), lambda i: (geo.mod_row(tile0 + i), 0, 0)),
            pl.BlockSpec((1, D_MODEL), lambda i: (0, 0)),
            pl.BlockSpec((D_MODEL, LANES), lambda i: (0, 0)),
        ],
        out_specs=[pl.BlockSpec((tm, D_MODEL), lambda i: (i, 0)), rec_tile, pl.BlockSpec((8, LANES), lambda i: (0, 0))],
        out_shape=[
            jax.ShapeDtypeStruct((rows, D_MODEL), BF16),
            jax.ShapeDtypeStruct((rows, LANES), F32),
            jax.ShapeDtypeStruct((8, LANES), F32),
        ],
        scratch_shapes=[pltpu.VMEM((tm, tm), BF16), pltpu.VMEM((8, LANES), F32)],
        compiler_params=_params(1),
        name="moe_route",
    )(xs, mod_l, nw, router)

    cnt = counts[0, :N_EXPERTS].astype(jnp.int32)
    padded = ((cnt + tg - 1) // tg) * tg
    ends = jnp.cumsum(padded)
    starts = ends - padded
    n_used = (ends[-1] // tg).astype(jnp.int32).reshape(1)
    tile_expert = jnp.searchsorted(ends, jnp.minimum(jnp.arange(n_gt), n_used[0] - 1) * tg, side="right")
    tile_expert = jnp.minimum(tile_expert, N_EXPERTS - 1).astype(jnp.int32)
    start_row = jnp.zeros((1, LANES), F32).at[0, :N_EXPERTS].set(starts.astype(F32))

    dest = pl.pallas_call(
        _slots_kernel,
        grid=(n_tiles,),
        in_specs=[rec_tile, pl.BlockSpec((1, LANES), lambda i: (0, 0))],
        out_specs=rec_tile,
        out_shape=jax.ShapeDtypeStruct((rows, LANES), jnp.int32),
        compiler_params=_params(1),
        name="moe_slots",
    )(route, start_row)

    pos_lo = route.reshape(nb, tb, LANES)[:, 0, :N_EXPERTS].astype(jnp.int32)
    pos_hi = jnp.concatenate([pos_lo[1:], cnt[None, :]], axis=0)
    t_lo = (starts[None, :] + pos_lo) // tg
    t_hi = (starts[None, :] + pos_hi - 1) // tg
    tiles = jnp.arange(n_gt)[:, None, None]
    share = jnp.any((pos_hi > pos_lo)[None] & (t_lo[None] <= tiles) & (tiles <= t_hi[None]), axis=-1)
    n_pairs = jnp.sum(share).astype(jnp.int32).reshape(1)
    last_valid = jnp.minimum(jnp.arange(n_pairs_max), n_pairs[0] - 1)

    def pair_list(mat):
        flat = jnp.nonzero(mat.ravel(), size=n_pairs_max, fill_value=0)[0][last_valid]
        major, minor = (flat // mat.shape[1]).astype(jnp.int32), (flat % mat.shape[1]).astype(jnp.int32)
        first = jnp.concatenate([jnp.ones((1,), jnp.int32), (major[1:] != major[:-1]).astype(jnp.int32)])
        last = jnp.concatenate([(major[1:] != major[:-1]).astype(jnp.int32), jnp.ones((1,), jnp.int32)])
        last = jnp.where(jnp.arange(n_pairs_max) == n_pairs[0] - 1, 1, last)
        return major, minor, first, last

    d_r, d_sb, d_first, _ = pair_list(share)
    c_sb, c_r, c_first, c_last = pair_list(share.T)
    dest_t = dest[:, :2].T

    xg = pl.pallas_call(
        _dispatch_kernel,
        grid_spec=pltpu.PrefetchScalarGridSpec(
            num_scalar_prefetch=4,
            grid=(n_pairs_max,),
            in_specs=[
                pl.BlockSpec((2, tb), lambda p, rp, sbp, fp, npr: (0, sbp[p])),
                pl.BlockSpec((tb, D_MODEL), lambda p, rp, sbp, fp, npr: (sbp[p], 0)),
            ],
            out_specs=pl.BlockSpec((tg, D_MODEL), lambda p, rp, sbp, fp, npr: (rp[p], 0)),
        ),
        out_shape=jax.ShapeDtypeStruct((n_slots, D_MODEL), BF16),
        compiler_params=_params(1),
        name="moe_dispatch",
    )(d_r, d_sb, d_first, n_pairs, dest_t, h2)

    tf = 1024
    hg = pl.pallas_call(
        _expert_up_kernel,
        grid_spec=pltpu.PrefetchScalarGridSpec(
            num_scalar_prefetch=2,
            grid=(MOE_D_FF // tf, n_gt),
            in_specs=[
                pl.BlockSpec((tg, D_MODEL), lambda j, r, te, nu: (r, 0)),
                pl.BlockSpec((None, D_MODEL, tf), lambda j, r, te, nu: (te[r], 0, j)),
                pl.BlockSpec((None, D_MODEL, tf), lambda j, r, te, nu: (te[r], 0, j)),
            ],
            out_specs=pl.BlockSpec((tg, tf), lambda j, r, te, nu: (r, j)),
        ),
        out_shape=jax.ShapeDtypeStruct((n_slots, MOE_D_FF), BF16),
        compiler_params=_params(2),
        name="moe_expert_up",
    )(tile_expert, n_used, xg, w_gate, w_up)

    tn = 1024
    yg = pl.pallas_call(
        _expert_down_kernel,
        grid_spec=pltpu.PrefetchScalarGridSpec(
            num_scalar_prefetch=2,
            grid=(D_MODEL // tn, n_gt),
            in_specs=[
                pl.BlockSpec((tg, MOE_D_FF), lambda j, r, te, nu: (r, 0)),
                pl.BlockSpec((None, MOE_D_FF, tn), lambda j, r, te, nu: (te[r], 0, j)),
            ],
            out_specs=pl.BlockSpec((tg, tn), lambda j, r, te, nu: (r, j)),
        ),
        out_shape=jax.ShapeDtypeStruct((n_slots, D_MODEL), BF16),
        compiler_params=_params(2),
        name="moe_expert_down",
    )(tile_expert, n_used, hg, w_down)

    per = tm // tb
    return pl.pallas_call(
        _collect_kernel,
        grid_spec=pltpu.PrefetchScalarGridSpec(
            num_scalar_prefetch=5,
            grid=(n_pairs_max,),
            in_specs=[
                pl.BlockSpec((tb, D_MODEL), lambda p, sbp, rp, fp, lp, npr: (tile0 * per + sbp[p], 0)),
                pl.BlockSpec((tb, LANES), lambda p, sbp, rp, fp, lp, npr: (sbp[p], 0)),
                pl.BlockSpec((tb, LANES), lambda p, sbp, rp, fp, lp, npr: (sbp[p], 0)),
                pl.BlockSpec((tg, D_MODEL), lambda p, sbp, rp, fp, lp, npr: (rp[p], 0)),
                pl.BlockSpec((None, 6, D_MODEL), lambda p, sbp, rp, fp, lp, npr: (geo.mod_row(tile0 + sbp[p] // per), 0, 0)),
            ],
            out_specs=pl.BlockSpec((tb, D_MODEL), lambda p, sbp, rp, fp, lp, npr: (sbp[p], 0)),
        ),
        out_shape=jax.ShapeDtypeStruct((rows, D_MODEL), F32),
        compiler_params=_params(1),
        name="moe_collect",
    )(c_sb, c_r, c_first, c_last, n_pairs, xs, dest, route, yg, mod_l)


def _final_norm_kernel(x_ref, w_ref, o_ref):
    x = x_ref[...]
    o_ref[...] = x * lax.rsqrt(jnp.mean(x * x, axis=-1, keepdims=True) + EPS) * w_ref[...]


def _final_norm(x, w, tm):
    rows = x.shape[0]
    return pl.pallas_call(
        _final_norm_kernel,
        grid=(rows // tm,),
        in_specs=[pl.BlockSpec((tm, D_MODEL), lambda i: (i, 0)), pl.BlockSpec((1, D_MODEL), lambda i: (0, 0))],
        out_specs=pl.BlockSpec((tm, D_MODEL), lambda i: (i, 0)),
        out_shape=jax.ShapeDtypeStruct((rows, D_MODEL), F32),
        compiler_params=_params(1),
        name="final_norm",
    )(x, w)


def _rope_tables(geo):
    half = RET_DK // 4
    inv = ROPE_BASE ** (-jnp.arange(half, dtype=F32) / half)
    pos = jnp.arange(geo.seq)
    ang_r = (pos // GRID_W).astype(F32)[:, None] * inv[None, :]
    ang_c = (pos % GRID_W).astype(F32)[:, None] * inv[None, :]
    cos = jnp.concatenate([jnp.cos(ang_r), jnp.cos(ang_r), jnp.cos(ang_c), jnp.cos(ang_c)], axis=1)
    sin = jnp.concatenate([-jnp.sin(ang_r), jnp.sin(ang_r), -jnp.sin(ang_c), jnp.sin(ang_c)], axis=1)
    cos = jnp.concatenate([jnp.ones((geo.tm, LANES), F32), cos], axis=0)
    sin = jnp.concatenate([jnp.zeros((geo.tm, LANES), F32), sin], axis=0)
    return cos, sin


_DT_PERM = np.array([d * SSM_HEADS + g * SSM_HPG + h
                     for g in range(SSM_GROUPS) for d in range(2) for h in range(SSM_HPG)])


def kernel(x, c, ctx, c_ctx, w_ada, b_ada, norm1_w, norm2_w, w_in, conv_w, conv_b, ret_decay_f, ret_decay_b, ret_gn_w, ssm_a_log_f, ssm_a_log_b, ssm_dt_bias_f, ssm_dt_bias_b, ssm_d, ssm_norm_w, w_ret_proj, w_ssm_proj, w_out, ffn_w_gate, ffn_w_up, ffn_w_down, moe_router, moe_w_gate, moe_w_up, moe_w_down, final_norm_w):
    batch, seq, d = x.shape
    ctx_len = ctx.shape[1]
    depth = w_ada.shape[0]
    assert d == D_MODEL and seq % GRID_W == 0
    geo = _Geom(batch, ctx_len, seq)
    tm = geo.tm

    mod_rows = -(-(batch + 1) // 8) * 8
    cvec = jnp.zeros((mod_rows, d), F32).at[0].set(c_ctx).at[1:batch + 1].set(c)
    mod = _modulation(cvec, w_ada, b_ada).reshape(depth, mod_rows, 6, d)
    cos_t, sin_t = _rope_tables(geo)

    xs = jnp.concatenate([ctx.reshape(batch * ctx_len, d), x.reshape(batch * seq, d)], axis=0)
    lat_tile0 = geo.n_ctx_tiles
    n_lat_tiles = geo.n_tiles - geo.n_ctx_tiles

    for i in range(depth):
        mod_l = mod[i]
        w_main = jnp.concatenate([w_in[i][:, ORIG_Z_LO:ORIG_Z_HI], w_in[i][:, :ORIG_Z_LO],
                                  w_in[i][:, ORIG_Z_HI:ORIG_DT_LO], w_in[i][:, ORIG_DT_HI:]], axis=1).astype(BF16)
        w_dt = w_in[i][:, ORIG_DT_LO:ORIG_DT_HI][:, _DT_PERM].astype(BF16)
        proj, dt = _inproj(geo, xs, mod_l, norm1_w[i][None, :], w_main, w_dt, cos_t, sin_t)

        xbc = _conv_silu(geo, proj, conv_w[i], conv_b[i][None, :])
        bias = jnp.concatenate([ssm_dt_bias_f[i], ssm_dt_bias_b[i]])[_DT_PERM][None, :]
        alog = jnp.concatenate([ssm_a_log_f[i], ssm_a_log_b[i]])[_DT_PERM][None, :]
        cc, cct, cpt, dtt = _decay_tables(geo, dt, bias, alog)
        dskip = jnp.repeat(ssm_d[i], SSM_INNER // SSM_HEADS)[None, :]
        ys = _ssd_scan(geo, proj, xbc, cc, cct, cpt, dtt, dskip, ssm_norm_w[i][None, :])
        yr = _ret_scan(geo, proj, jnp.stack([ret_decay_f[i], ret_decay_b[i]]), ret_gn_w[i][None, :])

        merged = _merge(geo, yr, ys, w_ret_proj[i].astype(BF16), w_ssm_proj[i].astype(BF16), proj)
        xs = _matmul_residual(geo, merged, w_out[i].astype(BF16), xs, mod_l, 2, 0, geo.n_tiles, 1024)

        j = i // 2
        tile0, n_tiles = (0, geo.n_tiles) if i < depth - 1 else (lat_tile0, n_lat_tiles)
        if i % 2 == 0:
            hid = _ffn_up(geo, xs, mod_l, norm2_w[i][None, :], ffn_w_gate[j].astype(BF16), ffn_w_up[j].astype(BF16))
            hid = hid[tile0 * tm:]
            xs = _matmul_residual(geo, hid, ffn_w_down[j].astype(BF16), xs, mod_l, 5, tile0, n_tiles, 512)
        else:
            router = jnp.zeros((d, LANES), F32).at[:, :N_EXPERTS].set(moe_router[j])
            xs = _moe(geo, xs, mod_l, norm2_w[i][None, :], router, moe_w_gate[j].astype(BF16),
                      moe_w_up[j].astype(BF16), moe_w_down[j].astype(BF16), tile0, n_tiles)

    lat = xs if xs.shape[0] == batch * seq else xs[batch * ctx_len:]
    return _final_norm(lat, final_norm_w[None, :], tm).reshape(batch, seq, d)
```

```python
import functools

import numpy as np
import jax
import jax.numpy as jnp
from jax import lax
from jax.experimental import pallas as pl
from jax.experimental.pallas import tpu as pltpu

F32 = jnp.float32
BF16 = jnp.bfloat16

D_MODEL = 2048
GRID_W = 64
CHUNK = 128
EPS = 1e-6
ROPE_BASE = 10000.0
RET_HEADS = 8
RET_DK = 128
RET_DV = 256
SSM_INNER = 4096
SSM_HEADS = 64
SSM_GROUPS = 8
SSM_HPG = 8
SSM_GROUP_W = SSM_INNER // SSM_GROUPS
SSM_STATE = 128
XBC_W = 6144
CONV_K = 5
D_FF = 5632
N_EXPERTS = 8
MOE_D_FF = 4096

COL_Z, COL_Q, COL_K, COL_V, COL_G, COL_XBC, COL_GATES = 0, 4096, 5120, 6144, 8192, 10240, 16384
PROJ_W = 20480
ORIG_Z_LO, ORIG_Z_HI, ORIG_DT_LO, ORIG_DT_HI = 6144, 10240, 16384, 16512
SCAN_ROWS = 256

LANES = 128
VMEM_LIMIT_BYTES = 56 * 1024 * 1024


def _params(n_axes, vmem=VMEM_LIMIT_BYTES):
    return pltpu.CompilerParams(dimension_semantics=("arbitrary",) * n_axes, vmem_limit_bytes=vmem)


def _silu(x):
    return x * jax.nn.sigmoid(x)


def _split2(x):
    hi = x.astype(BF16)
    lo = (x - hi.astype(F32)).astype(BF16)
    return hi, lo


def _split3(x):
    hi = x.astype(BF16)
    r = x - hi.astype(F32)
    mid = r.astype(BF16)
    lo = (r - mid.astype(F32)).astype(BF16)
    return hi, mid, lo


def _dot(a, b):
    return jnp.dot(a, b, preferred_element_type=F32)


def _dot_nt(a, b):
    return lax.dot_general(a, b, (((1,), (1,)), ((), ())), preferred_element_type=F32)


def _dot_tn(a, b):
    return lax.dot_general(a, b, (((0,), (0,)), ((), ())), preferred_element_type=F32)


def _norm_mod(x, nw, shift, scale):
    y = x * lax.rsqrt(jnp.mean(x * x, axis=-1, keepdims=True) + EPS)
    return (y * nw) * (1.0 + scale) + shift


def _mod_kernel(c_ref, w_ref, b_ref, o_ref):
    s_hi, s_lo = _split2(_silu(c_ref[...]))
    w_hi, w_lo = _split2(w_ref[...])
    o_ref[...] = _dot(s_hi, w_hi) + _dot(s_hi, w_lo) + _dot(s_lo, w_hi) + b_ref[...]


def _modulation(cvec, w_ada, b_ada):
    depth, d, w6 = w_ada.shape
    rows = cvec.shape[0]
    tn = 512
    return pl.pallas_call(
        _mod_kernel,
        grid=(depth, w6 // tn),
        in_specs=[
            pl.BlockSpec((rows, d), lambda l, j: (0, 0)),
            pl.BlockSpec((None, d, tn), lambda l, j: (l, 0, j)),
            pl.BlockSpec((None, 1, tn), lambda l, j: (l, 0, j)),
        ],
        out_specs=pl.BlockSpec((None, rows, tn), lambda l, j: (l, 0, j)),
        out_shape=jax.ShapeDtypeStruct((depth, rows, w6), F32),
        compiler_params=_params(2),
        name="adaln_mod",
    )(cvec, w_ada, b_ada.reshape(depth, 1, w6))


class _Geom:
    def __init__(self, batch, ctx_len, seq):
        self.batch, self.ctx_len, self.seq = batch, ctx_len, seq
        tm = 1024
        while (batch * ctx_len) % tm or seq % tm:
            tm //= 2
        assert tm >= CHUNK and ctx_len % CHUNK == 0 and seq % CHUNK == 0
        self.tm = tm
        self.ctx_rows = batch * ctx_len
        self.rows = self.ctx_rows + batch * seq
        self.n_ctx_tiles = self.ctx_rows // tm
        self.tiles_per_batch = seq // tm
        self.n_tiles = self.rows // tm
        self.step_rows = min(SCAN_ROWS, ctx_len)
        assert ctx_len % self.step_rows == 0 and seq % self.step_rows == 0 and self.step_rows % CHUNK == 0
        self.ns_ctx = ctx_len // self.step_rows
        self.ns_lat = seq // self.step_rows
        self.ns = self.ns_ctx + self.ns_lat

    def mod_row(self, i):
        return jnp.where(i < self.n_ctx_tiles, 0, 1 + (i - self.n_ctx_tiles) // self.tiles_per_batch)

    def rope_block(self, i):
        return jnp.where(i < self.n_ctx_tiles, 0, 1 + (i - self.n_ctx_tiles) % self.tiles_per_batch)

    def step_block(self, b, c):
        return jnp.where(c < self.ns_ctx, b * self.ns_ctx + c,
                         self.batch * self.ns_ctx + b * self.ns_lat + (c - self.ns_ctx))

    def backward_order(self, s):
        return jnp.where(s < self.ns_ctx, self.ns_ctx - 1 - s, self.ns - 1 - (s - self.ns_ctx))


def _rope_store(acc, cos, sin, o_ref, scale):
    lane = lax.broadcasted_iota(jnp.int32, (acc.shape[0], LANES), 1)
    first_half = (lane % 64) < 32
    for h in range(acc.shape[1] // LANES):
        xh = acc[:, h * LANES:(h + 1) * LANES]
        partner = jnp.where(first_half, pltpu.roll(xh, 96, 1), pltpu.roll(xh, 32, 1))
        o_ref[:, h * LANES:(h + 1) * LANES] = ((xh * cos + partner * sin) * scale).astype(o_ref.dtype)


def _inproj_kernel(x_ref, mod_ref, nw_ref, w_ref, wdt_ref, cos_ref, sin_ref, o_ref, dt_ref, h_ref):
    j = pl.program_id(1)

    @pl.when(j == 0)
    def _():
        h = _norm_mod(x_ref[...], nw_ref[...], mod_ref[0:1, :], mod_ref[1:2, :]).astype(BF16)
        h_ref[...] = h
        dt_ref[...] = _dot(h, wdt_ref[...])

    acc = _dot(h_ref[...], w_ref[...])

    jq, jk = COL_Q // acc.shape[1], COL_K // acc.shape[1]

    @pl.when(j == jq)
    def _():
        _rope_store(acc, cos_ref[...], sin_ref[...], o_ref, 1.0)

    @pl.when(j == jk)
    def _():
        _rope_store(acc, cos_ref[...], sin_ref[...], o_ref, RET_DK ** -0.5)

    @pl.when(jnp.logical_and(j != jq, j != jk))
    def _():
        o_ref[...] = acc.astype(o_ref.dtype)


def _inproj(geo, xs, mod_l, nw, w_main, w_dt, cos_t, sin_t):
    tm, tn = geo.tm, 1024
    return pl.pallas_call(
        _inproj_kernel,
        grid=(geo.n_tiles, PROJ_W // tn),
        in_specs=[
            pl.BlockSpec((tm, D_MODEL), lambda i, j: (i, 0)),
            pl.BlockSpec((None, 6, D_MODEL), lambda i, j: (geo.mod_row(i), 0, 0)),
            pl.BlockSpec((1, D_MODEL), lambda i, j: (0, 0)),
            pl.BlockSpec((D_MODEL, tn), lambda i, j: (0, j)),
            pl.BlockSpec((D_MODEL, LANES), lambda i, j: (0, 0)),
            pl.BlockSpec((tm, LANES), lambda i, j: (geo.rope_block(i), 0)),
            pl.BlockSpec((tm, LANES), lambda i, j: (geo.rope_block(i), 0)),
        ],
        out_specs=[
            pl.BlockSpec((tm, tn), lambda i, j: (i, j)),
            pl.BlockSpec((tm, LANES), lambda i, j: (i, 0)),
        ],
        out_shape=[
            jax.ShapeDtypeStruct((geo.rows, PROJ_W), BF16),
            jax.ShapeDtypeStruct((geo.rows, LANES), F32),
        ],
        scratch_shapes=[pltpu.VMEM((tm, D_MODEL), BF16)],
        compiler_params=_params(2),
        name="in_proj",
    )(xs, mod_l, nw, w_main, w_dt, cos_t, sin_t)


CONV_HALO = 16


def _conv_kernel(prev_ref, main_ref, next_ref, w_ref, b_ref, o_ref, ext_ref, *, geo, rows):
    i = pl.program_id(0)
    start = i * rows
    in_ctx = start < geo.ctx_rows
    seq_len = jnp.where(in_ctx, geo.ctx_len, geo.seq)
    off = jnp.where(in_ctx, start, start - geo.ctx_rows)
    first = (off % seq_len) == 0
    last = ((off + rows) % seq_len) == 0
    ext_ref[0:CONV_HALO, :] = jnp.where(first, 0.0, prev_ref[...].astype(F32))
    ext_ref[CONV_HALO:CONV_HALO + rows, :] = main_ref[...].astype(F32)
    ext_ref[CONV_HALO + rows:, :] = jnp.where(last, 0.0, next_ref[...].astype(F32))
    ext = ext_ref[...]
    n_ext = rows + 2 * CONV_HALO
    acc = b_ref[...] + ext[CONV_HALO:CONV_HALO + rows, :] * w_ref[CONV_K // 2:CONV_K // 2 + 1, :]
    for k in range(CONV_K):
        s = k - CONV_K // 2
        if s != 0:
            acc = acc + pltpu.roll(ext, (-s) % n_ext, 0)[CONV_HALO:CONV_HALO + rows, :] * w_ref[k:k + 1, :]
    o_ref[...] = _silu(acc).astype(o_ref.dtype)


def _conv_silu(geo, proj, conv_w, conv_b):
    rows = min(256, geo.ctx_len)
    tc = 2048
    col0 = COL_XBC // tc
    n_halo = geo.rows // CONV_HALO
    per = rows // CONV_HALO
    return pl.pallas_call(
        functools.partial(_conv_kernel, geo=geo, rows=rows),
        grid=(geo.rows // rows, XBC_W // tc),
        in_specs=[
            pl.BlockSpec((CONV_HALO, tc), lambda i, j: (jnp.maximum(i * per - 1, 0), col0 + j)),
            pl.BlockSpec((rows, tc), lambda i, j: (i, col0 + j)),
            pl.BlockSpec((CONV_HALO, tc), lambda i, j: (jnp.minimum((i + 1) * per, n_halo - 1), col0 + j)),
            pl.BlockSpec((CONV_K, tc), lambda i, j: (0, j)),
            pl.BlockSpec((1, tc), lambda i, j: (0, j)),
        ],
        out_specs=pl.BlockSpec((rows, tc), lambda i, j: (i, j)),
        out_shape=jax.ShapeDtypeStruct((geo.rows, XBC_W), BF16),
        scratch_shapes=[pltpu.VMEM((rows + 2 * CONV_HALO, tc), F32)],
        compiler_params=_params(2),
        name="ssm_conv",
    )(proj, proj, proj, conv_w, conv_b)


def _decay_kernel(dt_ref, bias_ref, alog_ref, cc_ref, cct_ref, cpt_ref, dtt_ref):
    ii = lax.broadcasted_iota(jnp.int32, (CHUNK, CHUNK), 0)
    jj = lax.broadcasted_iota(jnp.int32, (CHUNK, CHUNK), 1)
    lower = (jj <= ii).astype(BF16)
    upper = (jj >= ii).astype(BF16)
    forward = (lax.broadcasted_iota(jnp.int32, (CHUNK, LANES), 1) % 16) < 8
    neg_a = -jnp.exp(alog_ref[...])
    for ci in range(cct_ref.shape[0]):
        rows = pl.ds(ci * CHUNK, CHUNK)
        x = dt_ref[rows, :] + bias_ref[...]
        dtv = jnp.maximum(x, 0.0) + jnp.log1p(jnp.exp(-jnp.abs(x)))
        hi, mid, lo = _split3(dtv * neg_a)
        cum_f = _dot(lower, hi) + _dot(lower, mid) + _dot(lower, lo)
        cum_b = _dot(upper, hi) + _dot(upper, mid) + _dot(upper, lo)
        cc = jnp.where(forward, cum_f, cum_b)
        cc_ref[rows, :] = cc
        cct_ref[ci] = cc.T
        cpt_ref[ci] = (cc - jnp.log(dtv)).T
        dtt_ref[ci] = dtv.T


def _decay_tables(geo, dt, bias, alog):
    n_chunks = geo.rows // CHUNK
    per = max(k for k in (4, 2, 1) if n_chunks % k == 0)
    col = pl.BlockSpec((per * CHUNK, LANES), lambda i: (i, 0))
    row = pl.BlockSpec((per, LANES, CHUNK), lambda i: (i, 0, 0))
    vec = pl.BlockSpec((1, LANES), lambda i: (0, 0))
    col_shape = jax.ShapeDtypeStruct((geo.rows, LANES), F32)
    row_shape = jax.ShapeDtypeStruct((n_chunks, LANES, CHUNK), F32)
    return pl.pallas_call(
        _decay_kernel,
        grid=(n_chunks // per,),
        in_specs=[col, vec, vec],
        out_specs=[col, row, row, row],
        out_shape=[col_shape, row_shape, row_shape, row_shape],
        compiler_params=_params(1),
        name="ssm_decay",
    )(dt, bias, alog)


def _lanes2(a):
    return jnp.concatenate([a, a], axis=1)


def _ret_tables(dec_ref, direction, rows, wk_ref, a_ref, e_ref=None):
    n = float(rows)
    row = lax.broadcasted_iota(jnp.int32, (rows, LANES), 0).astype(F32)
    for h in range(RET_HEADS):
        lam = jnp.exp(jnp.full((rows, LANES), dec_ref[direction, h], F32))
        if direction == 0:
            wk_ref[h] = jnp.exp(-lam * (n - 1.0 - row))
            if e_ref is not None:
                e_ref[0, h] = jnp.exp(-lam * (row + 1.0))
        else:
            wk_ref[h] = jnp.exp(-lam * row)
        a_ref[h] = _lanes2(jnp.exp(-lam[0:8, :] * n))


def _ret_state_kernel(dec_ref, k_ref, v_ref, hist_ref, sb_ref, wk_ref, a_ref, *, rows):
    @pl.when(pl.program_id(1) == 0)
    def _():
        sb_ref[...] = jnp.zeros_like(sb_ref)
        _ret_tables(dec_ref, 1, rows, wk_ref, a_ref)

    def body(h, carry):
        kh = k_ref[:, pl.ds(pl.multiple_of(h * RET_DK, RET_DK), RET_DK)]
        vh = v_ref[:, pl.ds(pl.multiple_of(h * RET_DV, RET_DV), RET_DV)]
        hist_ref[h] = sb_ref[h].astype(BF16)
        kw = (kh.astype(F32) * wk_ref[h]).astype(BF16)
        sb_ref[h] = a_ref[h][0:1, :] * sb_ref[h] + _dot_tn(kw, vh)
        return carry

    lax.fori_loop(0, RET_HEADS, body, 0, unroll=8)


def _ret_out_kernel(dec_ref, q_ref, k_ref, v_ref, g_ref, hist_ref, gnw_ref, o_ref,
                    sf_ref, wk_ref, a_ref, e_ref, m_ref, *, rows):
    @pl.when(pl.program_id(1) == 0)
    def _():
        sf_ref[...] = jnp.zeros_like(sf_ref)
        _ret_tables(dec_ref, 0, rows, wk_ref, a_ref, e_ref)
        n = float(rows)
        row = lax.broadcasted_iota(jnp.int32, (rows, LANES), 0).astype(F32)
        ii = lax.broadcasted_iota(jnp.int32, (rows, rows), 0)
        jj = lax.broadcasted_iota(jnp.int32, (rows, rows), 1)
        dist = (ii - jj).astype(F32)
        for h in range(RET_HEADS):
            lam_f = jnp.exp(jnp.full((rows, 1), dec_ref[0, h], F32))
            lam_b = jnp.exp(jnp.full((rows, 1), dec_ref[1, h], F32))
            m_ref[h] = (jnp.where(jj <= ii, jnp.exp(-lam_f * dist), 0.0)
                        + jnp.where(jj >= ii, jnp.exp(lam_b * dist), 0.0))
            e_ref[1, h] = jnp.exp(-jnp.exp(jnp.full((rows, LANES), dec_ref[1, h], F32)) * (n - row))

    def body(h, carry):
        ok = pl.multiple_of(h * RET_DK, RET_DK)
        ov = pl.multiple_of(h * RET_DV, RET_DV)
        qh = q_ref[:, pl.ds(ok, RET_DK)]
        kh = k_ref[:, pl.ds(ok, RET_DK)]
        vh = v_ref[:, pl.ds(ov, RET_DV)]
        probs = (_dot_nt(qh, kh) * m_ref[h]).astype(BF16)
        y = _dot(probs, vh)
        y = y + _lanes2(e_ref[0, h]) * _dot(qh, sf_ref[h].astype(BF16)) + _lanes2(e_ref[1, h]) * _dot(qh, hist_ref[h])
        kw = (kh.astype(F32) * wk_ref[h]).astype(BF16)
        sf_ref[h] = a_ref[h][0:1, :] * sf_ref[h] + _dot_tn(kw, vh)
        mu = jnp.mean(y, axis=-1, keepdims=True)
        yc = y - mu
        yn = yc * lax.rsqrt(jnp.mean(yc * yc, axis=-1, keepdims=True) + EPS)
        gate = _silu(g_ref[:, pl.ds(ov, RET_DV)].astype(F32))
        o_ref[:, pl.ds(ov, RET_DV)] = (yn * gnw_ref[:, pl.ds(ov, RET_DV)] * gate).astype(o_ref.dtype)
        return carry

    lax.fori_loop(0, RET_HEADS, body, 0, unroll=8)


def _ret_scan(geo, proj, decays, gn_w):
    rows = geo.step_rows
    qk_w, v_w = RET_HEADS * RET_DK, RET_HEADS * RET_DV

    def bwd(b, s):
        return geo.step_block(b, geo.backward_order(s))

    def fwd(b, s):
        return geo.step_block(b, s)

    smem = pl.BlockSpec(memory_space=pltpu.SMEM)
    hist_shape = (geo.batch, geo.ns, RET_HEADS, RET_DK, RET_DV)
    hist = pl.pallas_call(
        functools.partial(_ret_state_kernel, rows=rows),
        grid=(geo.batch, geo.ns),
        in_specs=[
            smem,
            pl.BlockSpec((rows, qk_w), lambda b, s: (bwd(b, s), COL_K // qk_w)),
            pl.BlockSpec((rows, v_w), lambda b, s: (bwd(b, s), COL_V // v_w)),
        ],
        out_specs=pl.BlockSpec((None, None) + hist_shape[2:], lambda b, s: (b, geo.backward_order(s), 0, 0, 0)),
        out_shape=jax.ShapeDtypeStruct(hist_shape, BF16),
        scratch_shapes=[
            pltpu.VMEM((RET_HEADS, RET_DK, RET_DV), F32),
            pltpu.VMEM((RET_HEADS, rows, LANES), F32),
            pltpu.VMEM((RET_HEADS, 8, RET_DV), F32),
        ],
        compiler_params=_params(2),
        name="ret_state",
    )(decays, proj, proj)
    return pl.pallas_call(
        functools.partial(_ret_out_kernel, rows=rows),
        grid=(geo.batch, geo.ns),
        in_specs=[
            smem,
            pl.BlockSpec((rows, qk_w), lambda b, s: (fwd(b, s), COL_Q // qk_w)),
            pl.BlockSpec((rows, qk_w), lambda b, s: (fwd(b, s), COL_K // qk_w)),
            pl.BlockSpec((rows, v_w), lambda b, s: (fwd(b, s), COL_V // v_w)),
            pl.BlockSpec((rows, v_w), lambda b, s: (fwd(b, s), COL_G // v_w)),
            pl.BlockSpec((None, None) + hist_shape[2:], lambda b, s: (b, s, 0, 0, 0)),
            pl.BlockSpec((1, v_w), lambda b, s: (0, 0)),
        ],
        out_specs=pl.BlockSpec((rows, v_w), lambda b, s: (fwd(b, s), 0)),
        out_shape=jax.ShapeDtypeStruct((geo.rows, v_w), BF16),
        scratch_shapes=[
            pltpu.VMEM((RET_HEADS, RET_DK, RET_DV), F32),
            pltpu.VMEM((RET_HEADS, rows, LANES), F32),
            pltpu.VMEM((RET_HEADS, 8, RET_DV), F32),
            pltpu.VMEM((2, RET_HEADS, rows, LANES), F32),
            pltpu.VMEM((RET_HEADS, rows, rows), F32),
        ],
        compiler_params=_params(2),
        name="ret_out",
    )(decays, proj, proj, proj, proj, hist, gn_w)


def _group_cols(ref, rows, g):
    return pltpu.roll(ref[rows, :], (LANES - 16 * g) % LANES, 1)


def _pair_rhs(left, vp):
    zero = jnp.zeros_like(vp)
    return jnp.concatenate([jnp.where(left, vp, zero), jnp.where(left, zero, vp)], axis=0)


def _state_update(s_ref, g, sl, kt, w_rows, h1, h2, a_pair, rhs):
    lhs = jnp.concatenate([(kt * w_rows[h1:h1 + 1, :]).astype(BF16), (kt * w_rows[h2:h2 + 1, :]).astype(BF16)], axis=1)
    s_ref[g, :, sl] = a_pair * s_ref[g, :, sl] + _dot(lhs, rhs)


def _ssd_state_kernel(k_ref, xs_ref, cc_ref, cct_ref, cpt_ref, hist_ref, sb_ref, *, cps):
    @pl.when(pl.program_id(1) == 0)
    def _():
        sb_ref[...] = jnp.zeros_like(sb_ref)

    left = lax.broadcasted_iota(jnp.int32, (CHUNK, LANES), 1) < 64
    left_row = lax.broadcasted_iota(jnp.int32, (1, LANES), 1) < 64

    def body(g, carry):
        ok = pl.multiple_of(g * SSM_STATE, SSM_STATE)
        ov = pl.multiple_of(g * SSM_GROUP_W, SSM_GROUP_W)
        ob = pl.multiple_of(g * 16 + SSM_HPG, SSM_HPG)
        for ci in reversed(range(cps)):
            rows = pl.ds(ci * CHUNK, CHUNK)
            a_all = jnp.exp(_group_cols(cc_ref, rows, g)[0:1, :])
            kt = k_ref[rows, pl.ds(ok, SSM_STATE)].astype(F32).T
            hist_ref[ci, g] = sb_ref[g].astype(BF16)
            w_rows = jnp.exp(cct_ref[ci, pl.ds(ob, SSM_HPG), 0:1] - cpt_ref[ci, pl.ds(ob, SSM_HPG), :])
            for p in range(SSM_HPG // 2):
                h1, h2 = 2 * p, 2 * p + 1
                sl = slice(p * LANES, (p + 1) * LANES)
                vp = xs_ref[rows, pl.ds(pl.multiple_of(ov + p * LANES, LANES), LANES)]
                ab = jnp.where(left_row, a_all[:, SSM_HPG + h1:SSM_HPG + h1 + 1], a_all[:, SSM_HPG + h2:SSM_HPG + h2 + 1])
                _state_update(sb_ref, g, sl, kt, w_rows, h1, h2, ab, _pair_rhs(left, vp))
        return carry

    lax.fori_loop(0, SSM_GROUPS, body, 0, unroll=4)


def _ssd_out_kernel(q_ref, k_ref, xs_ref, z_ref, cc_ref, cct_ref, cpt_ref, dtt_ref, hist_ref,
                    dskip_ref, nw_ref, o_ref, sf_ref, *, cps):
    @pl.when(pl.program_id(1) == 0)
    def _():
        sf_ref[...] = jnp.zeros_like(sf_ref)

    left = lax.broadcasted_iota(jnp.int32, (CHUNK, LANES), 1) < 64
    left_row = lax.broadcasted_iota(jnp.int32, (1, LANES), 1) < 64
    ii = lax.broadcasted_iota(jnp.int32, (CHUNK, CHUNK), 0)
    jj = lax.broadcasted_iota(jnp.int32, (CHUNK, CHUNK), 1)
    lower = jj <= ii
    diag = jj == ii

    def body(g, carry):
        ok = pl.multiple_of(g * SSM_STATE, SSM_STATE)
        ov = pl.multiple_of(g * SSM_GROUP_W, SSM_GROUP_W)
        og = pl.multiple_of(g * 16, 16)
        for ci in range(cps):
            rows = pl.ds(ci * CHUNK, CHUNK)
            cc = _group_cols(cc_ref, rows, g)
            cpt = cpt_ref[ci, pl.ds(og, 16), :]
            dtt = dtt_ref[ci, pl.ds(og, 16), :]
            q = q_ref[rows, pl.ds(ok, SSM_STATE)]
            kt = k_ref[rows, pl.ds(ok, SSM_STATE)].astype(F32).T
            scores = _dot(q, kt.astype(BF16))
            yf = _dot(q, sf_ref[g].astype(BF16))
            yb = _dot(q, hist_ref[ci, g])
            w_rows = jnp.exp(cct_ref[ci, pl.ds(og, SSM_HPG), CHUNK - 1:CHUNK] - cpt[0:SSM_HPG, :])
            a_all = jnp.exp(cc[CHUNK - 1:CHUNK, :])
            ys = []
            for p in range(SSM_HPG // 2):
                h1, h2 = 2 * p, 2 * p + 1
                sl = slice(p * LANES, (p + 1) * LANES)
                cols = pl.ds(pl.multiple_of(ov + p * LANES, LANES), LANES)
                vp = xs_ref[rows, cols]
                probs, ef, eb = [], [], []
                for h in (h1, h2):
                    hb = SSM_HPG + h
                    cf = jnp.broadcast_to(cc[:, h:h + 1], (CHUNK, CHUNK))
                    cb = jnp.broadcast_to(cc[:, hb:hb + 1], (CHUNK, CHUNK))
                    arg = jnp.where(lower, cf - cpt[h:h + 1, :], cb - cpt[hb:hb + 1, :])
                    e = jnp.exp(arg) + jnp.where(diag, dtt[hb:hb + 1, :], 0.0)
                    probs.append((scores * e).astype(BF16))
                    ef.append(jnp.exp(cf))
                    eb.append(jnp.exp(cb))
                rhs = _pair_rhs(left, vp)
                y = _dot(jnp.concatenate(probs, axis=1), rhs)
                y = (y + jnp.where(left, ef[0], ef[1]) * yf[:, sl] + jnp.where(left, eb[0], eb[1]) * yb[:, sl])
                af = jnp.where(left_row, a_all[:, h1:h1 + 1], a_all[:, h2:h2 + 1])
                _state_update(sf_ref, g, sl, kt, w_rows, h1, h2, af, rhs)
                ys.append(y + dskip_ref[:, cols] * vp.astype(F32))
            gcols = pl.ds(ov, SSM_GROUP_W)
            y = jnp.concatenate(ys, axis=1) * _silu(z_ref[rows, gcols].astype(F32))
            y = y * lax.rsqrt(jnp.mean(y * y, axis=-1, keepdims=True) + EPS)
            o_ref[rows, gcols] = (y * nw_ref[:, gcols]).astype(o_ref.dtype)
        return carry

    lax.fori_loop(0, SSM_GROUPS, body, 0)


def _ssd_scan(geo, proj, xbc, cc, cct, cpt, dtt, dskip, norm_w):
    rows = geo.step_rows
    cps = rows // CHUNK
    bc_w = SSM_GROUPS * SSM_STATE

    def bwd(b, s):
        return geo.step_block(b, geo.backward_order(s))

    def fwd(b, s):
        return geo.step_block(b, s)

    hist_shape = (geo.batch, geo.ns * cps, SSM_GROUPS, SSM_STATE, SSM_GROUP_W)
    hist_block = (None, cps) + hist_shape[2:]
    row_bwd = pl.BlockSpec((cps, LANES, CHUNK), lambda b, s: (bwd(b, s), 0, 0))
    hist = pl.pallas_call(
        functools.partial(_ssd_state_kernel, cps=cps),
        grid=(geo.batch, geo.ns),
        in_specs=[
            pl.BlockSpec((rows, bc_w), lambda b, s: (bwd(b, s), SSM_INNER // bc_w)),
            pl.BlockSpec((rows, SSM_INNER), lambda b, s: (bwd(b, s), 0)),
            pl.BlockSpec((rows, LANES), lambda b, s: (bwd(b, s), 0)),
            row_bwd, row_bwd,
        ],
        out_specs=pl.BlockSpec(hist_block, lambda b, s: (b, geo.backward_order(s), 0, 0, 0)),
        out_shape=jax.ShapeDtypeStruct(hist_shape, BF16),
        scratch_shapes=[pltpu.VMEM((SSM_GROUPS, SSM_STATE, SSM_GROUP_W), F32)],
        compiler_params=_params(2),
        name="ssd_state",
    )(xbc, xbc, cc, cct, cpt)
    col = pl.BlockSpec((rows, LANES), lambda b, s: (fwd(b, s), 0))
    row = pl.BlockSpec((cps, LANES, CHUNK), lambda b, s: (fwd(b, s), 0, 0))
    return pl.pallas_call(
        functools.partial(_ssd_out_kernel, cps=cps),
        grid=(geo.batch, geo.ns),
        in_specs=[
            pl.BlockSpec((rows, bc_w), lambda b, s: (fwd(b, s), SSM_INNER // bc_w + 1)),
            pl.BlockSpec((rows, bc_w), lambda b, s: (fwd(b, s), SSM_INNER // bc_w)),
            pl.BlockSpec((rows, SSM_INNER), lambda b, s: (fwd(b, s), 0)),
            pl.BlockSpec((rows, SSM_INNER), lambda b, s: (fwd(b, s), COL_Z // SSM_INNER)),
            col, row, row, row,
            pl.BlockSpec(hist_block, lambda b, s: (b, s, 0, 0, 0)),
            pl.BlockSpec((1, SSM_INNER), lambda b, s: (0, 0)),
            pl.BlockSpec((1, SSM_INNER), lambda b, s: (0, 0)),
        ],
        out_specs=pl.BlockSpec((rows, SSM_INNER), lambda b, s: (fwd(b, s), 0)),
        out_shape=jax.ShapeDtypeStruct((geo.rows, SSM_INNER), BF16),
        scratch_shapes=[pltpu.VMEM((SSM_GROUPS, SSM_STATE, SSM_GROUP_W), F32)],
        compiler_params=_params(2),
        name="ssd_out",
    )(xbc, xbc, xbc, proj, cc, cct, cpt, dtt, hist, dskip, norm_w)


def _merge_kernel(yr_ref, ys_ref, wr_ref, ws_ref, gr_ref, gs_ref, o_ref):
    r = _dot(yr_ref[...], wr_ref[...])
    s = _dot(ys_ref[...], ws_ref[...])
    m = jax.nn.sigmoid(gr_ref[...].astype(F32)) * r + jax.nn.sigmoid(gs_ref[...].astype(F32)) * s
    o_ref[...] = m.astype(o_ref.dtype)


def _merge(geo, yr, ys, w_ret, w_ssm, proj):
    tm, tn = geo.tm, 512
    gr0 = COL_GATES // tn
    gs0 = (COL_GATES + D_MODEL) // tn
    return pl.pallas_call(
        _merge_kernel,
        grid=(geo.n_tiles, D_MODEL // tn),
        in_specs=[
            pl.BlockSpec((tm, yr.shape[1]), lambda i, j: (i, 0)),
            pl.BlockSpec((tm, ys.shape[1]), lambda i, j: (i, 0)),
            pl.BlockSpec((yr.shape[1], tn), lambda i, j: (0, j)),
            pl.BlockSpec((ys.shape[1], tn), lambda i, j: (0, j)),
            pl.BlockSpec((tm, tn), lambda i, j: (i, gr0 + j)),
            pl.BlockSpec((tm, tn), lambda i, j: (i, gs0 + j)),
        ],
        out_specs=pl.BlockSpec((tm, tn), lambda i, j: (i, j)),
        out_shape=jax.ShapeDtypeStruct((geo.rows, D_MODEL), BF16),
        compiler_params=_params(2),
        name="branch_merge",
    )(yr, ys, w_ret, w_ssm, proj, proj)


def _residual_kernel(a_ref, w_ref, x_ref, mod_ref, o_ref, *, gate_row):
    o_ref[...] = x_ref[...] + mod_ref[gate_row:gate_row + 1, :] * _dot(a_ref[...], w_ref[...])


def _matmul_residual(geo, a, w, xs, mod_l, gate_row, tile0, n_tiles, tn):
    tm = geo.tm
    kdim = a.shape[1]
    return pl.pallas_call(
        functools.partial(_residual_kernel, gate_row=gate_row),
        grid=(n_tiles, D_MODEL // tn),
        in_specs=[
            pl.BlockSpec((tm, kdim), lambda i, j: (i, 0)),
            pl.BlockSpec((kdim, tn), lambda i, j: (0, j)),
            pl.BlockSpec((tm, tn), lambda i, j: (tile0 + i, j)),
            pl.BlockSpec((None, 6, tn), lambda i, j: (geo.mod_row(tile0 + i), 0, j)),
        ],
        out_specs=pl.BlockSpec((tm, tn), lambda i, j: (i, j)),
        out_shape=jax.ShapeDtypeStruct((n_tiles * tm, D_MODEL), F32),
        compiler_params=_params(2),
        name="proj_residual",
    )(a, w, xs, mod_l)


def _ffn_up_kernel(x_ref, mod_ref, nw_ref, wg_ref, wu_ref, o_ref, h_ref):
    @pl.when(pl.program_id(1) == 0)
    def _():
        h_ref[...] = _norm_mod(x_ref[...], nw_ref[...], mod_ref[3:4, :], mod_ref[4:5, :]).astype(BF16)

    h = h_ref[...]
    o_ref[...] = (_silu(_dot(h, wg_ref[...])) * _dot(h, wu_ref[...])).astype(o_ref.dtype)


def _ffn_up(geo, xs, mod_l, nw, w_gate, w_up):
    tm, tf = geo.tm, 512
    return pl.pallas_call(
        _ffn_up_kernel,
        grid=(geo.n_tiles, D_FF // tf),
        in_specs=[
            pl.BlockSpec((tm, D_MODEL), lambda i, j: (i, 0)),
            pl.BlockSpec((None, 6, D_MODEL), lambda i, j: (geo.mod_row(i), 0, 0)),
            pl.BlockSpec((1, D_MODEL), lambda i, j: (0, 0)),
            pl.BlockSpec((D_MODEL, tf), lambda i, j: (0, j)),
            pl.BlockSpec((D_MODEL, tf), lambda i, j: (0, j)),
        ],
        out_specs=pl.BlockSpec((tm, tf), lambda i, j: (i, j)),
        out_shape=jax.ShapeDtypeStruct((geo.rows, D_FF), BF16),
        scratch_shapes=[pltpu.VMEM((tm, D_MODEL), BF16)],
        compiler_params=_params(2),
        name="ffn_up",
    )(xs, mod_l, nw, w_gate, w_up)


MOE_ROW_TILE = 512
MOE_TOKEN_BLOCK = 512
ROUTE_W1, ROUTE_W2, ROUTE_E1, ROUTE_E2 = 8, 9, 10, 11


def _route_kernel(x_ref, mod_ref, nw_ref, router_ref, h_ref, route_ref, cnt_ref, tri_ref, carry_ref):
    i = pl.program_id(0)
    tm = x_ref.shape[0]
    lane = lax.broadcasted_iota(jnp.int32, (tm, LANES), 1)

    @pl.when(i == 0)
    def _():
        ii = lax.broadcasted_iota(jnp.int32, (tm, tm), 0)
        jj = lax.broadcasted_iota(jnp.int32, (tm, tm), 1)
        tri_ref[...] = (jj < ii).astype(BF16)
        carry_ref[...] = jnp.zeros_like(carry_ref)

    h = _norm_mod(x_ref[...], nw_ref[...], mod_ref[3:4, :], mod_ref[4:5, :])
    h_hi, h_lo = _split2(h)
    h_ref[...] = h_hi
    r_hi, r_lo = _split2(router_ref[...])
    logits = _dot(h_hi, r_hi) + _dot(h_hi, r_lo) + _dot(h_lo, r_hi)
    neg = -jnp.inf
    lg = jnp.where(lane < N_EXPERTS, logits, neg)
    m1 = jnp.max(lg, axis=-1, keepdims=True)
    i1 = jnp.min(jnp.where(lg == m1, lane, LANES), axis=-1, keepdims=True)
    lg2 = jnp.where(lane == i1, neg, lg)
    m2 = jnp.max(lg2, axis=-1, keepdims=True)
    i2 = jnp.min(jnp.where(lg2 == m2, lane, LANES), axis=-1, keepdims=True)
    e2 = jnp.exp(m2 - m1)
    w1 = 1.0 / (1.0 + e2)
    w2 = e2 / (1.0 + e2)
    chosen = jnp.logical_or(lane == i1, lane == i2)
    mask = jnp.where(chosen, 1.0, 0.0)
    pos = _dot(tri_ref[...], mask.astype(BF16)) + carry_ref[0:1, :]
    carry_ref[0:1, :] = carry_ref[0:1, :] + jnp.sum(mask, axis=0, keepdims=True)
    rec = jnp.where(lane < N_EXPERTS, pos, 0.0)
    rec = jnp.where(lane == ROUTE_W1, w1, rec)
    rec = jnp.where(lane == ROUTE_W2, w2, rec)
    rec = jnp.where(lane == ROUTE_E1, i1.astype(F32), rec)
    rec = jnp.where(lane == ROUTE_E2, i2.astype(F32), rec)
    route_ref[...] = rec
    cnt_ref[...] = carry_ref[...]


def _slots_kernel(route_ref, start_ref, o_ref):
    rec = route_ref[...]
    lane = lax.broadcasted_iota(jnp.int32, rec.shape, 1)
    slot = rec + start_ref[...]
    e1 = rec[:, ROUTE_E1:ROUTE_E1 + 1].astype(jnp.int32)
    e2 = rec[:, ROUTE_E2:ROUTE_E2 + 1].astype(jnp.int32)
    d1 = jnp.sum(jnp.where(lane == e1, slot, 0.0), axis=-1, keepdims=True)
    d2 = jnp.sum(jnp.where(lane == e2, slot, 0.0), axis=-1, keepdims=True)
    o_ref[...] = jnp.where(lane == 0, d1, jnp.where(lane == 1, d2, 0.0)).astype(jnp.int32)


def _dispatch_kernel(rp_ref, sbp_ref, fp_ref, np_ref, dt_ref, h_ref, o_ref):
    p = pl.program_id(0)
    tg, tb = o_ref.shape[0], h_ref.shape[0]

    @pl.when(p < np_ref[0])
    def _():
        slot = rp_ref[p] * tg + lax.broadcasted_iota(jnp.int32, (tg, tb), 0)
        d = dt_ref[...]
        hit = jnp.logical_or(d[0:1, :] == slot, d[1:2, :] == slot)
        sel = jnp.where(hit, 1.0, 0.0).astype(BF16)
        rows = _dot(sel, h_ref[...])

        @pl.when(fp_ref[p] == 1)
        def _():
            o_ref[...] = rows.astype(o_ref.dtype)

        @pl.when(fp_ref[p] == 0)
        def _():
            o_ref[...] = (o_ref[...].astype(F32) + rows).astype(o_ref.dtype)


def _expert_up_kernel(te_ref, nu_ref, x_ref, wg_ref, wu_ref, o_ref):
    del te_ref
    live = pl.program_id(1) < nu_ref[0]

    @pl.when(live)
    def _():
        x = x_ref[...]
        o_ref[...] = (_silu(_dot(x, wg_ref[...])) * _dot(x, wu_ref[...])).astype(o_ref.dtype)

    @pl.when(jnp.logical_not(live))
    def _():
        o_ref[...] = jnp.zeros_like(o_ref)


def _expert_down_kernel(te_ref, nu_ref, h_ref, wd_ref, o_ref):
    del te_ref
    live = pl.program_id(1) < nu_ref[0]

    @pl.when(live)
    def _():
        o_ref[...] = _dot(h_ref[...], wd_ref[...]).astype(o_ref.dtype)

    @pl.when(jnp.logical_not(live))
    def _():
        o_ref[...] = jnp.zeros_like(o_ref)


def _collect_kernel(sbp_ref, rp_ref, fp_ref, lp_ref, np_ref, x_ref, dest_ref, route_ref, y_ref, mod_ref, o_ref):
    p = pl.program_id(0)
    tb, tg = o_ref.shape[0], y_ref.shape[0]

    @pl.when(p < np_ref[0])
    def _():
        slot = rp_ref[p] * tg + lax.broadcasted_iota(jnp.int32, (tb, tg), 1)
        dest = dest_ref[...]
        rec = route_ref[...]
        sel = (jnp.where(dest[:, 0:1] == slot, rec[:, ROUTE_W1:ROUTE_W1 + 1], 0.0)
               + jnp.where(dest[:, 1:2] == slot, rec[:, ROUTE_W2:ROUTE_W2 + 1], 0.0))
        part = _dot(sel.astype(BF16), y_ref[...])

        @pl.when(fp_ref[p] == 1)
        def _():
            o_ref[...] = part

        @pl.when(fp_ref[p] == 0)
        def _():
            o_ref[...] += part

        @pl.when(lp_ref[p] == 1)
        def _():
            o_ref[...] = x_ref[...] + mod_ref[5:6, :] * o_ref[...]


def _moe(geo, xs, mod_l, nw, router, w_gate, w_up, w_down, tile0, n_tiles):
    tm, tg = geo.tm, MOE_ROW_TILE
    tb = min(MOE_TOKEN_BLOCK, tm)
    rows = n_tiles * tm
    nb = rows // tb
    n_slots = 2 * rows + N_EXPERTS * tg
    n_gt = n_slots // tg
    n_pairs_max = n_gt + nb * N_EXPERTS
    rec_tile = pl.BlockSpec((tm, LANES), lambda i: (i, 0))

    h2, route, counts = pl.pallas_call(
        _route_kernel,
        grid=(n_tiles,),
        in_specs=[
            pl.BlockSpec((tm, D_MODEL), lambda i: (tile0 + i, 0)),
            pl.BlockSpec((None, 6, D_MODEL), lambda i: (geo.mod_row(tile0 + i), 0, 0)),
            pl.BlockSpec((1, D_MODEL), lambda i: (0, 0)),
            pl.BlockSpec((D_MODEL, LANES), lambda i: (0, 0)),
        ],
        out_specs=[pl.BlockSpec((tm, D_MODEL), lambda i: (i, 0)), rec_tile, pl.BlockSpec((8, LANES), lambda i: (0, 0))],
        out_shape=[
            jax.ShapeDtypeStruct((rows, D_MODEL), BF16),
            jax.ShapeDtypeStruct((rows, LANES), F32),
            jax.ShapeDtypeStruct((8, LANES), F32),
        ],
        scratch_shapes=[pltpu.VMEM((tm, tm), BF16), pltpu.VMEM((8, LANES), F32)],
        compiler_params=_params(1),
        name="moe_route",
    )(xs, mod_l, nw, router)

    cnt = counts[0, :N_EXPERTS].astype(jnp.int32)
    padded = ((cnt + tg - 1) // tg) * tg
    ends = jnp.cumsum(padded)
    starts = ends - padded
    n_used = (ends[-1] // tg).astype(jnp.int32).reshape(1)
    tile_expert = jnp.searchsorted(ends, jnp.minimum(jnp.arange(n_gt), n_used[0] - 1) * tg, side="right")
    tile_expert = jnp.minimum(tile_expert, N_EXPERTS - 1).astype(jnp.int32)
    start_row = jnp.zeros((1, LANES), F32).at[0, :N_EXPERTS].set(starts.astype(F32))

    dest = pl.pallas_call(
        _slots_kernel,
        grid=(n_tiles,),
        in_specs=[rec_tile, pl.BlockSpec((1, LANES), lambda i: (0, 0))],
        out_specs=rec_tile,
        out_shape=jax.ShapeDtypeStruct((rows, LANES), jnp.int32),
        compiler_params=_params(1),
        name="moe_slots",
    )(route, start_row)

    pos_lo = route.reshape(nb, tb, LANES)[:, 0, :N_EXPERTS].astype(jnp.int32)
    pos_hi = jnp.concatenate([pos_lo[1:], cnt[None, :]], axis=0)
    t_lo = (starts[None, :] + pos_lo) // tg
    t_hi = (starts[None, :] + pos_hi - 1) // tg
    tiles = jnp.arange(n_gt)[:, None, None]
    share = jnp.any((pos_hi > pos_lo)[None] & (t_lo[None] <= tiles) & (tiles <= t_hi[None]), axis=-1)
    n_pairs = jnp.sum(share).astype(jnp.int32).reshape(1)
    last_valid = jnp.minimum(jnp.arange(n_pairs_max), n_pairs[0] - 1)

    def pair_list(mat):
        flat = jnp.nonzero(mat.ravel(), size=n_pairs_max, fill_value=0)[0][last_valid]
        major, minor = (flat // mat.shape[1]).astype(jnp.int32), (flat % mat.shape[1]).astype(jnp.int32)
        first = jnp.concatenate([jnp.ones((1,), jnp.int32), (major[1:] != major[:-1]).astype(jnp.int32)])
        last = jnp.concatenate([(major[1:] != major[:-1]).astype(jnp.int32), jnp.ones((1,), jnp.int32)])
        last = jnp.where(jnp.arange(n_pairs_max) == n_pairs[0] - 1, 1, last)
        return major, minor, first, last

    d_r, d_sb, d_first, _ = pair_list(share)
    c_sb, c_r, c_first, c_last = pair_list(share.T)
    dest_t = dest[:, :2].T

    xg = pl.pallas_call(
        _dispatch_kernel,
        grid_spec=pltpu.PrefetchScalarGridSpec(
            num_scalar_prefetch=4,
            grid=(n_pairs_max,),
            in_specs=[
                pl.BlockSpec((2, tb), lambda p, rp, sbp, fp, npr: (0, sbp[p])),
                pl.BlockSpec((tb, D_MODEL), lambda p, rp, sbp, fp, npr: (sbp[p], 0)),
            ],
            out_specs=pl.BlockSpec((tg, D_MODEL), lambda p, rp, sbp, fp, npr: (rp[p], 0)),
        ),
        out_shape=jax.ShapeDtypeStruct((n_slots, D_MODEL), BF16),
        compiler_params=_params(1),
        name="moe_dispatch",
    )(d_r, d_sb, d_first, n_pairs, dest_t, h2)

    tf = 1024
    hg = pl.pallas_call(
        _expert_up_kernel,
        grid_spec=pltpu.PrefetchScalarGridSpec(
            num_scalar_prefetch=2,
            grid=(MOE_D_FF // tf, n_gt),
            in_specs=[
                pl.BlockSpec((tg, D_MODEL), lambda j, r, te, nu: (r, 0)),
                pl.BlockSpec((None, D_MODEL, tf), lambda j, r, te, nu: (te[r], 0, j)),
                pl.BlockSpec((None, D_MODEL, tf), lambda j, r, te, nu: (te[r], 0, j)),
            ],
            out_specs=pl.BlockSpec((tg, tf), lambda j, r, te, nu: (r, j)),
        ),
        out_shape=jax.ShapeDtypeStruct((n_slots, MOE_D_FF), BF16),
        compiler_params=_params(2),
        name="moe_expert_up",
    )(tile_expert, n_used, xg, w_gate, w_up)

    tn = 1024
    yg = pl.pallas_call(
        _expert_down_kernel,
        grid_spec=pltpu.PrefetchScalarGridSpec(
            num_scalar_prefetch=2,
            grid=(D_MODEL // tn, n_gt),
            in_specs=[
                pl.BlockSpec((tg, MOE_D_FF), lambda j, r, te, nu: (r, 0)),
                pl.BlockSpec((None, MOE_D_FF, tn), lambda j, r, te, nu: (te[r], 0, j)),
            ],
            out_specs=pl.BlockSpec((tg, tn), lambda j, r, te, nu: (r, j)),
        ),
        out_shape=jax.ShapeDtypeStruct((n_slots, D_MODEL), BF16),
        compiler_params=_params(2),
        name="moe_expert_down",
    )(tile_expert, n_used, hg, w_down)

    per = tm // tb
    return pl.pallas_call(
        _collect_kernel,
        grid_spec=pltpu.PrefetchScalarGridSpec(
            num_scalar_prefetch=5,
            grid=(n_pairs_max,),
            in_specs=[
                pl.BlockSpec((tb, D_MODEL), lambda p, sbp, rp, fp, lp, npr: (tile0 * per + sbp[p], 0)),
                pl.BlockSpec((tb, LANES), lambda p, sbp, rp, fp, lp, npr: (sbp[p], 0)),
                pl.BlockSpec((tb, LANES), lambda p, sbp, rp, fp, lp, npr: (sbp[p], 0)),
                pl.BlockSpec((tg, D_MODEL), lambda p, sbp, rp, fp, lp, npr: (rp[p], 0)),
                pl.BlockSpec((None, 6, D_MODEL), lambda p, sbp, rp, fp, lp, npr: (geo.mod_row(tile0 + sbp[p] // per), 0, 0)),
            ],
            out_specs=pl.BlockSpec((tb, D_MODEL), lambda p, sbp, rp, fp, lp, npr: (sbp[p], 0)),
        ),
        out_shape=jax.ShapeDtypeStruct((rows, D_MODEL), F32),
        compiler_params=_params(1),
        name="moe_collect",
    )(c_sb, c_r, c_first, c_last, n_pairs, xs, dest, route, yg, mod_l)


def _final_norm_kernel(x_ref, w_ref, o_ref):
    x = x_ref[...]
    o_ref[...] = x * lax.rsqrt(jnp.mean(x * x, axis=-1, keepdims=True) + EPS) * w_ref[...]


def _final_norm(x, w, tm):
    rows = x.shape[0]
    return pl.pallas_call(
        _final_norm_kernel,
        grid=(rows // tm,),
        in_specs=[pl.BlockSpec((tm, D_MODEL), lambda i: (i, 0)), pl.BlockSpec((1, D_MODEL), lambda i: (0, 0))],
        out_specs=pl.BlockSpec((tm, D_MODEL), lambda i: (i, 0)),
        out_shape=jax.ShapeDtypeStruct((rows, D_MODEL), F32),
        compiler_params=_params(1),
        name="final_norm",
    )(x, w)


def _rope_tables(geo):
    half = RET_DK // 4
    inv = ROPE_BASE ** (-jnp.arange(half, dtype=F32) / half)
    pos = jnp.arange(geo.seq)
    ang_r = (pos // GRID_W).astype(F32)[:, None] * inv[None, :]
    ang_c = (pos % GRID_W).astype(F32)[:, None] * inv[None, :]
    cos = jnp.concatenate([jnp.cos(ang_r), jnp.cos(ang_r), jnp.cos(ang_c), jnp.cos(ang_c)], axis=1)
    sin = jnp.concatenate([-jnp.sin(ang_r), jnp.sin(ang_r), -jnp.sin(ang_c), jnp.sin(ang_c)], axis=1)
    cos = jnp.concatenate([jnp.ones((geo.tm, LANES), F32), cos], axis=0)
    sin = jnp.concatenate([jnp.zeros((geo.tm, LANES), F32), sin], axis=0)
    return cos, sin


_DT_PERM = np.array([d * SSM_HEADS + g * SSM_HPG + h
                     for g in range(SSM_GROUPS) for d in range(2) for h in range(SSM_HPG)])


def kernel(x, c, ctx, c_ctx, w_ada, b_ada, norm1_w, norm2_w, w_in, conv_w, conv_b, ret_decay_f, ret_decay_b, ret_gn_w, ssm_a_log_f, ssm_a_log_b, ssm_dt_bias_f, ssm_dt_bias_b, ssm_d, ssm_norm_w, w_ret_proj, w_ssm_proj, w_out, ffn_w_gate, ffn_w_up, ffn_w_down, moe_router, moe_w_gate, moe_w_up, moe_w_down, final_norm_w):
    batch, seq, d = x.shape
    ctx_len = ctx.shape[1]
    depth = w_ada.shape[0]
    assert d == D_MODEL and seq % GRID_W == 0
    geo = _Geom(batch, ctx_len, seq)
    tm = geo.tm

    mod_rows = -(-(batch + 1) // 8) * 8
    cvec = jnp.zeros((mod_rows, d), F32).at[0].set(c_ctx).at[1:batch + 1].set(c)
    mod = _modulation(cvec, w_ada, b_ada).reshape(depth, mod_rows, 6, d)
    cos_t, sin_t = _rope_tables(geo)

    xs = jnp.concatenate([ctx.reshape(batch * ctx_len, d), x.reshape(batch * seq, d)], axis=0)
    lat_tile0 = geo.n_ctx_tiles
    n_lat_tiles = geo.n_tiles - geo.n_ctx_tiles

    for i in range(depth):
        mod_l = mod[i]
        w_main = jnp.concatenate([w_in[i][:, ORIG_Z_LO:ORIG_Z_HI], w_in[i][:, :ORIG_Z_LO],
                                  w_in[i][:, ORIG_Z_HI:ORIG_DT_LO], w_in[i][:, ORIG_DT_HI:]], axis=1).astype(BF16)
        w_dt = w_in[i][:, ORIG_DT_LO:ORIG_DT_HI][:, _DT_PERM].astype(BF16)
        proj, dt = _inproj(geo, xs, mod_l, norm1_w[i][None, :], w_main, w_dt, cos_t, sin_t)

        xbc = _conv_silu(geo, proj, conv_w[i], conv_b[i][None, :])
        bias = jnp.concatenate([ssm_dt_bias_f[i], ssm_dt_bias_b[i]])[_DT_PERM][None, :]
        alog = jnp.concatenate([ssm_a_log_f[i], ssm_a_log_b[i]])[_DT_PERM][None, :]
        cc, cct, cpt, dtt = _decay_tables(geo, dt, bias, alog)
        dskip = jnp.repeat(ssm_d[i], SSM_INNER // SSM_HEADS)[None, :]
        ys = _ssd_scan(geo, proj, xbc, cc, cct, cpt, dtt, dskip, ssm_norm_w[i][None, :])
        yr = _ret_scan(geo, proj, jnp.stack([ret_decay_f[i], ret_decay_b[i]]), ret_gn_w[i][None, :])

        merged = _merge(geo, yr, ys, w_ret_proj[i].astype(BF16), w_ssm_proj[i].astype(BF16), proj)
        xs = _matmul_residual(geo, merged, w_out[i].astype(BF16), xs, mod_l, 2, 0, geo.n_tiles, 1024)

        j = i // 2
        tile0, n_tiles = (0, geo.n_tiles) if i < depth - 1 else (lat_tile0, n_lat_tiles)
        if i % 2 == 0:
            hid = _ffn_up(geo, xs, mod_l, norm2_w[i][None, :], ffn_w_gate[j].astype(BF16), ffn_w_up[j].astype(BF16))
            hid = hid[tile0 * tm:]
            xs = _matmul_residual(geo, hid, ffn_w_down[j].astype(BF16), xs, mod_l, 5, tile0, n_tiles, 512)
        else:
            router = jnp.zeros((d, LANES), F32).at[:, :N_EXPERTS].set(moe_router[j])
            xs = _moe(geo, xs, mod_l, norm2_w[i][None, :], router, moe_w_gate[j].astype(BF16),
                      moe_w_up[j].astype(BF16), moe_w_down[j].astype(BF16), tile0, n_tiles)

    lat = xs if xs.shape[0] == batch * seq else xs[batch * ctx_len:]
    return _final_norm(lat, final_norm_w[None, :], tm).reshape(batch, seq, d)
```
